```python
import math
import jax, jax.numpy as jnp
from jax import lax
import numpy as np

D_MODEL = 2048
BATCH = 1
SEQ = 8192
DEPTH = 4

GRID_W = 64
CTX_LEN = 256
N_MIXERS = 3
RMS_EPS = 1e-6
CONV_WIDTH = 3
DIFF_HEAD_DIM = 128
DIFF_HEADS = D_MODEL // (2 * DIFF_HEAD_DIM)
DIFF_SUBLN_EPS = 1e-5
DIFF_LAMBDA_A = 0.8
DIFF_LAMBDA_B = 0.6
DIFF_LAMBDA_C = 0.3
ROPE_THETA = 10000.0
QUERY_BLOCK = 128
GLA_HEADS = 4
GLA_DK = D_MODEL // (2 * GLA_HEADS)
GLA_DV = D_MODEL // GLA_HEADS
GLA_GATE_RANK = 16
GLA_TAU = 16.0
GLA_CHUNK = 64
N_EXPERTS = 16
EXPERT_FF = (3 * D_MODEL) // 4
CAPACITY_FACTOR = 2

kernel_name = 'hybrid_dit_shortconv_diffattn_gla_ecmoe'


def _n_layers_of(kind):
    return sum(1 for i in range(DEPTH) if i % N_MIXERS == kind)


def _rmsnorm(x, g, eps=RMS_EPS):
    xf = x.astype(jnp.float32)
    y = xf * lax.rsqrt(jnp.mean(xf * xf, axis=-1, keepdims=True) + eps)
    return (y * g.astype(jnp.float32)).astype(x.dtype)


def _modulation(cvec, w, b):
    return jnp.split(jax.nn.silu(cvec) @ w + b, 6, axis=-1)


def _modulate(h, shift, scale):
    return h * (1 + scale) + shift


def _short_conv(h, w_in, w_dw, w_out):
    L = h.shape[1]
    b_gate, c_gate, u = jnp.split(h @ w_in, 3, axis=-1)
    pad = CONV_WIDTH // 2
    v = jnp.pad(c_gate * u, ((0, 0), (pad, pad), (0, 0)))
    conv = v[:, 0:L] * w_dw[0]
    for k in range(1, CONV_WIDTH):
        conv = conv + v[:, k:k + L] * w_dw[k]
    return (b_gate * conv) @ w_out


def _rope_half(x, pos):
    half = x.shape[-1] // 2
    inv = ROPE_THETA ** (-jnp.arange(half, dtype=jnp.float32) / half)
    ang = pos.astype(jnp.float32)[:, None] * inv
    shape = (1, ang.shape[0]) + (1,) * (x.ndim - 3) + (half,)
    cos = jnp.cos(ang).reshape(shape)
    sin = jnp.sin(ang).reshape(shape)
    x1 = x[..., :half].astype(jnp.float32)
    x2 = x[..., half:].astype(jnp.float32)
    return jnp.concatenate([x1 * cos - x2 * sin, x2 * cos + x1 * sin], axis=-1).astype(x.dtype)


def _rope_2d(x, row, col):
    r = x.shape[-1] // 2
    return jnp.concatenate([_rope_half(x[..., :r], row), _rope_half(x[..., r:], col)], axis=-1)


def _diff_softmax_mix(q, keys, vals, lam):
    s = jnp.einsum('bqhtd,bkhtd->bhtqk', q, keys).astype(jnp.float32) * (DIFF_HEAD_DIM ** -0.5)
    p = jax.nn.softmax(s, axis=-1)
    w = p[:, :, 0] - lam * p[:, :, 1]
    return jnp.einsum('bhqk,bkhe->bqhe', w, vals.astype(jnp.float32))


def _diff_attention(h_lat, h_ctx, row, col, w_qkv, lam_params, subln, w_out, lambda_init, with_ctx_out):
    def project(h):
        B, L, _ = h.shape
        q, k, v = jnp.split(h @ w_qkv, 3, axis=-1)
        return (q.reshape(B, L, DIFF_HEADS, 2, DIFF_HEAD_DIM),
                k.reshape(B, L, DIFF_HEADS, 2, DIFF_HEAD_DIM),
                v.reshape(B, L, DIFF_HEADS, 2 * DIFF_HEAD_DIM))

    ql, kl, vl = project(h_lat)
    ql = _rope_2d(ql, row, col)
    kl = _rope_2d(kl, row, col)
    qc, kc, vc = project(h_ctx)
    lf = lam_params.astype(jnp.float32)
    lam = jnp.exp(jnp.sum(lf[0] * lf[1])) - jnp.exp(jnp.sum(lf[2] * lf[3])) + lambda_init
    keys = jnp.concatenate([kc, kl], axis=1)
    vals = jnp.concatenate([vc, vl], axis=1)
    B, L = ql.shape[0], ql.shape[1]
    nb = L // QUERY_BLOCK
    qb = jnp.moveaxis(ql.reshape(B, nb, QUERY_BLOCK, DIFF_HEADS, 2, DIFF_HEAD_DIM), 1, 0)
    ol = lax.map(lambda q: _diff_softmax_mix(q, keys, vals, lam), qb)
    ol = jnp.moveaxis(ol, 0, 1).reshape(B, L, DIFF_HEADS, 2 * DIFF_HEAD_DIM)

    def finish(o):
        o = _rmsnorm(o, subln, DIFF_SUBLN_EPS) * (1.0 - lambda_init)
        return o.reshape(o.shape[0], o.shape[1], -1).astype(h_lat.dtype) @ w_out

    y_lat = finish(ol)
    y_ctx = finish(_diff_softmax_mix(qc, kc, vc, lam)) if with_ctx_out else None
    return y_lat, y_ctx


def _gla_log_gate(h, w1, w2, b):
    B, L, _ = h.shape
    z = ((h @ w1) @ w2 + b).astype(jnp.float32)
    return (jax.nn.log_sigmoid(z) / GLA_TAU).reshape(B, L, GLA_HEADS, GLA_DK)


def _gla_chunk_scan(q, k, v, log_a, s0):
    B, L, H, dk = q.shape
    dv = v.shape[-1]
    n = L // GLA_CHUNK

    def chunks(t):
        return jnp.moveaxis(t.reshape(B, n, GLA_CHUNK, H, t.shape[-1]), 1, 0)

    causal = jnp.tril(jnp.ones((GLA_CHUNK, GLA_CHUNK), dtype=bool))[None, :, :, None, None]

    def step(S, inp):
        qc, kc, vc, gc = inp
        b = jnp.cumsum(gc, axis=1)
        rel = jnp.where(causal, b[:, :, None] - b[:, None, :], -jnp.inf)
        a = jnp.sum(qc[:, :, None] * kc[:, None, :] * jnp.exp(rel), axis=-1)
        o = (jnp.einsum('bijh,bjhe->bihe', a, vc)
             + jnp.einsum('bihd,bhde->bihe', qc * jnp.exp(b), S))
        b_end = b[:, -1]
        S = (jnp.exp(b_end)[..., None] * S
             + jnp.einsum('bjhd,bjhe->bhde', kc * jnp.exp(b_end[:, None] - b), vc))
        return S, o

    s_fin, o = lax.scan(step, s0, (chunks(q), chunks(k), chunks(v), chunks(log_a)))
    return jnp.moveaxis(o, 0, 1).reshape(B, L, H, dv), s_fin


def _gla(h_lat, h_ctx, w_in, gate_w1, gate_w2, gate_b, onorm, w_out, with_ctx_out):
    splits = [GLA_HEADS * GLA_DK, 2 * GLA_HEADS * GLA_DK, 2 * GLA_HEADS * GLA_DK + GLA_HEADS * GLA_DV]

    def project(h):
        B, L, _ = h.shape
        q, k, v, g = jnp.split(h @ w_in, splits, axis=-1)
        q = q.reshape(B, L, GLA_HEADS, GLA_DK).astype(jnp.float32) * (GLA_DK ** -0.5)
        k = k.reshape(B, L, GLA_HEADS, GLA_DK).astype(jnp.float32)
        v = v.reshape(B, L, GLA_HEADS, GLA_DV).astype(jnp.float32)
        la_f = _gla_log_gate(h, gate_w1[0], gate_w2[0], gate_b[0])
        la_b = _gla_log_gate(h, gate_w1[1], gate_w2[1], gate_b[1])
        return q, k, v, g, la_f, la_b

    def rev(t):
        return t[:, ::-1]

    qc, kc, vc, gc, lcf, lcb = project(h_ctx)
    ql, kl, vl, gl, llf, llb = project(h_lat)
    s0 = jnp.zeros((h_lat.shape[0], GLA_HEADS, GLA_DK, GLA_DV), jnp.float32)
    oc_f, s_cf = _gla_chunk_scan(qc, kc, vc, lcf, s0)
    oc_b, s_cb = _gla_chunk_scan(rev(qc), rev(kc), rev(vc), rev(lcb), s0)
    ol_f, _ = _gla_chunk_scan(ql, kl, vl, llf, s_cf)
    ol_b, _ = _gla_chunk_scan(rev(ql), rev(kl), rev(vl), rev(llb), s_cb)

    def finish(o, g):
        B, L = o.shape[0], o.shape[1]
        o = _rmsnorm(o, onorm) * jax.nn.silu(g.astype(jnp.float32)).reshape(B, L, GLA_HEADS, GLA_DV)
        return o.reshape(B, L, -1).astype(h_lat.dtype) @ w_out

    y_lat = finish(ol_f + rev(ol_b), gl)
    y_ctx = finish(oc_f + rev(oc_b), gc) if with_ctx_out else None
    return y_lat, y_ctx


def _ec_moe(h, router_w, w_gate, w_up, w_down):
    B, N, D = h.shape
    cap = max(1, (CAPACITY_FACTOR * N) // N_EXPERTS)
    affinity = jax.nn.softmax((h @ router_w).astype(jnp.float32), axis=-1)
    gate, idx = lax.top_k(jnp.swapaxes(affinity, 1, 2), cap)
    xin = jax.vmap(lambda hb, ib: hb[ib])(h, idx)
    a = jnp.einsum('becd,edf->becf', xin, w_gate)
    u = jnp.einsum('becd,edf->becf', xin, w_up)
    y = jnp.einsum('becf,efd->becd', jax.nn.silu(a) * u, w_down) * gate[..., None].astype(h.dtype)

    def scatter(yb, ib):
        return jnp.zeros((N, D), h.dtype).at[ib.reshape(-1)].add(yb.reshape(-1, D))

    return jax.vmap(scatter)(y, idx)


def setup_inputs(seed: int = 0) -> dict:
    key = jax.random.key(seed)
    ks = iter(jax.random.split(key, 40))

    def nrm(shape, scale):
        return jax.random.normal(next(ks), shape, jnp.float32) * scale

    D = D_MODEL
    nA, nB, nC = _n_layers_of(0), _n_layers_of(1), _n_layers_of(2)
    return {
        'x': nrm((BATCH, SEQ, D), 1.0),
        'c': nrm((BATCH, D), 1.0),
        'ctx': nrm((BATCH, CTX_LEN, D), 1.0),
        'c_ctx': nrm((D,), 1.0),
        'mod_w': nrm((DEPTH, D, 6 * D), 0.5 * D ** -0.5),
        'mod_b': nrm((DEPTH, 6 * D), 0.02),
        'norm_mix': 1.0 + nrm((DEPTH, D), 0.02),
        'norm_ffn': 1.0 + nrm((DEPTH, D), 0.02),
        'conv_w_in': nrm((nA, D, 3 * D), D ** -0.5),
        'conv_w_dw': nrm((nA, CONV_WIDTH, D), CONV_WIDTH ** -0.5),
        'conv_w_out': nrm((nA, D, D), D ** -0.5),
        'diff_w_qkv': nrm((nB, D, 3 * D), D ** -0.5),
        'diff_lambda': nrm((nB, 4, DIFF_HEAD_DIM), 0.1),
        'diff_subln': 1.0 + nrm((nB, 2 * DIFF_HEAD_DIM), 0.02),
        'diff_w_out': nrm((nB, D, D), D ** -0.5),
        'gla_w_in': nrm((nC, D, 2 * GLA_HEADS * GLA_DK + 2 * GLA_HEADS * GLA_DV), D ** -0.5),
        'gla_gate_w1': nrm((nC, 2, D, GLA_GATE_RANK), D ** -0.5),
        'gla_gate_w2': nrm((nC, 2, GLA_GATE_RANK, GLA_HEADS * GLA_DK), GLA_GATE_RANK ** -0.5),
        'gla_gate_b': nrm((nC, 2, GLA_HEADS * GLA_DK), 0.5),
        'gla_onorm': 1.0 + nrm((nC, GLA_DV), 0.02),
        'gla_w_out': nrm((nC, D, D), D ** -0.5),
        'router_w': nrm((DEPTH, D, N_EXPERTS), D ** -0.5),
        'exp_w_gate': nrm((DEPTH, N_EXPERTS, D, EXPERT_FF), D ** -0.5),
        'exp_w_up': nrm((DEPTH, N_EXPERTS, D, EXPERT_FF), D ** -0.5),
        'exp_w_down': nrm((DEPTH, N_EXPERTS, EXPERT_FF, D), EXPERT_FF ** -0.5),
        'final_norm': 1.0 + nrm((D,), 0.02),
    }


def reference(x, c, ctx, c_ctx, mod_w, mod_b, norm_mix, norm_ffn, conv_w_in, conv_w_dw, conv_w_out,
              diff_w_qkv, diff_lambda, diff_subln, diff_w_out, gla_w_in, gla_gate_w1, gla_gate_w2,
              gla_gate_b, gla_onorm, gla_w_out, router_w, exp_w_gate, exp_w_up, exp_w_down, final_norm):
    rows = x.shape[1] // GRID_W
    row = jnp.repeat(jnp.arange(rows, dtype=jnp.int32), GRID_W)
    col = jnp.tile(jnp.arange(GRID_W, dtype=jnp.int32), rows)
    c_lat = c[:, None, :]
    c_pre = c_ctx[None, None, :]
    for i in range(DEPTH):
        kind, j = i % N_MIXERS, i // N_MIXERS
        ctx_next = i < DEPTH - 1
        ctx_read = ctx_next or kind != 0
        sh1, sc1, g1, sh2, sc2, g2 = _modulation(c_lat, mod_w[i], mod_b[i])
        h_lat = _modulate(_rmsnorm(x, norm_mix[i]), sh1, sc1)
        h_ctx = None
        if ctx_read:
            csh1, csc1, cg1, csh2, csc2, cg2 = _modulation(c_pre, mod_w[i], mod_b[i])
            h_ctx = _modulate(_rmsnorm(ctx, norm_mix[i]), csh1, csc1)
        if kind == 0:
            y_lat = _short_conv(h_lat, conv_w_in[j], conv_w_dw[j], conv_w_out[j])
            y_ctx = _short_conv(h_ctx, conv_w_in[j], conv_w_dw[j], conv_w_out[j]) if ctx_next else None
        elif kind == 1:
            lambda_init = DIFF_LAMBDA_A - DIFF_LAMBDA_B * math.exp(-DIFF_LAMBDA_C * i)
            y_lat, y_ctx = _diff_attention(h_lat, h_ctx, row, col, diff_w_qkv[j], diff_lambda[j],
                                           diff_subln[j], diff_w_out[j], lambda_init, ctx_next)
        else:
            y_lat, y_ctx = _gla(h_lat, h_ctx, gla_w_in[j], gla_gate_w1[j], gla_gate_w2[j], gla_gate_b[j],
                                gla_onorm[j], gla_w_out[j], ctx_next)
        x = x + g1 * y_lat
        x = x + g2 * _ec_moe(_modulate(_rmsnorm(x, norm_ffn[i]), sh2, sc2),
                             router_w[i], exp_w_gate[i], exp_w_up[i], exp_w_down[i])
        if ctx_next:
            ctx = ctx + cg1 * y_ctx
            ctx = ctx + cg2 * _ec_moe(_modulate(_rmsnorm(ctx, norm_ffn[i]), csh2, csc2),
                                      router_w[i], exp_w_gate[i], exp_w_up[i], exp_w_down[i])
    return _rmsnorm(x, final_norm)
```

```python
import functools
import math

import jax
import jax.numpy as jnp
from jax import lax
from jax.experimental import pallas as pl
from jax.experimental.pallas import tpu as pltpu

F32 = jnp.float32
BF16 = jnp.bfloat16
I32 = jnp.int32

GRID_W = 64
N_MIXERS = 3
RMS_EPS = 1e-6
DIFF_HEAD_DIM = 128
DIFF_SUBLN_EPS = 1e-5
DIFF_LAMBDA_A = 0.8
DIFF_LAMBDA_B = 0.6
DIFF_LAMBDA_C = 0.3
ROPE_THETA = 10000.0
GLA_HEADS = 4
GLA_TAU = 16.0
GLA_CHUNK = 64
GLA_SUB = 16
CAPACITY_FACTOR = 2

LANES = 128
SUBLANES = 8
BF16_ROWS = 16
MXU_DIM = 256
VMEM_LIMIT_BYTES = 56 * 1024 * 1024
ROW_CHUNK = 128


def _params(sem, vmem=VMEM_LIMIT_BYTES):
    return pltpu.CompilerParams(dimension_semantics=sem, vmem_limit_bytes=vmem)


def _pick(n, cap, mult):
    best = None
    for d in range(mult, min(n, cap) + 1, mult):
        if n % d == 0:
            best = d
    assert best is not None, (n, cap, mult)
    return best


def _dot(a, b):
    return jnp.dot(a, b, preferred_element_type=F32)


def _dot_nt(a, b):
    return lax.dot_general(a, b, (((1,), (1,)), ((), ())), preferred_element_type=F32)


def _dot_tn(a, b):
    return lax.dot_general(a, b, (((0,), (0,)), ((), ())), preferred_element_type=F32)


def _split2(x):
    hi = x.astype(BF16)
    lo = (x - hi.astype(F32)).astype(BF16)
    return hi, lo


def _split3(x):
    a = x.astype(BF16)
    r = x - a.astype(F32)
    b = r.astype(BF16)
    c = (r - b.astype(F32)).astype(BF16)
    return a, b, c


def _sigmoid(x):
    return 1.0 / (1.0 + jnp.exp(-x))


def _silu(x):
    return x * _sigmoid(x)


def _for_rows(n_rows, chunk, fn):
    def body(r, carry):
        fn(pl.multiple_of(r * chunk, chunk))
        return carry

    lax.fori_loop(0, n_rows // chunk, body, 0)


def _row_select(mod, isctx, col, width):
    return jnp.where(isctx, mod[1:2, col:col + width], mod[0:1, col:col + width])


def _norm_mod(x, g, mod, isctx, shift_col, scale_col):
    d = x.shape[-1]
    ms = jnp.mean(x * x, axis=-1, keepdims=True)
    y = x * lax.rsqrt(ms + RMS_EPS) * g
    shift = _row_select(mod, isctx, shift_col, d)
    scale = _row_select(mod, isctx, scale_col, d)
    return y * (1.0 + scale) + shift


def _mod_kernel(c_ref, w_ref, b_ref, o_ref):
    c = c_ref[...]
    s_hi, s_lo = _split2(_silu(c))
    w_hi, w_lo = _split2(w_ref[0])
    acc = _dot(s_hi, w_hi) + _dot(s_lo, w_hi) + _dot(s_hi, w_lo)
    o_ref[0] = acc + b_ref[0]


def _modulation(cvec8, mod_w, mod_b):
    depth, d, n6 = mod_w.shape
    tn = _pick(n6, 1024, LANES)
    return pl.pallas_call(
        _mod_kernel,
        grid=(depth, n6 // tn),
        in_specs=[
            pl.BlockSpec((SUBLANES, d), lambda l, j: (0, 0)),
            pl.BlockSpec((1, d, tn), lambda l, j: (l, 0, j)),
            pl.BlockSpec((1, 1, tn), lambda l, j: (l, 0, j)),
        ],
        out_specs=pl.BlockSpec((1, SUBLANES, tn), lambda l, j: (l, 0, j)),
        out_shape=jax.ShapeDtypeStruct((depth, SUBLANES, n6), F32),
        compiler_params=_params(("arbitrary", "arbitrary")),
        name="modulation",
    )(cvec8, mod_w, mod_b.reshape(depth, 1, n6))


def _rope_kernel(inv_ref, sgn_ref, cos_ref, sin_ref, *, s_lat, tm):
    i = pl.program_id(0)
    t = i * tm + lax.broadcasted_iota(I32, (tm, LANES), 0)
    lane = lax.broadcasted_iota(I32, (tm, LANES), 1)
    row = t // GRID_W
    col = t % GRID_W
    pos = jnp.where(lane < DIFF_HEAD_DIM // 2, row, col)
    pos = jnp.where(t < s_lat, pos, 0)
    ang = pos.astype(F32) * inv_ref[...]
    cos_ref[...] = jnp.cos(ang)
    sin_ref[...] = jnp.sin(ang) * sgn_ref[...]


def _rope_tables(nt, s_lat):
    quarter = DIFF_HEAD_DIM // 4
    inv = ROPE_THETA ** (-jnp.arange(quarter, dtype=F32) / quarter)
    inv128 = jnp.tile(inv, 4).reshape(1, LANES)
    sgn = jnp.tile(jnp.concatenate([-jnp.ones((quarter,), F32), jnp.ones((quarter,), F32)]), 2)
    tm = _pick(nt, 1024, LANES)
    return pl.pallas_call(
        functools.partial(_rope_kernel, s_lat=s_lat, tm=tm),
        grid=(nt // tm,),
        in_specs=[pl.BlockSpec((1, LANES), lambda i: (0, 0)),
                  pl.BlockSpec((1, LANES), lambda i: (0, 0))],
        out_specs=[pl.BlockSpec((tm, LANES), lambda i: (i, 0)),
                   pl.BlockSpec((tm, LANES), lambda i: (i, 0))],
        out_shape=[jax.ShapeDtypeStruct((nt, LANES), F32)] * 2,
        compiler_params=_params(("arbitrary",)),
        name="rope_tables",
    )(inv128, sgn.reshape(1, LANES))


def _nmm_kernel(*refs, n_w, n_extra, n_out, s_lat, tm, mc, shift_col, scale_col, epilogue):
    x_ref, g_ref, mod_ref = refs[:3]
    w_refs = refs[3:3 + n_w]
    extra = refs[3 + n_w:3 + n_w + n_extra]
    outs = refs[3 + n_w + n_extra:3 + n_w + n_extra + n_out]
    h_ref = refs[-1]
    i = pl.program_id(0)
    j = pl.program_id(1)

    @pl.when(j == 0)
    def _():
        mod = mod_ref[...]
        g = g_ref[...]

        def slab(r0):
            rows = i * tm + r0 + lax.broadcasted_iota(I32, (ROW_CHUNK, 1), 0)
            h = _norm_mod(x_ref[pl.ds(r0, ROW_CHUNK), :], g, mod, rows >= s_lat, shift_col, scale_col)
            h_ref[pl.ds(r0, ROW_CHUNK), :] = h.astype(BF16)

        _for_rows(tm, ROW_CHUNK, slab)

    def mm(r0):
        rows = pl.ds(r0, mc)
        h = h_ref[rows, :]
        epilogue([_dot(h, w[...]) for w in w_refs], extra, outs, j, rows)

    _for_rows(tm, mc, mm)


def _norm_mod_matmul(t_arr, n_rows, s_lat, g, mod, shift_col, scale_col, w, w_col_blocks, tn,
                     n_tiles, epilogue, extra, extra_specs, out_shapes, out_specs, name):
    d = t_arr.shape[1]
    assert w.dtype == BF16
    tm = _pick(n_rows, 1408, ROW_CHUNK)
    mc = _pick(tm, 384, ROW_CHUNK)
    w_specs = [pl.BlockSpec((d, tn), functools.partial(lambda i, j, o: (0, o + j), o=o))
               for o in w_col_blocks]
    kern = functools.partial(
        _nmm_kernel, n_w=len(w_col_blocks), n_extra=len(extra), n_out=len(out_shapes),
        s_lat=s_lat, tm=tm, mc=mc, shift_col=shift_col, scale_col=scale_col, epilogue=epilogue)
    return pl.pallas_call(
        kern,
        grid=(n_rows // tm, n_tiles),
        in_specs=[pl.BlockSpec((tm, d), lambda i, j: (i, 0)),
                  pl.BlockSpec((1, d), lambda i, j: (0, 0)),
                  pl.BlockSpec(mod.shape, lambda i, j: (0, 0))]
                 + w_specs + [s(tm) for s in extra_specs],
        out_specs=[s(tm) for s in out_specs],
        out_shape=out_shapes,
        scratch_shapes=[pltpu.VMEM((tm, d), BF16)],
        compiler_params=_params(("arbitrary", "arbitrary")),
        name=name,
    )(t_arr, g.reshape(1, d), mod, *([w] * len(w_col_blocks)), *extra)


def _gated_residual_rows(a_ref, w_ref, x_ref, mod_ref, o_ref, i, s_lat, tm, mc):
    mod = mod_ref[...]

    def mm(r0):
        rows = pl.ds(r0, mc)
        acc = _dot(a_ref[rows, :], w_ref[...])
        tok = i * tm + r0 + lax.broadcasted_iota(I32, (mc, 1), 0)
        gate = jnp.where(tok >= s_lat, mod[1:2, :], mod[0:1, :])
        o_ref[rows, :] = x_ref[rows, :] + gate * acc

    _for_rows(tm, mc, mm)


def _mmres_kernel(a_ref, w_ref, x_ref, mod_ref, o_ref, *, s_lat, tm, mc):
    _gated_residual_rows(a_ref, w_ref, x_ref, mod_ref, o_ref, pl.program_id(0), s_lat, tm, mc)


def _matmul_residual(a, w, t_arr, n_rows, s_lat, mod, gate_col, name):
    k = a.shape[1]
    d = t_arr.shape[1]
    assert w.dtype == BF16
    tm = _pick(n_rows, 1408, ROW_CHUNK)
    mc = _pick(tm, 384, ROW_CHUNK)
    tn = _pick(d, 512, LANES)
    gblk = gate_col // tn
    return pl.pallas_call(
        functools.partial(_mmres_kernel, s_lat=s_lat, tm=tm, mc=mc),
        grid=(n_rows // tm, d // tn),
        in_specs=[pl.BlockSpec((tm, k), lambda i, j: (i, 0)),
                  pl.BlockSpec((k, tn), lambda i, j: (0, j)),
                  pl.BlockSpec((tm, tn), lambda i, j: (i, j)),
                  pl.BlockSpec((SUBLANES, tn), lambda i, j: (0, gblk + j))],
        out_specs=pl.BlockSpec((tm, tn), lambda i, j: (i, j)),
        out_shape=jax.ShapeDtypeStruct(t_arr.shape, F32),
        input_output_aliases={2: 0},
        compiler_params=_params(("arbitrary", "arbitrary")),
        name=name,
    )(a, w, t_arr, mod)


def _conv_in_epilogue(accs, extra, outs, j, rows):
    b, c, u = accs
    outs[0][rows, :] = b.astype(BF16)
    outs[1][rows, :] = c * u


def _conv_out_kernel(b_ref, v_ref, vp_ref, vn_ref, dw_ref, w_ref, x_ref, mod_ref, o_ref, a_ref, buf_ref,
                     *, s_lat, n_rows, tm, mc):
    i = pl.program_id(0)
    j = pl.program_id(1)

    @pl.when(j == 0)
    def _():
        buf_ref[0:SUBLANES, :] = vp_ref[...]
        buf_ref[SUBLANES + tm:2 * SUBLANES + tm, :] = vn_ref[...]

        def copy(r0):
            buf_ref[pl.ds(SUBLANES + r0, ROW_CHUNK), :] = v_ref[pl.ds(r0, ROW_CHUNK), :]

        _for_rows(tm, ROW_CHUNK, copy)
        dw = dw_ref[...]

        def slab(r0):
            win = buf_ref[pl.ds(r0, ROW_CHUNK + 2 * SUBLANES), :]
            prev = win[SUBLANES - 1:SUBLANES - 1 + ROW_CHUNK]
            cur = win[SUBLANES:SUBLANES + ROW_CHUNK]
            nxt = win[SUBLANES + 1:SUBLANES + 1 + ROW_CHUNK]
            rows = i * tm + r0 + lax.broadcasted_iota(I32, (ROW_CHUNK, 1), 0)
            has_prev = (rows != 0) & (rows != s_lat)
            has_next = (rows != s_lat - 1) & (rows != n_rows - 1)
            conv = (jnp.where(has_prev, prev, 0.0) * dw[0:1] + cur * dw[1:2]
                    + jnp.where(has_next, nxt, 0.0) * dw[2:3])
            a = b_ref[pl.ds(r0, ROW_CHUNK), :].astype(F32) * conv
            a_ref[pl.ds(r0, ROW_CHUNK), :] = a.astype(BF16)

        _for_rows(tm, ROW_CHUNK, slab)

    _gated_residual_rows(a_ref, w_ref, x_ref, mod_ref, o_ref, i, s_lat, tm, mc)


def _short_conv_layer(t_arr, n_rows, s_lat, mod, g, w_in, w_dw, w_out):
    nt, d = t_arr.shape
    w_in, w_out = w_in.astype(BF16), w_out.astype(BF16)
    tn = _pick(d, 512, LANES)
    nblk = d // tn
    b, v = _norm_mod_matmul(
        t_arr, n_rows, s_lat, g, mod, 0, d, w_in, [0, nblk, 2 * nblk], tn, nblk,
        _conv_in_epilogue, [], [],
        [jax.ShapeDtypeStruct((n_rows, d), BF16), jax.ShapeDtypeStruct((n_rows, d), F32)],
        [lambda tm: pl.BlockSpec((tm, tn), lambda i, j: (i, j))] * 2, "conv_in")
    tm = _pick(n_rows, 1024, ROW_CHUNK)
    mc = _pick(tm, 384, ROW_CHUNK)
    hb = tm // SUBLANES
    last = n_rows // SUBLANES - 1
    gblk = (2 * d) // tn
    return pl.pallas_call(
        functools.partial(_conv_out_kernel, s_lat=s_lat, n_rows=n_rows, tm=tm, mc=mc),
        grid=(n_rows // tm, nblk),
        in_specs=[pl.BlockSpec((tm, d), lambda i, j: (i, 0)),
                  pl.BlockSpec((tm, d), lambda i, j: (i, 0)),
                  pl.BlockSpec((SUBLANES, d), lambda i, j: (jnp.maximum(i * hb - 1, 0), 0)),
                  pl.BlockSpec((SUBLANES, d), lambda i, j: (jnp.minimum((i + 1) * hb, last), 0)),
                  pl.BlockSpec(w_dw.shape, lambda i, j: (0, 0)),
                  pl.BlockSpec((d, tn), lambda i, j: (0, j)),
                  pl.BlockSpec((tm, tn), lambda i, j: (i, j)),
                  pl.BlockSpec((SUBLANES, tn), lambda i, j: (0, gblk + j))],
        out_specs=pl.BlockSpec((tm, tn), lambda i, j: (i, j)),
        out_shape=jax.ShapeDtypeStruct((nt, d), F32),
        input_output_aliases={6: 0},
        scratch_shapes=[pltpu.VMEM((tm, d), BF16),
                        pltpu.VMEM((tm + 2 * SUBLANES, d), F32)],
        compiler_params=_params(("arbitrary", "arbitrary")),
        name="conv_out",
    )(b, v, v, v, w_dw, w_out, t_arr, mod)


def _rope_rot(a):
    half = DIFF_HEAD_DIM // 4
    lane = lax.broadcasted_iota(I32, a.shape, 1)
    up = pltpu.roll(a, LANES - half, axis=1)
    dn = pltpu.roll(a, half, axis=1)
    return jnp.where(lane % (2 * half) < half, up, dn)


def _qkv_epilogue(accs, extra, outs, j, rows, *, n_q, n_qk, tn):
    acc = accs[0]
    cos_ref, sin_ref = extra
    o_ref = outs[0]

    @pl.when(j < n_qk)
    def _():
        cos = cos_ref[rows, :]
        sin = sin_ref[rows, :]
        scale = jnp.where(j < n_q, DIFF_HEAD_DIM ** -0.5 * math.log2(math.e), 1.0).astype(F32)
        for gidx in range(tn // LANES):
            a = acc[:, gidx * LANES:(gidx + 1) * LANES]
            r = (a * cos + _rope_rot(a) * sin) * scale
            o_ref[rows, gidx * LANES:(gidx + 1) * LANES] = r.astype(BF16)

    @pl.when(j >= n_qk)
    def _():
        o_ref[rows, :] = acc.astype(BF16)


def _attn_body(q_ref, k_ref, v_ref, lam_ref, sub_ref, o_ref, k_lo, k_hi, lambda_init):
    hd = DIFF_HEAD_DIM
    tq = q_ref.shape[0]
    kc = _pick(k_hi - k_lo, 1024, 256)
    chunks = [slice(k_lo + c * kc, k_lo + (c + 1) * kc) for c in range((k_hi - k_lo) // kc)]
    state = []
    for t in range(2):
        q = q_ref[:, t * hd:(t + 1) * hd]
        m = jnp.full((tq, 1), -jnp.inf, F32)
        l = jnp.zeros((tq, 1), F32)
        acc = jnp.zeros((tq, 2 * hd), F32)
        for sl in chunks:
            s = _dot_nt(q, k_ref[sl, t * hd:(t + 1) * hd])
            m_new = jnp.maximum(m, jnp.max(s, axis=1, keepdims=True))
            alpha = jnp.exp2(m - m_new)
            p = jnp.exp2(s - m_new)
            l = alpha * l + jnp.sum(p, axis=1, keepdims=True)
            acc = alpha * acc + _dot(p.astype(BF16), v_ref[sl, :])
            m = m_new
        state.append((l, acc))
    (l0, acc0), (l1, acc1) = state
    lp = lam_ref[...]
    lam = (jnp.exp(jnp.sum(lp[0:1] * lp[1:2], axis=1, keepdims=True))
           - jnp.exp(jnp.sum(lp[2:3] * lp[3:4], axis=1, keepdims=True)) + lambda_init)
    o = acc0 * (1.0 / l0) - acc1 * (lam / l1)
    ms = jnp.mean(o * o, axis=1, keepdims=True)
    o = o * lax.rsqrt(ms + DIFF_SUBLN_EPS) * sub_ref[...] * (1.0 - lambda_init)
    o_ref[...] = o.astype(BF16)


def _attn_kernel(q_ref, k_ref, v_ref, lam_ref, sub_ref, o_ref, *, s_lat, lambda_init):
    tq = q_ref.shape[0]
    nt = k_ref.shape[0]
    n_lat_blocks = s_lat // tq
    i = pl.program_id(1)

    @pl.when(i < n_lat_blocks)
    def _():
        _attn_body(q_ref, k_ref, v_ref, lam_ref, sub_ref, o_ref, 0, nt, lambda_init)

    @pl.when(i >= n_lat_blocks)
    def _():
        _attn_body(q_ref, k_ref, v_ref, lam_ref, sub_ref, o_ref, s_lat, nt, lambda_init)


def _diff_attention(qkv, s_lat, lam_p, subln, lambda_init, d):
    nt = qkv.shape[0]
    hw = 2 * DIFF_HEAD_DIM
    heads = d // hw
    tq = 256
    assert s_lat % tq == 0 and (nt - s_lat) % tq == 0
    return pl.pallas_call(
        functools.partial(_attn_kernel, s_lat=s_lat, lambda_init=lambda_init),
        grid=(heads, nt // tq),
        in_specs=[pl.BlockSpec((tq, hw), lambda h, i: (i, h)),
                  pl.BlockSpec((nt, hw), lambda h, i: (0, heads + h)),
                  pl.BlockSpec((nt, hw), lambda h, i: (0, 2 * heads + h)),
                  pl.BlockSpec(lam_p.shape, lambda h, i: (0, 0)),
                  pl.BlockSpec((1, hw), lambda h, i: (0, 0))],
        out_specs=pl.BlockSpec((tq, hw), lambda h, i: (i, h)),
        out_shape=jax.ShapeDtypeStruct((nt, d), BF16),
        compiler_params=_params(("arbitrary", "arbitrary")),
        name="diff_attn",
    )(qkv, qkv, qkv, lam_p, subln.reshape(1, hw))


def _diff_layer(t_arr, s_lat, mod, g, w_qkv, lam_p, subln, w_out, lambda_init, cos_t, sin_t):
    nt, d = t_arr.shape
    w_qkv, w_out = w_qkv.astype(BF16), w_out.astype(BF16)
    tn = _pick(d, 512, LANES)
    nblk = d // tn
    ep = functools.partial(_qkv_epilogue, n_q=nblk, n_qk=2 * nblk, tn=tn)
    (qkv,) = _norm_mod_matmul(
        t_arr, nt, s_lat, g, mod, 0, d, w_qkv, [0], tn, 3 * nblk, ep,
        [cos_t, sin_t], [lambda tm: pl.BlockSpec((tm, LANES), lambda i, j: (i, 0))] * 2,
        [jax.ShapeDtypeStruct((nt, 3 * d), BF16)],
        [lambda tm: pl.BlockSpec((tm, tn), lambda i, j: (i, j))], "diff_qkv")
    o = _diff_attention(qkv, s_lat, lam_p, subln, lambda_init, d)
    return _matmul_residual(o, w_out, t_arr, nt, s_lat, mod, 2 * d, "diff_out")


def _gla_proj_epilogue(accs, extra, outs, j, rows, *, n_q, dk):
    scale = jnp.where(j < n_q, dk ** -0.5, 1.0).astype(F32)
    outs[0][rows, :] = accs[0] * scale


def _gla_rank_epilogue(accs, extra, outs, j, rows):
    outs[0][rows, :] = accs[0]


def _log_sigmoid(z):
    return jnp.minimum(z, 0.0) - jnp.log(1.0 + jnp.exp(-jnp.abs(z)))


def _gla_direction(q, k, v, z1, w2, bias, s_ref, kpad_ref, bpad_ref, rev):
    c, dk = q.shape
    sub = GLA_SUB
    nsub = c // sub
    z_hi, z_lo = _split2(z1)
    w_hi, w_lo = _split2(w2)
    z = _dot(z_hi, w_hi) + _dot(z_lo, w_hi) + _dot(z_hi, w_lo) + bias
    g = _log_sigmoid(z) * (1.0 / GLA_TAU)

    ri = lax.broadcasted_iota(I32, (c, c), 0)
    ci = lax.broadcasted_iota(I32, (c, c), 1)
    tri = ((ci >= ri) if rev else (ci <= ri)).astype(BF16)
    g1, g2, g3 = _split3(g)
    b = _dot(tri, g1) + _dot(tri, g2) + _dot(tri, g3)
    end_row = 0 if rev else c - 1
    b_end = b[end_row:end_row + 1, :]

    rowi = lax.broadcasted_iota(I32, (c, 1), 0)
    coli = lax.broadcasted_iota(I32, (1, c), 1)
    tau = (c - 1 - rowi) if rev else rowi
    blk_of_row = tau // sub

    kpad_ref[sub:sub + c, :] = k
    bpad_ref[sub:sub + c, :] = b
    a_mat = jnp.zeros((c, c), F32)
    for dlt in range(sub):
        off = sub + dlt if rev else sub - dlt
        k_sh = kpad_ref[off:off + c, :]
        b_sh = bpad_ref[off:off + c, :]
        valid = (tau % sub) >= dlt
        e = jnp.exp(jnp.where(valid, b - b_sh, 0.0))
        dsum = jnp.sum(q * k_sh * e, axis=1, keepdims=True)
        partner = (rowi + dlt) if rev else (rowi - dlt)
        a_mat = a_mat + jnp.where((coli == partner) & valid, dsum, 0.0)
    q_parts, k_parts = [], []
    for blk in range(1, nsub):
        ref_row = (c - sub * blk) if rev else sub * blk - 1
        b_at = b[ref_row:ref_row + 1, :]
        in_blk = blk_of_row == blk
        earlier = tau < sub * blk
        qt = jnp.where(in_blk, q * jnp.exp(jnp.where(in_blk, b - b_at, 0.0)), 0.0)
        kt = jnp.where(earlier, k * jnp.exp(jnp.where(earlier, b_at - b, 0.0)), 0.0)
        q_parts.append(qt.astype(BF16))
        k_parts.append(kt.astype(BF16))
    a_mat = a_mat + _dot_nt(jnp.concatenate(q_parts, axis=1), jnp.concatenate(k_parts, axis=1))
    o_intra = _dot(a_mat.astype(BF16), v.astype(BF16))

    s_old = s_ref[...]
    o_inter = _dot((q * jnp.exp(b)).astype(BF16), s_old.astype(BF16))
    kd = (k * jnp.exp(b_end - b)).astype(BF16)
    upd = _dot_tn(kd, v.astype(BF16))
    d1, d2, d3 = _split3(jnp.where(rowi == end_row, b, 0.0))
    ones = jnp.ones((c, LANES), BF16)
    decay_col = _dot_tn(d1, ones) + _dot_tn(d2, ones) + _dot_tn(d3, ones)
    s_ref[...] = jnp.exp(decay_col[:, 0:1]) * s_old + upd
    return o_intra + o_inter


def _gla_kernel(qf_ref, kf_ref, vf_ref, zf_ref, qb_ref, kb_ref, vb_ref, zb_ref,
                w2_ref, bias_ref, of_ref, ob_ref, sf_ref, sb_ref, kpf_ref, bpf_ref, kpb_ref, bpb_ref):
    cstep = pl.program_id(1)

    @pl.when(cstep == 0)
    def _():
        for ref in (sf_ref, sb_ref, kpf_ref, bpf_ref, kpb_ref, bpb_ref):
            ref[...] = jnp.zeros_like(ref)

    of_ref[...] = _gla_direction(qf_ref[...], kf_ref[...], vf_ref[...], zf_ref[...],
                                 w2_ref[0], bias_ref[0], sf_ref, kpf_ref, bpf_ref, False)
    ob_ref[...] = _gla_direction(qb_ref[...], kb_ref[...], vb_ref[...], zb_ref[...],
                                 w2_ref[1], bias_ref[1], sb_ref, kpb_ref, bpb_ref, True)


def _gla_post_kernel(of_ref, ob_ref, g_ref, on_ref, a_ref, *, dv):
    o = of_ref[...] + ob_ref[...]
    gate = _silu(g_ref[...])
    for h in range(o.shape[1] // dv):
        oh = o[:, h * dv:(h + 1) * dv]
        ms = jnp.mean(oh * oh, axis=1, keepdims=True)
        y = oh * lax.rsqrt(ms + RMS_EPS) * on_ref[...] * gate[:, h * dv:(h + 1) * dv]
        a_ref[:, h * dv:(h + 1) * dv] = y.astype(BF16)


def _gla_layer(t_arr, s_lat, mod, g, w_in, gate_w1, gate_w2, gate_b, onorm, w_out):
    nt, d = t_arr.shape
    w_in, w_out = w_in.astype(BF16), w_out.astype(BF16)
    cx = nt - s_lat
    heads = GLA_HEADS
    dk = d // (2 * heads)
    dv = d // heads
    rank = gate_w1.shape[2]
    n_proj = w_in.shape[1]
    tn = _pick(d, 512, LANES)
    nq_tiles = (heads * dk) // tn
    proj_ep = functools.partial(_gla_proj_epilogue, n_q=nq_tiles, dk=dk)
    (proj,) = _norm_mod_matmul(
        t_arr, nt, s_lat, g, mod, 0, d, w_in, [0], tn, n_proj // tn, proj_ep, [], [],
        [jax.ShapeDtypeStruct((nt, n_proj), F32)],
        [lambda tm: pl.BlockSpec((tm, tn), lambda i, j: (i, j))], "gla_proj")
    w1cat = jnp.concatenate([gate_w1[0], gate_w1[1]], axis=1)
    w1pad = jnp.pad(w1cat, ((0, 0), (0, LANES - 2 * rank))).astype(BF16)
    (z1,) = _norm_mod_matmul(
        t_arr, nt, s_lat, g, mod, 0, d, w1pad, [0], LANES, 1, _gla_rank_epilogue, [], [],
        [jax.ShapeDtypeStruct((nt, LANES), F32)],
        [lambda tm: pl.BlockSpec((tm, LANES), lambda i, j: (i, 0))], "gla_gate_rank")
    w2pad = jnp.zeros((2, LANES, heads * dk), F32)
    w2pad = w2pad.at[0, 0:rank].set(gate_w2[0]).at[1, rank:2 * rank].set(gate_w2[1])
    bias = gate_b.reshape(2, 1, heads * dk)

    c = GLA_CHUNK
    n_lat, n_ctx = s_lat // c, cx // c
    nch = n_lat + n_ctx

    def fwd(s):
        return jnp.where(s < n_ctx, n_lat + s, s - n_ctx)

    def bwd(s):
        return nch - 1 - s

    kcol = (heads * dk) // dk
    vcol = (2 * heads * dk) // dv

    def specs(row_of):
        return [pl.BlockSpec((c, dk), lambda h, s: (row_of(s), h)),
                pl.BlockSpec((c, dk), lambda h, s: (row_of(s), kcol + h)),
                pl.BlockSpec((c, dv), lambda h, s: (row_of(s), vcol + h)),
                pl.BlockSpec((c, LANES), lambda h, s: (row_of(s), 0))]

    pad_rows = c + 2 * GLA_SUB
    of, ob = pl.pallas_call(
        _gla_kernel,
        grid=(heads, nch),
        in_specs=specs(fwd) + specs(bwd)
                 + [pl.BlockSpec((2, LANES, dk), lambda h, s: (0, 0, h)),
                    pl.BlockSpec((2, 1, dk), lambda h, s: (0, 0, h))],
        out_specs=[pl.BlockSpec((c, dv), lambda h, s: (fwd(s), h)),
                   pl.BlockSpec((c, dv), lambda h, s: (bwd(s), h))],
        out_shape=[jax.ShapeDtypeStruct((nt, d), F32)] * 2,
        scratch_shapes=[pltpu.VMEM((dk, dv), F32), pltpu.VMEM((dk, dv), F32)]
                       + [pltpu.VMEM((pad_rows, dk), F32)] * 4,
        compiler_params=_params(("arbitrary", "arbitrary")),
        name="gla_scan",
    )(proj, proj, proj, z1, proj, proj, proj, z1, w2pad, bias)

    tm = _pick(nt, 512, ROW_CHUNK)
    gblk = (2 * heads * dk + heads * dv) // d
    a = pl.pallas_call(
        functools.partial(_gla_post_kernel, dv=dv),
        grid=(nt // tm,),
        in_specs=[pl.BlockSpec((tm, d), lambda i: (i, 0)),
                  pl.BlockSpec((tm, d), lambda i: (i, 0)),
                  pl.BlockSpec((tm, d), lambda i: (i, gblk)),
                  pl.BlockSpec((1, dv), lambda i: (0, 0))],
        out_specs=pl.BlockSpec((tm, d), lambda i: (i, 0)),
        out_shape=jax.ShapeDtypeStruct((nt, d), BF16),
        compiler_params=_params(("arbitrary",)),
        name="gla_post",
    )(of, ob, proj, onorm.reshape(1, dv))
    return _matmul_residual(a, w_out, t_arr, nt, s_lat, mod, 2 * d, "gla_out")


def _router_kernel(x_ref, g_ref, mod_ref, rw_ref, h_ref, aff_ref, *, s_lat, tm, n_exp, shift_col, scale_col):
    i = pl.program_id(0)
    mod = mod_ref[...]
    g = g_ref[...]
    rw_hi, rw_lo = _split2(rw_ref[...])
    rw2 = (rw_hi.astype(F32) + pltpu.roll(rw_lo.astype(F32), n_exp, axis=1)).astype(BF16)

    def slab(r0):
        rows = i * tm + r0 + lax.broadcasted_iota(I32, (ROW_CHUNK, 1), 0)
        h = _norm_mod(x_ref[pl.ds(r0, ROW_CHUNK), :], g, mod, rows >= s_lat, shift_col, scale_col)
        h_ref[pl.ds(r0, ROW_CHUNK), :] = h
        h_hi, h_lo = _split2(h)
        p = _dot(h_hi, rw2)
        p2 = _dot(h_lo, rw2)
        logits = p + pltpu.roll(p, LANES - n_exp, axis=1) + p2
        lt = jnp.transpose(logits)[0:n_exp, :]
        m = jnp.max(lt, axis=0, keepdims=True)
        e = jnp.exp(lt - m)
        aff_ref[r0 // ROW_CHUNK] = e / jnp.sum(e, axis=0, keepdims=True)

    _for_rows(tm, ROW_CHUNK, slab)


def _router(t_arr, n_rows, s_lat, g, mod, router_w):
    nt, d = t_arr.shape
    n_exp = router_w.shape[1]
    tm = _pick(n_rows, 1024, ROW_CHUNK)
    rw_pad = jnp.pad(router_w, ((0, 0), (0, LANES - n_exp)))
    hmod, aff = pl.pallas_call(
        functools.partial(_router_kernel, s_lat=s_lat, tm=tm, n_exp=n_exp,
                          shift_col=3 * d, scale_col=4 * d),
        grid=(n_rows // tm,),
        in_specs=[pl.BlockSpec((tm, d), lambda i: (i, 0)),
                  pl.BlockSpec((1, d), lambda i: (0, 0)),
                  pl.BlockSpec(mod.shape, lambda i: (0, 0)),
                  pl.BlockSpec((d, LANES), lambda i: (0, 0))],
        out_specs=[pl.BlockSpec((tm, d), lambda i: (i, 0)),
                   pl.BlockSpec((tm // ROW_CHUNK, n_exp, LANES), lambda i: (i, 0, 0))],
        out_shape=[jax.ShapeDtypeStruct((n_rows, d), F32),
                   jax.ShapeDtypeStruct((n_rows // ROW_CHUNK, n_exp, LANES), F32)],
        compiler_params=_params(("arbitrary",)),
        name="moe_router",
    )(t_arr, g.reshape(1, d), mod, rw_pad)
    return hmod, aff


def _select_kernel(aff_ref, idx_ref, gate_ref, *, cap):
    n_exp, nb, _ = aff_ref.shape
    aff = aff_ref[...]
    prefix = jnp.zeros((n_exp, 1, 1), I32)
    for bit in range(30, -1, -1):
        cand = prefix | (1 << bit)
        ge = aff >= lax.bitcast_convert_type(cand, F32)
        cnt = jnp.sum(jnp.sum(ge.astype(F32), axis=2, keepdims=True), axis=1, keepdims=True)
        prefix = jnp.where(cnt >= cap, cand, prefix)
    kth = lax.bitcast_convert_type(prefix, F32)
    gt = (aff > kth).astype(F32)
    eq = (aff == kth).astype(F32)
    n_gt = jnp.sum(jnp.sum(gt, axis=2, keepdims=True), axis=1, keepdims=True)
    need = cap - n_gt

    li = lax.broadcasted_iota(I32, (LANES, LANES), 0)
    lj = lax.broadcasted_iota(I32, (LANES, LANES), 1)
    upper = (li <= lj).astype(BF16)
    bi = lax.broadcasted_iota(I32, (nb, nb), 0)
    bj = lax.broadcasted_iota(I32, (nb, nb), 1)
    lower_strict = (bj < bi).astype(BF16)
    upper_nb = (bi <= bj).astype(BF16)
    ones_rows = jnp.ones((BF16_ROWS, LANES), BF16)
    pcol = lax.broadcasted_iota(I32, (cap, 1), 0).astype(F32)
    brow = lax.broadcasted_iota(I32, (1, nb), 1).astype(F32)
    lane_row = lax.broadcasted_iota(I32, (1, LANES), 1).astype(F32)

    for e in range(n_exp):
        eq_e = eq[e].astype(BF16)
        rank = _dot(eq_e, upper) + jnp.sum(_dot(lower_strict, eq_e), axis=1, keepdims=True)
        mask = jnp.maximum(gt[e], eq[e] * (rank <= need[e]).astype(F32))
        m_bf = mask.astype(BF16)
        lcs = _dot(m_bf, upper)
        tot = _dot_nt(ones_rows, m_bf)
        cb_row = _dot(tot.astype(BF16), upper_nb)[0:1, :]
        le = cb_row <= pcol
        blk = jnp.sum(le.astype(F32), axis=1, keepdims=True)
        lt = pcol - jnp.max(jnp.where(le, cb_row, 0.0), axis=1, keepdims=True)
        onehot = (brow == blk).astype(BF16)
        rowcs = _dot(onehot, lcs.astype(BF16))
        j = jnp.sum((rowcs <= lt).astype(F32), axis=1, keepdims=True)
        idx_ref[e] = (blk * LANES + j).astype(I32)
        a1, a2, a3 = _split3(aff[e])
        arow = _dot(onehot, a1) + _dot(onehot, a2) + _dot(onehot, a3)
        gate_ref[e] = jnp.sum(jnp.where(lane_row == j, arow, 0.0), axis=1, keepdims=True)


def _select(aff, cap):
    n_exp = aff.shape[0]
    return pl.pallas_call(
        functools.partial(_select_kernel, cap=cap),
        grid=(1,),
        in_specs=[pl.BlockSpec(aff.shape, lambda i: (0, 0, 0))],
        out_specs=[pl.BlockSpec((n_exp, cap, 1), lambda i: (0, 0, 0)),
                   pl.BlockSpec((n_exp, cap, 1), lambda i: (0, 0, 0))],
        out_shape=[jax.ShapeDtypeStruct((n_exp, cap, 1), I32),
                   jax.ShapeDtypeStruct((n_exp, cap, 1), F32)],
        compiler_params=_params(("arbitrary",)),
        name="moe_select",
    )(aff)


def _start_row_copies(hbm, buf, idx_ref, sem, n_rows, gather):
    def body(p, carry):
        row = idx_ref[0, 0, p]
        src, dst = hbm.at[pl.ds(row, 1), :], buf.at[pl.ds(p, 1), :]
        if not gather:
            src, dst = dst, src
        pltpu.make_async_copy(src, dst, sem).start()
        return carry

    lax.fori_loop(0, n_rows, body, 0, unroll=8)


def _wait_row_copies(buf, sem):
    pltpu.make_async_copy(buf, buf, sem).wait()


def _ffn_kernel(idx_ref, idx_next_ref, gate_ref, h_hbm, x_hbm, wg_ref, wu_ref, wd_ref, mod_ref, o_hbm,
                xg_ref, xb_ref, y_ref, acc_ref, wgb_ref, wub_ref, wdb_ref, sem,
                *, k_lat, kt, nf, mc, n_exp):
    del x_hbm
    e = pl.program_id(0)
    f = pl.program_id(1)
    x_sem, acc_sem, out_sem = sem.at[0], sem.at[1], sem.at[2]

    @pl.when(f == 0)
    def _():
        @pl.when(e == 0)
        def _():
            _start_row_copies(h_hbm, xg_ref, idx_ref, x_sem, kt, True)

        _wait_row_copies(xg_ref, x_sem)

        def cast(r0):
            xb_ref[pl.ds(r0, mc), :] = xg_ref[pl.ds(r0, mc), :].astype(BF16)

        _for_rows(kt, mc, cast)

        @pl.when(e + 1 < n_exp)
        def _():
            _start_row_copies(h_hbm, xg_ref, idx_next_ref, x_sem, kt, True)

    @pl.when(f == nf - 2)
    def _():
        @pl.when(e > 0)
        def _():
            _wait_row_copies(acc_ref, out_sem)

        _start_row_copies(o_hbm, acc_ref, idx_ref, acc_sem, kt, True)

    wgb_ref[...] = wg_ref[0, 0].astype(BF16)
    wub_ref[...] = wu_ref[0, 0].astype(BF16)
    wdb_ref[...] = wd_ref[0, 0].astype(BF16)

    def ffn(r0):
        xb = xb_ref[pl.ds(r0, mc), :]
        a = _dot(xb, wgb_ref[...])
        u = _dot(xb, wub_ref[...])
        part = _dot((_silu(a) * u).astype(BF16), wdb_ref[...])

        @pl.when(f == 0)
        def _():
            y_ref[pl.ds(r0, mc), :] = part

        @pl.when(f != 0)
        def _():
            y_ref[pl.ds(r0, mc), :] += part

    _for_rows(kt, mc, ffn)

    @pl.when(f == nf - 1)
    def _():
        _wait_row_copies(acc_ref, acc_sem)
        mod = mod_ref[...]

        def rmw(r0):
            rows = r0 + lax.broadcasted_iota(I32, (mc, 1), 0)
            g2 = jnp.where(rows >= k_lat, mod[1:2, :], mod[0:1, :])
            upd = g2 * (y_ref[pl.ds(r0, mc), :] * gate_ref[0, pl.ds(r0, mc), :])
            acc_ref[pl.ds(r0, mc), :] = acc_ref[pl.ds(r0, mc), :] + upd

        _for_rows(kt, mc, rmw)
        _start_row_copies(o_hbm, acc_ref, idx_ref, out_sem, kt, False)

        @pl.when(e == n_exp - 1)
        def _():
            _wait_row_copies(acc_ref, out_sem)


def _moe_ffn(t_arr, hmod, idx, gate, mod, w_gate, w_up, w_down, layer, k_lat):
    nt, d = t_arr.shape
    _, n_exp, _, ff = w_gate.shape
    kt = idx.shape[1]
    tf = _pick(ff, MXU_DIM, LANES)
    nf = ff // tf
    mc = _pick(kt, 384, BF16_ROWS)
    assert nf >= 2
    idx3 = idx.reshape(n_exp, 1, kt)
    return pl.pallas_call(
        functools.partial(_ffn_kernel, k_lat=k_lat, kt=kt, nf=nf, mc=mc, n_exp=n_exp),
        grid=(n_exp, nf),
        in_specs=[pl.BlockSpec((1, 1, kt), lambda e, f: (e, 0, 0), memory_space=pltpu.SMEM),
                  pl.BlockSpec((1, 1, kt), lambda e, f: (jnp.minimum(e + 1, n_exp - 1), 0, 0),
                               memory_space=pltpu.SMEM),
                  pl.BlockSpec((1, kt, 1), lambda e, f: (e, 0, 0)),
                  pl.BlockSpec(memory_space=pl.ANY),
                  pl.BlockSpec(memory_space=pl.ANY),
                  pl.BlockSpec((1, 1, d, tf), lambda e, f: (layer, e, 0, f)),
                  pl.BlockSpec((1, 1, d, tf), lambda e, f: (layer, e, 0, f)),
                  pl.BlockSpec((1, 1, tf, d), lambda e, f: (layer, e, f, 0)),
                  pl.BlockSpec((SUBLANES, d), lambda e, f: (0, 5))],
        out_specs=pl.BlockSpec(memory_space=pl.ANY),
        out_shape=jax.ShapeDtypeStruct((nt, d), F32),
        input_output_aliases={4: 0},
        scratch_shapes=[pltpu.VMEM((kt, d), F32), pltpu.VMEM((kt, d), BF16), pltpu.VMEM((kt, d), F32),
                        pltpu.VMEM((kt, d), F32),
                        pltpu.VMEM((d, tf), BF16), pltpu.VMEM((d, tf), BF16), pltpu.VMEM((tf, d), BF16),
                        pltpu.SemaphoreType.DMA((3,))],
        compiler_params=_params(("arbitrary", "arbitrary")),
        name="moe_ffn",
    )(idx3, idx3, gate, hmod, t_arr, w_gate, w_up, w_down, mod)


def _pad_blocks(aff):
    nb = aff.shape[1]
    pad = (-nb) % BF16_ROWS
    if pad:
        aff = jnp.concatenate([aff, jnp.full((aff.shape[0], pad, LANES), -1.0, F32)], axis=1)
    return aff


def _moe_layer(t_arr, n_rows, s_lat, mod, g, router_w, w_gate, w_up, w_down, layer):
    n_exp = router_w.shape[1]
    hmod, aff = _router(t_arr, n_rows, s_lat, g, mod, router_w)
    aff = jnp.transpose(aff, (1, 0, 2))
    nb_lat = s_lat // LANES
    cap_lat = max(1, (CAPACITY_FACTOR * s_lat) // n_exp)
    idx, gate = _select(_pad_blocks(aff[:, :nb_lat]), cap_lat)
    idx = idx.reshape(n_exp, cap_lat)
    if n_rows > s_lat:
        cx = n_rows - s_lat
        cap_ctx = max(1, (CAPACITY_FACTOR * cx) // n_exp)
        idx_c, gate_c = _select(_pad_blocks(aff[:, nb_lat:]), cap_ctx)
        idx = jnp.concatenate([idx, idx_c.reshape(n_exp, cap_ctx) + s_lat], axis=1)
        gate = jnp.concatenate([gate, gate_c], axis=1)
    return _moe_ffn(t_arr, hmod, idx, gate, mod, w_gate, w_up, w_down, layer, cap_lat)


def _final_kernel(x_ref, g_ref, o_ref):
    x = x_ref[...]
    ms = jnp.mean(x * x, axis=-1, keepdims=True)
    o_ref[...] = x * lax.rsqrt(ms + RMS_EPS) * g_ref[...]


def _final_norm(t_arr, s_lat, g):
    d = t_arr.shape[1]
    tm = _pick(s_lat, 512, ROW_CHUNK)
    return pl.pallas_call(
        _final_kernel,
        grid=(s_lat // tm,),
        in_specs=[pl.BlockSpec((tm, d), lambda i: (i, 0)), pl.BlockSpec((1, d), lambda i: (0, 0))],
        out_specs=pl.BlockSpec((tm, d), lambda i: (i, 0)),
        out_shape=jax.ShapeDtypeStruct((s_lat, d), F32),
        compiler_params=_params(("arbitrary",)),
        name="final_norm",
    )(t_arr, g.reshape(1, d))


def kernel(x, c, ctx, c_ctx, mod_w, mod_b, norm_mix, norm_ffn, conv_w_in, conv_w_dw, conv_w_out,
           diff_w_qkv, diff_lambda, diff_subln, diff_w_out, gla_w_in, gla_gate_w1, gla_gate_w2,
           gla_gate_b, gla_onorm, gla_w_out, router_w, exp_w_gate, exp_w_up, exp_w_down, final_norm):
    batch, s_lat, d = x.shape
    cx = ctx.shape[1]
    depth = mod_w.shape[0]
    assert batch == 1 and s_lat % GRID_W == 0
    nt = s_lat + cx
    t_arr = jnp.concatenate([x[0], ctx[0]], axis=0)
    cvec8 = jnp.concatenate([c, c_ctx[None, :], jnp.zeros((SUBLANES - 2, d), F32)], axis=0)
    mods = _modulation(cvec8, mod_w, mod_b)
    cos_t, sin_t = _rope_tables(nt, s_lat)

    for i in range(depth):
        kind, j = i % N_MIXERS, i // N_MIXERS
        ctx_next = i < depth - 1
        ctx_read = ctx_next or kind != 0
        n_rows = nt if ctx_read else s_lat
        if t_arr.shape[0] != n_rows:
            t_arr = t_arr[:n_rows]
        mod = mods[i]
        if kind == 0:
            t_arr = _short_conv_layer(t_arr, n_rows, s_lat, mod, norm_mix[i],
                                      conv_w_in[j], conv_w_dw[j], conv_w_out[j])
        elif kind == 1:
            lambda_init = DIFF_LAMBDA_A - DIFF_LAMBDA_B * math.exp(-DIFF_LAMBDA_C * i)
            t_arr = _diff_layer(t_arr, s_lat, mod, norm_mix[i], diff_w_qkv[j], diff_lambda[j],
                                diff_subln[j], diff_w_out[j], lambda_init, cos_t, sin_t)
        else:
            t_arr = _gla_layer(t_arr, s_lat, mod, norm_mix[i], gla_w_in[j], gla_gate_w1[j],
                               gla_gate_w2[j], gla_gate_b[j], gla_onorm[j], gla_w_out[j])
        n_moe = nt if ctx_next else s_lat
        if t_arr.shape[0] != n_moe:
            t_arr = t_arr[:n_moe]
        t_arr = _moe_layer(t_arr, n_moe, s_lat, mod, norm_ffn[i], router_w[i],
                           exp_w_gate, exp_w_up, exp_w_down, i)
    return _final_norm(t_arr, s_lat, final_norm)[None]
```

```python
import functools
import math

import jax
import jax.numpy as jnp
from jax import lax
from jax.experimental import pallas as pl
from jax.experimental.pallas import tpu as pltpu

F32 = jnp.float32
BF16 = jnp.bfloat16
I32 = jnp.int32

GRID_W = 64
N_MIXERS = 3
RMS_EPS = 1e-6
DIFF_HEAD_DIM = 128
DIFF_SUBLN_EPS = 1e-5
DIFF_LAMBDA_A = 0.8
DIFF_LAMBDA_B = 0.6
DIFF_LAMBDA_C = 0.3
ROPE_THETA = 10000.0
GLA_HEADS = 4
GLA_TAU = 16.0
GLA_CHUNK = 64
GLA_SUB = 16
GLA_FACTOR_BOUND = 60.0
CAPACITY_FACTOR = 2

LANES = 128
SUBLANES = 8
BF16_ROWS = 16
MXU_DIM = 256
VMEM_LIMIT_BYTES = 56 * 1024 * 1024
ROW_CHUNK = 128
MATMUL_ROWS = 768


def _params(sem, vmem=VMEM_LIMIT_BYTES):
    return pltpu.CompilerParams(dimension_semantics=sem, vmem_limit_bytes=vmem)


def _pick(n, cap, mult):
    best = None
    for d in range(mult, min(n, cap) + 1, mult):
        if n % d == 0:
            best = d
    assert best is not None, (n, cap, mult)
    return best


def _dot(a, b):
    return jnp.dot(a, b, preferred_element_type=F32)


def _dot_nt(a, b):
    return lax.dot_general(a, b, (((1,), (1,)), ((), ())), preferred_element_type=F32)


def _dot_tn(a, b):
    return lax.dot_general(a, b, (((0,), (0,)), ((), ())), preferred_element_type=F32)


def _split2(x):
    hi = x.astype(BF16)
    lo = (x - hi.astype(F32)).astype(BF16)
    return hi, lo


def _split3(x):
    a = x.astype(BF16)
    r = x - a.astype(F32)
    b = r.astype(BF16)
    c = (r - b.astype(F32)).astype(BF16)
    return a, b, c


def _sigmoid(x):
    return 1.0 / (1.0 + jnp.exp(-x))


def _silu(x):
    return x * _sigmoid(x)


def _for_rows(n_rows, chunk, fn):
    def body(r, carry):
        fn(pl.multiple_of(r * chunk, chunk))
        return carry

    lax.fori_loop(0, n_rows // chunk, body, 0)


def _row_select(mod, isctx, col, width):
    return jnp.where(isctx, mod[1:2, col:col + width], mod[0:1, col:col + width])


def _norm_mod(x, g, mod, isctx, shift_col, scale_col):
    d = x.shape[-1]
    ms = jnp.mean(x * x, axis=-1, keepdims=True)
    y = x * lax.rsqrt(ms + RMS_EPS) * g
    shift = _row_select(mod, isctx, shift_col, d)
    scale = _row_select(mod, isctx, scale_col, d)
    return y * (1.0 + scale) + shift


def _mod_kernel(c_ref, w_ref, b_ref, o_ref):
    c = c_ref[...]
    s_hi, s_lo = _split2(_silu(c))
    w_hi, w_lo = _split2(w_ref[0])
    acc = _dot(s_hi, w_hi) + _dot(s_lo, w_hi) + _dot(s_hi, w_lo)
    o_ref[0] = acc + b_ref[0]


def _modulation(cvec8, mod_w, mod_b):
    depth, d, n6 = mod_w.shape
    tn = _pick(n6, 1024, LANES)
    return pl.pallas_call(
        _mod_kernel,
        grid=(depth, n6 // tn),
        in_specs=[
            pl.BlockSpec((SUBLANES, d), lambda l, j: (0, 0)),
            pl.BlockSpec((1, d, tn), lambda l, j: (l, 0, j)),
            pl.BlockSpec((1, 1, tn), lambda l, j: (l, 0, j)),
        ],
        out_specs=pl.BlockSpec((1, SUBLANES, tn), lambda l, j: (l, 0, j)),
        out_shape=jax.ShapeDtypeStruct((depth, SUBLANES, n6), F32),
        compiler_params=_params(("arbitrary", "arbitrary")),
        name="modulation",
    )(cvec8, mod_w, mod_b.reshape(depth, 1, n6))


def _rope_kernel(inv_ref, sgn_ref, cos_ref, sin_ref, *, s_lat, tm):
    i = pl.program_id(0)
    t = i * tm + lax.broadcasted_iota(I32, (tm, LANES), 0)
    lane = lax.broadcasted_iota(I32, (tm, LANES), 1)
    row = t // GRID_W
    col = t % GRID_W
    pos = jnp.where(lane < DIFF_HEAD_DIM // 2, row, col)
    pos = jnp.where(t < s_lat, pos, 0)
    ang = pos.astype(F32) * inv_ref[...]
    cos_ref[...] = jnp.cos(ang)
    sin_ref[...] = jnp.sin(ang) * sgn_ref[...]


def _rope_tables(nt, s_lat):
    quarter = DIFF_HEAD_DIM // 4
    inv = ROPE_THETA ** (-jnp.arange(quarter, dtype=F32) / quarter)
    inv128 = jnp.tile(inv, 4).reshape(1, LANES)
    sgn = jnp.tile(jnp.concatenate([-jnp.ones((quarter,), F32), jnp.ones((quarter,), F32)]), 2)
    tm = _pick(nt, 1024, LANES)
    return pl.pallas_call(
        functools.partial(_rope_kernel, s_lat=s_lat, tm=tm),
        grid=(nt // tm,),
        in_specs=[pl.BlockSpec((1, LANES), lambda i: (0, 0)),
                  pl.BlockSpec((1, LANES), lambda i: (0, 0))],
        out_specs=[pl.BlockSpec((tm, LANES), lambda i: (i, 0)),
                   pl.BlockSpec((tm, LANES), lambda i: (i, 0))],
        out_shape=[jax.ShapeDtypeStruct((nt, LANES), F32)] * 2,
        compiler_params=_params(("arbitrary",)),
        name="rope_tables",
    )(inv128, sgn.reshape(1, LANES))


def _nmm_kernel(*refs, n_w, n_extra, n_out, s_lat, tm, mc, shift_col, scale_col, epilogue):
    x_ref, g_ref, mod_ref = refs[:3]
    w_refs = refs[3:3 + n_w]
    extra = refs[3 + n_w:3 + n_w + n_extra]
    outs = refs[3 + n_w + n_extra:3 + n_w + n_extra + n_out]
    h_ref = refs[-1]
    i = pl.program_id(0)
    j = pl.program_id(1)

    @pl.when(j == 0)
    def _():
        mod = mod_ref[...]
        g = g_ref[...]

        def slab(r0):
            rows = i * tm + r0 + lax.broadcasted_iota(I32, (ROW_CHUNK, 1), 0)
            h = _norm_mod(x_ref[pl.ds(r0, ROW_CHUNK), :], g, mod, rows >= s_lat, shift_col, scale_col)
            h_ref[pl.ds(r0, ROW_CHUNK), :] = h.astype(BF16)

        _for_rows(tm, ROW_CHUNK, slab)

    def mm(r0):
        rows = pl.ds(r0, mc)
        h = h_ref[rows, :]
        epilogue([_dot(h, w[...]) for w in w_refs], extra, outs, j, rows)

    _for_rows(tm, mc, mm)


def _norm_mod_matmul(t_arr, n_rows, s_lat, g, mod, shift_col, scale_col, w, w_col_blocks, tn,
                     n_tiles, epilogue, extra, extra_specs, out_shapes, out_specs, name):
    d = t_arr.shape[1]
    assert w.dtype == BF16
    tm = _pick(n_rows, 1408, ROW_CHUNK)
    mc = _pick(tm, MATMUL_ROWS, BF16_ROWS)
    w_specs = [pl.BlockSpec((d, tn), functools.partial(lambda i, j, o: (0, o + j), o=o))
               for o in w_col_blocks]
    kern = functools.partial(
        _nmm_kernel, n_w=len(w_col_blocks), n_extra=len(extra), n_out=len(out_shapes),
        s_lat=s_lat, tm=tm, mc=mc, shift_col=shift_col, scale_col=scale_col, epilogue=epilogue)
    return pl.pallas_call(
        kern,
        grid=(n_rows // tm, n_tiles),
        in_specs=[pl.BlockSpec((tm, d), lambda i, j: (i, 0)),
                  pl.BlockSpec((1, d), lambda i, j: (0, 0)),
                  pl.BlockSpec(mod.shape, lambda i, j: (0, 0))]
                 + w_specs + [s(tm) for s in extra_specs],
        out_specs=[s(tm) for s in out_specs],
        out_shape=out_shapes,
        scratch_shapes=[pltpu.VMEM((tm, d), BF16)],
        compiler_params=_params(("arbitrary", "arbitrary")),
        name=name,
    )(t_arr, g.reshape(1, d), mod, *([w] * len(w_col_blocks)), *extra)


def _gated_residual_rows(a_ref, w_ref, x_ref, mod_ref, o_ref, i, s_lat, tm, mc):
    mod = mod_ref[...]

    def mm(r0):
        rows = pl.ds(r0, mc)
        acc = _dot(a_ref[rows, :], w_ref[...])
        tok = i * tm + r0 + lax.broadcasted_iota(I32, (mc, 1), 0)
        gate = jnp.where(tok >= s_lat, mod[1:2, :], mod[0:1, :])
        o_ref[rows, :] = x_ref[rows, :] + gate * acc

    _for_rows(tm, mc, mm)


def _mmres_kernel(a_ref, w_ref, x_ref, mod_ref, o_ref, *, s_lat, tm, mc):
    _gated_residual_rows(a_ref, w_ref, x_ref, mod_ref, o_ref, pl.program_id(0), s_lat, tm, mc)


def _matmul_residual(a, w, t_arr, n_rows, s_lat, mod, gate_col, name):
    k = a.shape[1]
    d = t_arr.shape[1]
    assert w.dtype == BF16
    tm = _pick(n_rows, 1408, ROW_CHUNK)
    mc = _pick(tm, MATMUL_ROWS, BF16_ROWS)
    tn = _pick(d, 512, LANES)
    gblk = gate_col // tn
    return pl.pallas_call(
        functools.partial(_mmres_kernel, s_lat=s_lat, tm=tm, mc=mc),
        grid=(n_rows // tm, d // tn),
        in_specs=[pl.BlockSpec((tm, k), lambda i, j: (i, 0)),
                  pl.BlockSpec((k, tn), lambda i, j: (0, j)),
                  pl.BlockSpec((tm, tn), lambda i, j: (i, j)),
                  pl.BlockSpec((SUBLANES, tn), lambda i, j: (0, gblk + j))],
        out_specs=pl.BlockSpec((tm, tn), lambda i, j: (i, j)),
        out_shape=jax.ShapeDtypeStruct(t_arr.shape, F32),
        input_output_aliases={2: 0},
        compiler_params=_params(("arbitrary", "arbitrary")),
        name=name,
    )(a, w, t_arr, mod)


def _conv_in_epilogue(accs, extra, outs, j, rows):
    b, c, u = accs
    outs[0][rows, :] = b.astype(BF16)
    outs[1][rows, :] = c * u


def _conv_out_kernel(b_ref, v_ref, vp_ref, vn_ref, dw_ref, w_ref, x_ref, mod_ref, o_ref, a_ref, buf_ref,
                     *, s_lat, n_rows, tm, mc):
    i = pl.program_id(0)
    j = pl.program_id(1)

    @pl.when(j == 0)
    def _():
        buf_ref[0:SUBLANES, :] = vp_ref[...]
        buf_ref[SUBLANES + tm:2 * SUBLANES + tm, :] = vn_ref[...]

        def copy(r0):
            buf_ref[pl.ds(SUBLANES + r0, ROW_CHUNK), :] = v_ref[pl.ds(r0, ROW_CHUNK), :]

        _for_rows(tm, ROW_CHUNK, copy)
        dw = dw_ref[...]

        def slab(r0):
            win = buf_ref[pl.ds(r0, ROW_CHUNK + 2 * SUBLANES), :]
            prev = win[SUBLANES - 1:SUBLANES - 1 + ROW_CHUNK]
            cur = win[SUBLANES:SUBLANES + ROW_CHUNK]
            nxt = win[SUBLANES + 1:SUBLANES + 1 + ROW_CHUNK]
            rows = i * tm + r0 + lax.broadcasted_iota(I32, (ROW_CHUNK, 1), 0)
            has_prev = (rows != 0) & (rows != s_lat)
            has_next = (rows != s_lat - 1) & (rows != n_rows - 1)
            conv = (jnp.where(has_prev, prev, 0.0) * dw[0:1] + cur * dw[1:2]
                    + jnp.where(has_next, nxt, 0.0) * dw[2:3])
            a = b_ref[pl.ds(r0, ROW_CHUNK), :].astype(F32) * conv
            a_ref[pl.ds(r0, ROW_CHUNK), :] = a.astype(BF16)

        _for_rows(tm, ROW_CHUNK, slab)

    _gated_residual_rows(a_ref, w_ref, x_ref, mod_ref, o_ref, i, s_lat, tm, mc)


def _short_conv_layer(t_arr, n_rows, s_lat, mod, g, w_in, w_dw, w_out):
    nt, d = t_arr.shape
    w_in, w_out = w_in.astype(BF16), w_out.astype(BF16)
    tn = _pick(d, 512, LANES)
    nblk = d // tn
    b, v = _norm_mod_matmul(
        t_arr, n_rows, s_lat, g, mod, 0, d, w_in, [0, nblk, 2 * nblk], tn, nblk,
        _conv_in_epilogue, [], [],
        [jax.ShapeDtypeStruct((n_rows, d), BF16), jax.ShapeDtypeStruct((n_rows, d), F32)],
        [lambda tm: pl.BlockSpec((tm, tn), lambda i, j: (i, j))] * 2, "conv_in")
    tm = _pick(n_rows, 1024, ROW_CHUNK)
    mc = _pick(tm, MATMUL_ROWS, BF16_ROWS)
    hb = tm // SUBLANES
    last = n_rows // SUBLANES - 1
    gblk = (2 * d) // tn
    return pl.pallas_call(
        functools.partial(_conv_out_kernel, s_lat=s_lat, n_rows=n_rows, tm=tm, mc=mc),
        grid=(n_rows // tm, nblk),
        in_specs=[pl.BlockSpec((tm, d), lambda i, j: (i, 0)),
                  pl.BlockSpec((tm, d), lambda i, j: (i, 0)),
                  pl.BlockSpec((SUBLANES, d), lambda i, j: (jnp.maximum(i * hb - 1, 0), 0)),
                  pl.BlockSpec((SUBLANES, d), lambda i, j: (jnp.minimum((i + 1) * hb, last), 0)),
                  pl.BlockSpec(w_dw.shape, lambda i, j: (0, 0)),
                  pl.BlockSpec((d, tn), lambda i, j: (0, j)),
                  pl.BlockSpec((tm, tn), lambda i, j: (i, j)),
                  pl.BlockSpec((SUBLANES, tn), lambda i, j: (0, gblk + j))],
        out_specs=pl.BlockSpec((tm, tn), lambda i, j: (i, j)),
        out_shape=jax.ShapeDtypeStruct((nt, d), F32),
        input_output_aliases={6: 0},
        scratch_shapes=[pltpu.VMEM((tm, d), BF16),
                        pltpu.VMEM((tm + 2 * SUBLANES, d), F32)],
        compiler_params=_params(("arbitrary", "arbitrary")),
        name="conv_out",
    )(b, v, v, v, w_dw, w_out, t_arr, mod)


def _rope_rot(a):
    half = DIFF_HEAD_DIM // 4
    lane = lax.broadcasted_iota(I32, a.shape, 1)
    up = pltpu.roll(a, LANES - half, axis=1)
    dn = pltpu.roll(a, half, axis=1)
    return jnp.where(lane % (2 * half) < half, up, dn)


def _qkv_epilogue(accs, extra, outs, j, rows, *, n_q, n_qk, tn):
    acc = accs[0]
    cos_ref, sin_ref = extra
    o_ref = outs[0]

    @pl.when(j < n_qk)
    def _():
        cos = cos_ref[rows, :]
        sin = sin_ref[rows, :]
        scale = jnp.where(j < n_q, DIFF_HEAD_DIM ** -0.5 * math.log2(math.e), 1.0).astype(F32)
        for gidx in range(tn // LANES):
            a = acc[:, gidx * LANES:(gidx + 1) * LANES]
            r = (a * cos + _rope_rot(a) * sin) * scale
            o_ref[rows, gidx * LANES:(gidx + 1) * LANES] = r.astype(BF16)

    @pl.when(j >= n_qk)
    def _():
        o_ref[rows, :] = acc.astype(BF16)


def _attn_body(q_ref, k_ref, v_ref, lam_ref, sub_ref, o_ref, k_lo, k_hi, lambda_init):
    hd = DIFF_HEAD_DIM
    tq = q_ref.shape[0]
    kc = _pick(k_hi - k_lo, 1024, 256)
    chunks = [slice(k_lo + c * kc, k_lo + (c + 1) * kc) for c in range((k_hi - k_lo) // kc)]
    state = []
    for t in range(2):
        q = q_ref[:, t * hd:(t + 1) * hd]
        m = jnp.full((tq, 1), -jnp.inf, F32)
        l = jnp.zeros((tq, 1), F32)
        acc = jnp.zeros((tq, 2 * hd), F32)
        for sl in chunks:
            s = _dot_nt(q, k_ref[sl, t * hd:(t + 1) * hd])
            m_new = jnp.maximum(m, jnp.max(s, axis=1, keepdims=True))
            alpha = jnp.exp2(m - m_new)
            p = jnp.exp2(s - m_new)
            l = alpha * l + jnp.sum(p, axis=1, keepdims=True)
            acc = alpha * acc + _dot(p.astype(BF16), v_ref[sl, :])
            m = m_new
        state.append((l, acc))
    (l0, acc0), (l1, acc1) = state
    lp = lam_ref[...]
    lam = (jnp.exp(jnp.sum(lp[0:1] * lp[1:2], axis=1, keepdims=True))
           - jnp.exp(jnp.sum(lp[2:3] * lp[3:4], axis=1, keepdims=True)) + lambda_init)
    o = acc0 * (1.0 / l0) - acc1 * (lam / l1)
    ms = jnp.mean(o * o, axis=1, keepdims=True)
    o = o * lax.rsqrt(ms + DIFF_SUBLN_EPS) * sub_ref[...] * (1.0 - lambda_init)
    o_ref[...] = o.astype(BF16)


def _attn_kernel(q_ref, k_ref, v_ref, lam_ref, sub_ref, o_ref, *, s_lat, lambda_init):
    tq = q_ref.shape[0]
    nt = k_ref.shape[0]
    n_lat_blocks = s_lat // tq
    i = pl.program_id(1)

    @pl.when(i < n_lat_blocks)
    def _():
        _attn_body(q_ref, k_ref, v_ref, lam_ref, sub_ref, o_ref, 0, nt, lambda_init)

    @pl.when(i >= n_lat_blocks)
    def _():
        _attn_body(q_ref, k_ref, v_ref, lam_ref, sub_ref, o_ref, s_lat, nt, lambda_init)


def _diff_attention(qkv, s_lat, lam_p, subln, lambda_init, d):
    nt = qkv.shape[0]
    hw = 2 * DIFF_HEAD_DIM
    heads = d // hw
    tq = 256
    assert s_lat % tq == 0 and (nt - s_lat) % tq == 0
    return pl.pallas_call(
        functools.partial(_attn_kernel, s_lat=s_lat, lambda_init=lambda_init),
        grid=(heads, nt // tq),
        in_specs=[pl.BlockSpec((tq, hw), lambda h, i: (i, h)),
                  pl.BlockSpec((nt, hw), lambda h, i: (0, heads + h)),
                  pl.BlockSpec((nt, hw), lambda h, i: (0, 2 * heads + h)),
                  pl.BlockSpec(lam_p.shape, lambda h, i: (0, 0)),
                  pl.BlockSpec((1, hw), lambda h, i: (0, 0))],
        out_specs=pl.BlockSpec((tq, hw), lambda h, i: (i, h)),
        out_shape=jax.ShapeDtypeStruct((nt, d), BF16),
        compiler_params=_params(("arbitrary", "arbitrary")),
        name="diff_attn",
    )(qkv, qkv, qkv, lam_p, subln.reshape(1, hw))


def _diff_layer(t_arr, s_lat, mod, g, w_qkv, lam_p, subln, w_out, lambda_init, cos_t, sin_t):
    nt, d = t_arr.shape
    w_qkv, w_out = w_qkv.astype(BF16), w_out.astype(BF16)
    tn = _pick(d, 512, LANES)
    nblk = d // tn
    ep = functools.partial(_qkv_epilogue, n_q=nblk, n_qk=2 * nblk, tn=tn)
    (qkv,) = _norm_mod_matmul(
        t_arr, nt, s_lat, g, mod, 0, d, w_qkv, [0], tn, 3 * nblk, ep,
        [cos_t, sin_t], [lambda tm: pl.BlockSpec((tm, LANES), lambda i, j: (i, 0))] * 2,
        [jax.ShapeDtypeStruct((nt, 3 * d), BF16)],
        [lambda tm: pl.BlockSpec((tm, tn), lambda i, j: (i, j))], "diff_qkv")
    o = _diff_attention(qkv, s_lat, lam_p, subln, lambda_init, d)
    return _matmul_residual(o, w_out, t_arr, nt, s_lat, mod, 2 * d, "diff_out")


def _gla_proj_epilogue(accs, extra, outs, j, rows, *, n_q, dk):
    scale = jnp.where(j < n_q, dk ** -0.5, 1.0).astype(F32)
    outs[0][rows, :] = accs[0] * scale


def _gla_rank_epilogue(accs, extra, outs, j, rows):
    outs[0][rows, :] = accs[0]


def _log_sigmoid(z):
    return jnp.minimum(z, 0.0) - jnp.log(1.0 + jnp.exp(-jnp.abs(z)))


def _gla_time(c, rev):
    rowi = lax.broadcasted_iota(I32, (c, 1), 0)
    coli = lax.broadcasted_iota(I32, (1, c), 1)
    return rowi, coli, ((c - 1 - rowi) if rev else rowi), ((c - 1 - coli) if rev else coli)


def _gla_log_decay(z1, w2, bias, rev):
    c = z1.shape[0]
    z_hi, z_lo = _split2(z1)
    w_hi, w_lo = _split2(w2)
    z = _dot(z_hi, w_hi) + _dot(z_lo, w_hi) + _dot(z_hi, w_lo) + bias
    g = _log_sigmoid(z) * (1.0 / GLA_TAU)
    ri = lax.broadcasted_iota(I32, (c, c), 0)
    ci = lax.broadcasted_iota(I32, (c, c), 1)
    tri = ((ci >= ri) if rev else (ci <= ri)).astype(BF16)
    g1, g2, g3 = _split3(g)
    return _dot(tri, g1) + _dot(tri, g2) + _dot(tri, g3)


def _gla_block_first(b, rev):
    c = b.shape[0]
    sub = GLA_SUB
    parts = []
    for m in range(c // sub):
        r = m * sub + (sub - 1 if rev else 0)
        parts.append(jnp.broadcast_to(b[r:r + 1, :], (sub, b.shape[1])))
    return jnp.concatenate(parts, axis=0)


def _gla_scores_factored(q, k, b, b_first, rev):
    c = q.shape[0]
    sub = GLA_SUB
    _, _, tau, tau_col = _gla_time(c, rev)
    qt = q * jnp.exp(b - b_first)
    q_parts, k_parts = [], []
    for blk in range(c // sub):
        rb = (c - 1 - sub * blk) if rev else sub * blk
        b_at = b[rb:rb + 1, :]
        in_blk = (tau // sub) == blk
        upto = tau < sub * (blk + 1)
        q_parts.append(jnp.where(in_blk, qt, 0.0).astype(BF16))
        k_parts.append(jnp.where(upto, k * jnp.exp(jnp.where(upto, b_at - b, 0.0)), 0.0).astype(BF16))
    a = _dot_nt(jnp.concatenate(q_parts, axis=1), jnp.concatenate(k_parts, axis=1))
    return jnp.where(tau_col <= tau, a, 0.0)


def _gla_scores_exact(q, k, b, kpad_ref, bpad_ref, rev):
    c = q.shape[0]
    sub = GLA_SUB
    rowi, coli, tau, _ = _gla_time(c, rev)
    kpad_ref[sub:sub + c, :] = k
    bpad_ref[sub:sub + c, :] = b
    a_mat = jnp.zeros((c, c), F32)
    for dlt in range(sub):
        off = sub + dlt if rev else sub - dlt
        k_sh = kpad_ref[off:off + c, :]
        b_sh = bpad_ref[off:off + c, :]
        valid = (tau % sub) >= dlt
        e = jnp.exp(jnp.where(valid, b - b_sh, 0.0))
        dsum = jnp.sum(q * k_sh * e, axis=1, keepdims=True)
        partner = (rowi + dlt) if rev else (rowi - dlt)
        a_mat = a_mat + jnp.where((coli == partner) & valid, dsum, 0.0)
    q_parts, k_parts = [], []
    for blk in range(1, c // sub):
        ref_row = (c - sub * blk) if rev else sub * blk - 1
        b_at = b[ref_row:ref_row + 1, :]
        in_blk = (tau // sub) == blk
        earlier = tau < sub * blk
        qt = jnp.where(in_blk, q * jnp.exp(jnp.where(in_blk, b - b_at, 0.0)), 0.0)
        kt = jnp.where(earlier, k * jnp.exp(jnp.where(earlier, b_at - b, 0.0)), 0.0)
        q_parts.append(qt.astype(BF16))
        k_parts.append(kt.astype(BF16))
    return a_mat + _dot_nt(jnp.concatenate(q_parts, axis=1), jnp.concatenate(k_parts, axis=1))


def _gla_chunk_output(q, k, v, b, a_mat, s_ref, rev):
    c = q.shape[0]
    rowi = lax.broadcasted_iota(I32, (c, 1), 0)
    end_row = 0 if rev else c - 1
    b_end = b[end_row:end_row + 1, :]
    vb = v.astype(BF16)
    s_old = s_ref[...]
    o = _dot(a_mat.astype(BF16), vb) + _dot((q * jnp.exp(b)).astype(BF16), s_old.astype(BF16))
    upd = _dot_tn((k * jnp.exp(b_end - b)).astype(BF16), vb)
    d1, d2, d3 = _split3(jnp.where(rowi == end_row, b, 0.0))
    ones = jnp.ones((c, LANES), BF16)
    decay_col = _dot_tn(d1, ones) + _dot_tn(d2, ones) + _dot_tn(d3, ones)
    s_ref[...] = jnp.exp(decay_col[:, 0:1]) * s_old + upd
    return o


def _gla_kernel(qf_ref, kf_ref, vf_ref, zf_ref, qb_ref, kb_ref, vb_ref, zb_ref,
                w2_ref, bias_ref, of_ref, ob_ref, s_ref, b_ref, pad_ref, *, heads):
    @pl.when(pl.program_id(0) == 0)
    def _():
        s_ref[...] = jnp.zeros_like(s_ref)
        pad_ref[...] = jnp.zeros_like(pad_ref)

    dk = qf_ref.shape[1] // heads
    dv = vf_ref.shape[1] // heads
    dirs = ((qf_ref, kf_ref, vf_ref, zf_ref, of_ref, False), (qb_ref, kb_ref, vb_ref, zb_ref, ob_ref, True))
    excess = jnp.zeros((1, 1), F32)
    for t, (_, _, _, z_ref, _, rev) in enumerate(dirs):
        b = _gla_log_decay(z_ref[...], w2_ref[t], bias_ref[t], rev)
        b_first = _gla_block_first(b, rev)
        b_ref[t, 0] = b
        b_ref[t, 1] = b_first
        excess = jnp.maximum(excess, jnp.max(jnp.max(b_first - b, axis=1, keepdims=True), axis=0, keepdims=True))
    mild = jnp.max(excess) <= GLA_FACTOR_BOUND

    def run(exact):
        for t, (q_ref, k_ref, v_ref, _, o_ref, rev) in enumerate(dirs):
            for h in range(heads):
                ks = slice(h * dk, (h + 1) * dk)
                vs = slice(h * dv, (h + 1) * dv)
                q, k, v, b = q_ref[:, ks], k_ref[:, ks], v_ref[:, vs], b_ref[t, 0, :, ks]
                if exact:
                    a_mat = _gla_scores_exact(q, k, b, pad_ref.at[t, h, 0], pad_ref.at[t, h, 1], rev)
                else:
                    a_mat = _gla_scores_factored(q, k, b, b_ref[t, 1, :, ks], rev)
                o_ref[:, vs] = _gla_chunk_output(q, k, v, b, a_mat, s_ref.at[t, h], rev)

    pl.when(mild)(lambda: run(False))
    pl.when(jnp.logical_not(mild))(lambda: run(True))


def _gla_post_kernel(of_ref, ob_ref, g_ref, on_ref, a_ref, *, dv):
    o = of_ref[...] + ob_ref[...]
    gate = _silu(g_ref[...])
    for h in range(o.shape[1] // dv):
        oh = o[:, h * dv:(h + 1) * dv]
        ms = jnp.mean(oh * oh, axis=1, keepdims=True)
        y = oh * lax.rsqrt(ms + RMS_EPS) * on_ref[...] * gate[:, h * dv:(h + 1) * dv]
        a_ref[:, h * dv:(h + 1) * dv] = y.astype(BF16)


def _gla_layer(t_arr, s_lat, mod, g, w_in, gate_w1, gate_w2, gate_b, onorm, w_out):
    nt, d = t_arr.shape
    w_in, w_out = w_in.astype(BF16), w_out.astype(BF16)
    cx = nt - s_lat
    heads = GLA_HEADS
    dk = d // (2 * heads)
    dv = d // heads
    rank = gate_w1.shape[2]
    n_proj = w_in.shape[1]
    tn = _pick(d, 512, LANES)
    nq_tiles = (heads * dk) // tn
    proj_ep = functools.partial(_gla_proj_epilogue, n_q=nq_tiles, dk=dk)
    (proj,) = _norm_mod_matmul(
        t_arr, nt, s_lat, g, mod, 0, d, w_in, [0], tn, n_proj // tn, proj_ep, [], [],
        [jax.ShapeDtypeStruct((nt, n_proj), F32)],
        [lambda tm: pl.BlockSpec((tm, tn), lambda i, j: (i, j))], "gla_proj")
    w1cat = jnp.concatenate([gate_w1[0], gate_w1[1]], axis=1)
    w1pad = jnp.pad(w1cat, ((0, 0), (0, LANES - 2 * rank))).astype(BF16)
    (z1,) = _norm_mod_matmul(
        t_arr, nt, s_lat, g, mod, 0, d, w1pad, [0], LANES, 1, _gla_rank_epilogue, [], [],
        [jax.ShapeDtypeStruct((nt, LANES), F32)],
        [lambda tm: pl.BlockSpec((tm, LANES), lambda i, j: (i, 0))], "gla_gate_rank")
    w2pad = jnp.zeros((2, LANES, heads * dk), F32)
    w2pad = w2pad.at[0, 0:rank].set(gate_w2[0]).at[1, rank:2 * rank].set(gate_w2[1])
    bias = gate_b.reshape(2, 1, heads * dk)

    c = GLA_CHUNK
    n_lat, n_ctx = s_lat // c, cx // c
    nch = n_lat + n_ctx

    def fwd(s):
        return jnp.where(s < n_ctx, n_lat + s, s - n_ctx)

    def bwd(s):
        return nch - 1 - s

    def specs(row_of):
        return [pl.BlockSpec((c, heads * dk), lambda s: (row_of(s), 0)),
                pl.BlockSpec((c, heads * dk), lambda s: (row_of(s), 1)),
                pl.BlockSpec((c, heads * dv), lambda s: (row_of(s), (2 * heads * dk) // (heads * dv))),
                pl.BlockSpec((c, LANES), lambda s: (row_of(s), 0))]

    pad_rows = c + 2 * GLA_SUB
    of, ob = pl.pallas_call(
        functools.partial(_gla_kernel, heads=heads),
        grid=(nch,),
        in_specs=specs(fwd) + specs(bwd)
                 + [pl.BlockSpec((2, LANES, heads * dk), lambda s: (0, 0, 0)),
                    pl.BlockSpec((2, 1, heads * dk), lambda s: (0, 0, 0))],
        out_specs=[pl.BlockSpec((c, heads * dv), lambda s: (fwd(s), 0)),
                   pl.BlockSpec((c, heads * dv), lambda s: (bwd(s), 0))],
        out_shape=[jax.ShapeDtypeStruct((nt, d), F32)] * 2,
        scratch_shapes=[pltpu.VMEM((2, heads, dk, dv), F32),
                        pltpu.VMEM((2, 2, c, heads * dk), F32),
                        pltpu.VMEM((2, heads, 2, pad_rows, dk), F32)],
        compiler_params=_params(("arbitrary",)),
        name="gla_scan",
    )(proj, proj, proj, z1, proj, proj, proj, z1, w2pad, bias)

    tm = _pick(nt, 512, ROW_CHUNK)
    gblk = (2 * heads * dk + heads * dv) // d
    a = pl.pallas_call(
        functools.partial(_gla_post_kernel, dv=dv),
        grid=(nt // tm,),
        in_specs=[pl.BlockSpec((tm, d), lambda i: (i, 0)),
                  pl.BlockSpec((tm, d), lambda i: (i, 0)),
                  pl.BlockSpec((tm, d), lambda i: (i, gblk)),
                  pl.BlockSpec((1, dv), lambda i: (0, 0))],
        out_specs=pl.BlockSpec((tm, d), lambda i: (i, 0)),
        out_shape=jax.ShapeDtypeStruct((nt, d), BF16),
        compiler_params=_params(("arbitrary",)),
        name="gla_post",
    )(of, ob, proj, onorm.reshape(1, dv))
    return _matmul_residual(a, w_out, t_arr, nt, s_lat, mod, 2 * d, "gla_out")


def _router_kernel(x_ref, g_ref, mod_ref, rw_ref, h_ref, aff_ref, *, s_lat, tm, n_exp, shift_col, scale_col):
    i = pl.program_id(0)
    mod = mod_ref[...]
    g = g_ref[...]
    rw_hi, rw_lo = _split2(rw_ref[...])
    rw2 = (rw_hi.astype(F32) + pltpu.roll(rw_lo.astype(F32), n_exp, axis=1)).astype(BF16)

    def slab(r0):
        rows = i * tm + r0 + lax.broadcasted_iota(I32, (ROW_CHUNK, 1), 0)
        h = _norm_mod(x_ref[pl.ds(r0, ROW_CHUNK), :], g, mod, rows >= s_lat, shift_col, scale_col)
        h_ref[pl.ds(r0, ROW_CHUNK), :] = h
        h_hi, h_lo = _split2(h)
        p = _dot(h_hi, rw2)
        p2 = _dot(h_lo, rw2)
        logits = p + pltpu.roll(p, LANES - n_exp, axis=1) + p2
        lt = jnp.transpose(logits)[0:n_exp, :]
        m = jnp.max(lt, axis=0, keepdims=True)
        e = jnp.exp(lt - m)
        aff_ref[r0 // ROW_CHUNK] = e / jnp.sum(e, axis=0, keepdims=True)

    _for_rows(tm, ROW_CHUNK, slab)


def _router(t_arr, n_rows, s_lat, g, mod, router_w):
    nt, d = t_arr.shape
    n_exp = router_w.shape[1]
    tm = _pick(n_rows, 1024, ROW_CHUNK)
    rw_pad = jnp.pad(router_w, ((0, 0), (0, LANES - n_exp)))
    hmod, aff = pl.pallas_call(
        functools.partial(_router_kernel, s_lat=s_lat, tm=tm, n_exp=n_exp,
                          shift_col=3 * d, scale_col=4 * d),
        grid=(n_rows // tm,),
        in_specs=[pl.BlockSpec((tm, d), lambda i: (i, 0)),
                  pl.BlockSpec((1, d), lambda i: (0, 0)),
                  pl.BlockSpec(mod.shape, lambda i: (0, 0)),
                  pl.BlockSpec((d, LANES), lambda i: (0, 0))],
        out_specs=[pl.BlockSpec((tm, d), lambda i: (i, 0)),
                   pl.BlockSpec((tm // ROW_CHUNK, n_exp, LANES), lambda i: (i, 0, 0))],
        out_shape=[jax.ShapeDtypeStruct((n_rows, d), F32),
                   jax.ShapeDtypeStruct((n_rows // ROW_CHUNK, n_exp, LANES), F32)],
        compiler_params=_params(("arbitrary",)),
        name="moe_router",
    )(t_arr, g.reshape(1, d), mod, rw_pad)
    return hmod, aff


def _select_kernel(aff_ref, idx_ref, gate_ref, *, cap):
    n_exp, nb, _ = aff_ref.shape
    aff = aff_ref[...]
    prefix = jnp.zeros((n_exp, 1, 1), I32)
    for bit in range(30, -1, -1):
        cand = prefix | (1 << bit)
        ge = aff >= lax.bitcast_convert_type(cand, F32)
        cnt = jnp.sum(jnp.sum(ge.astype(F32), axis=2, keepdims=True), axis=1, keepdims=True)
        prefix = jnp.where(cnt >= cap, cand, prefix)
    kth = lax.bitcast_convert_type(prefix, F32)
    gt = (aff > kth).astype(F32)
    eq = (aff == kth).astype(F32)
    n_gt = jnp.sum(jnp.sum(gt, axis=2, keepdims=True), axis=1, keepdims=True)
    need = cap - n_gt

    li = lax.broadcasted_iota(I32, (LANES, LANES), 0)
    lj = lax.broadcasted_iota(I32, (LANES, LANES), 1)
    upper = (li <= lj).astype(BF16)
    bi = lax.broadcasted_iota(I32, (nb, nb), 0)
    bj = lax.broadcasted_iota(I32, (nb, nb), 1)
    lower_strict = (bj < bi).astype(BF16)
    upper_nb = (bi <= bj).astype(BF16)
    ones_rows = jnp.ones((BF16_ROWS, LANES), BF16)
    pcol = lax.broadcasted_iota(I32, (cap, 1), 0).astype(F32)
    brow = lax.broadcasted_iota(I32, (1, nb), 1).astype(F32)
    lane_row = lax.broadcasted_iota(I32, (1, LANES), 1).astype(F32)

    for e in range(n_exp):
        eq_e = eq[e].astype(BF16)
        rank = _dot(eq_e, upper) + jnp.sum(_dot(lower_strict, eq_e), axis=1, keepdims=True)
        mask = jnp.maximum(gt[e], eq[e] * (rank <= need[e]).astype(F32))
        m_bf = mask.astype(BF16)
        lcs = _dot(m_bf, upper)
        tot = _dot_nt(ones_rows, m_bf)
        cb_row = _dot(tot.astype(BF16), upper_nb)[0:1, :]
        le = cb_row <= pcol
        blk = jnp.sum(le.astype(F32), axis=1, keepdims=True)
        lt = pcol - jnp.max(jnp.where(le, cb_row, 0.0), axis=1, keepdims=True)
        onehot = (brow == blk).astype(BF16)
        rowcs = _dot(onehot, lcs.astype(BF16))
        j = jnp.sum((rowcs <= lt).astype(F32), axis=1, keepdims=True)
        idx_ref[e] = (blk * LANES + j).astype(I32)
        a1, a2, a3 = _split3(aff[e])
        arow = _dot(onehot, a1) + _dot(onehot, a2) + _dot(onehot, a3)
        gate_ref[e] = jnp.sum(jnp.where(lane_row == j, arow, 0.0), axis=1, keepdims=True)


def _select(aff, cap):
    n_exp = aff.shape[0]
    return pl.pallas_call(
        functools.partial(_select_kernel, cap=cap),
        grid=(1,),
        in_specs=[pl.BlockSpec(aff.shape, lambda i: (0, 0, 0))],
        out_specs=[pl.BlockSpec((n_exp, cap, 1), lambda i: (0, 0, 0)),
                   pl.BlockSpec((n_exp, cap, 1), lambda i: (0, 0, 0))],
        out_shape=[jax.ShapeDtypeStruct((n_exp, cap, 1), I32),
                   jax.ShapeDtypeStruct((n_exp, cap, 1), F32)],
        compiler_params=_params(("arbitrary",)),
        name="moe_select",
    )(aff)


def _row_copy(hbm, buf, idx_ref, p, gather, sem):
    row = idx_ref[0, 0, p]
    src, dst = hbm.at[pl.ds(row, 1), :], buf.at[pl.ds(p, 1), :]
    if not gather:
        src, dst = dst, src
    return pltpu.make_async_copy(src, dst, sem)


def _start_row_copies(hbm, buf, idx_ref, sem, n_rows, gather):
    def body(p, carry):
        _row_copy(hbm, buf, idx_ref, p, gather, sem).start()
        return carry

    lax.fori_loop(0, n_rows, body, 0, unroll=8)


def _wait_row_copies(buf, sem):
    pltpu.make_async_copy(buf, buf, sem).wait()


def _ffn_kernel(idx_prev_ref, idx_ref, idx_next_ref, gate_ref, h_hbm, x_hbm, wg_ref, wu_ref, wd_ref,
                mod_ref, o_hbm, xg_ref, xb_ref, y_ref, acc_ref, wgb_ref, wub_ref, wdb_ref, sem,
                *, k_lat, kt, nf, mc, n_exp, n_s, g0, n_g, n_x):
    del x_hbm
    e = pl.program_id(0)
    f = pl.program_id(1)
    x_sem, acc_sem, out_sem = sem.at[0], sem.at[1], sem.at[2]

    @pl.when(f == 0)
    def _():
        @pl.when(e == 0)
        def _():
            _start_row_copies(h_hbm, xg_ref, idx_ref, x_sem, kt, True)
            _start_row_copies(o_hbm, acc_ref, idx_ref, acc_sem, kt, True)
            _wait_row_copies(acc_ref, acc_sem)

        _wait_row_copies(xg_ref, x_sem)

        def cast(r0):
            xb_ref[pl.ds(r0, mc), :] = xg_ref[pl.ds(r0, mc), :].astype(BF16)

        _for_rows(kt, mc, cast)

    wgb_ref[...] = wg_ref[0, 0].astype(BF16)
    wub_ref[...] = wu_ref[0, 0].astype(BF16)
    wdb_ref[...] = wd_ref[0, 0].astype(BF16)

    chunks = kt // mc
    for fv in range(nf):
        jobs = []
        if fv < n_x:
            per = kt // (n_x * chunks)
            jobs.append((fv * chunks * per, per,
                         lambda p: _row_copy(h_hbm, xg_ref, idx_next_ref, p, True, x_sem).start()))
        if fv < n_s:
            per = kt // (n_s * chunks)
            jobs.append((fv * chunks * per, per,
                         lambda p: _row_copy(o_hbm, acc_ref, idx_prev_ref, p, False, out_sem).start()))
        if g0 <= fv < g0 + n_g:
            per = kt // (n_g * chunks)
            jobs.append(((fv - g0) * chunks * per, per,
                         lambda p: _row_copy(o_hbm, acc_ref, idx_ref, p, True, acc_sem).start()))

        @pl.when(f == fv)
        def _(fv=fv, jobs=jobs):
            if fv == g0:
                _wait_row_copies(acc_ref, out_sem)

            def ffn(r0):
                xb = xb_ref[pl.ds(r0, mc), :]
                a = _dot(xb, wgb_ref[...])
                u = _dot(xb, wub_ref[...])
                for first, per, issue in jobs:
                    base = first + (r0 // mc) * per
                    for k in range(per):
                        issue(base + k)
                part = _dot((_silu(a) * u).astype(BF16), wdb_ref[...])
                if fv == 0:
                    y_ref[pl.ds(r0, mc), :] = part
                else:
                    y_ref[pl.ds(r0, mc), :] += part

            _for_rows(kt, mc, ffn)

    @pl.when(f == nf - 1)
    def _():
        _wait_row_copies(acc_ref, acc_sem)
        mod = mod_ref[...]

        def rmw(r0):
            rows = r0 + lax.broadcasted_iota(I32, (mc, 1), 0)
            g2 = jnp.where(rows >= k_lat, mod[1:2, :], mod[0:1, :])
            upd = g2 * (y_ref[pl.ds(r0, mc), :] * gate_ref[0, pl.ds(r0, mc), :])
            acc_ref[pl.ds(r0, mc), :] = acc_ref[pl.ds(r0, mc), :] + upd

        _for_rows(kt, mc, rmw)

        @pl.when(e == n_exp - 1)
        def _():
            _start_row_copies(o_hbm, acc_ref, idx_ref, out_sem, kt, False)
            _wait_row_copies(acc_ref, out_sem)
            _wait_row_copies(xg_ref, x_sem)


def _moe_ffn(t_arr, hmod, idx, gate, mod, w_gate, w_up, w_down, layer, k_lat):
    nt, d = t_arr.shape
    _, n_exp, _, ff = w_gate.shape
    kt = idx.shape[1]
    tf = _pick(ff, MXU_DIM, LANES)
    nf = ff // tf
    mc = _pick(kt, 384, BF16_ROWS)
    assert nf >= 3
    n_s = max(1, nf // 3)
    g0 = min(n_s + 1, nf - 2)
    n_g = max(1, min(n_s, nf - 1 - g0))
    n_x = n_s + n_g
    chunks = kt // mc
    assert kt % (n_s * chunks) == 0 and kt % (n_g * chunks) == 0 and kt % (n_x * chunks) == 0
    idx3 = idx.reshape(n_exp, 1, kt)
    return pl.pallas_call(
        functools.partial(_ffn_kernel, k_lat=k_lat, kt=kt, nf=nf, mc=mc, n_exp=n_exp,
                          n_s=n_s, g0=g0, n_g=n_g, n_x=n_x),
        grid=(n_exp, nf),
        in_specs=[pl.BlockSpec((1, 1, kt), lambda e, f: (jnp.maximum(e - 1, 0), 0, 0),
                               memory_space=pltpu.SMEM),
                  pl.BlockSpec((1, 1, kt), lambda e, f: (e, 0, 0), memory_space=pltpu.SMEM),
                  pl.BlockSpec((1, 1, kt), lambda e, f: (jnp.minimum(e + 1, n_exp - 1), 0, 0),
                               memory_space=pltpu.SMEM),
                  pl.BlockSpec((1, kt, 1), lambda e, f: (e, 0, 0)),
                  pl.BlockSpec(memory_space=pl.ANY),
                  pl.BlockSpec(memory_space=pl.ANY),
                  pl.BlockSpec((1, 1, d, tf), lambda e, f: (layer, e, 0, f)),
                  pl.BlockSpec((1, 1, d, tf), lambda e, f: (layer, e, 0, f)),
                  pl.BlockSpec((1, 1, tf, d), lambda e, f: (layer, e, f, 0)),
                  pl.BlockSpec((SUBLANES, d), lambda e, f: (0, 5))],
        out_specs=pl.BlockSpec(memory_space=pl.ANY),
        out_shape=jax.ShapeDtypeStruct((nt, d), F32),
        input_output_aliases={5: 0},
        scratch_shapes=[pltpu.VMEM((kt, d), F32), pltpu.VMEM((kt, d), BF16), pltpu.VMEM((kt, d), F32),
                        pltpu.VMEM((kt, d), F32),
                        pltpu.VMEM((d, tf), BF16), pltpu.VMEM((d, tf), BF16), pltpu.VMEM((tf, d), BF16),
                        pltpu.SemaphoreType.DMA((3,))],
        compiler_params=_params(("arbitrary", "arbitrary")),
        name="moe_ffn",
    )(idx3, idx3, idx3, gate, hmod, t_arr, w_gate, w_up, w_down, mod)


def _pad_blocks(aff):
    nb = aff.shape[1]
    pad = (-nb) % BF16_ROWS
    if pad:
        aff = jnp.concatenate([aff, jnp.full((aff.shape[0], pad, LANES), -1.0, F32)], axis=1)
    return aff


def _moe_layer(t_arr, n_rows, s_lat, mod, g, router_w, w_gate, w_up, w_down, layer):
    n_exp = router_w.shape[1]
    hmod, aff = _router(t_arr, n_rows, s_lat, g, mod, router_w)
    aff = jnp.transpose(aff, (1, 0, 2))
    nb_lat = s_lat // LANES
    cap_lat = max(1, (CAPACITY_FACTOR * s_lat) // n_exp)
    idx, gate = _select(_pad_blocks(aff[:, :nb_lat]), cap_lat)
    idx = idx.reshape(n_exp, cap_lat)
    if n_rows > s_lat:
        cx = n_rows - s_lat
        cap_ctx = max(1, (CAPACITY_FACTOR * cx) // n_exp)
        idx_c, gate_c = _select(_pad_blocks(aff[:, nb_lat:]), cap_ctx)
        idx = jnp.concatenate([idx, idx_c.reshape(n_exp, cap_ctx) + s_lat], axis=1)
        gate = jnp.concatenate([gate, gate_c], axis=1)
    return _moe_ffn(t_arr, hmod, idx, gate, mod, w_gate, w_up, w_down, layer, cap_lat)


def _final_kernel(x_ref, g_ref, o_ref):
    x = x_ref[...]
    ms = jnp.mean(x * x, axis=-1, keepdims=True)
    o_ref[...] = x * lax.rsqrt(ms + RMS_EPS) * g_ref[...]


def _final_norm(t_arr, s_lat, g):
    d = t_arr.shape[1]
    tm = _pick(s_lat, 512, ROW_CHUNK)
    return pl.pallas_call(
        _final_kernel,
        grid=(s_lat // tm,),
        in_specs=[pl.BlockSpec((tm, d), lambda i: (i, 0)), pl.BlockSpec((1, d), lambda i: (0, 0))],
        out_specs=pl.BlockSpec((tm, d), lambda i: (i, 0)),
        out_shape=jax.ShapeDtypeStruct((s_lat, d), F32),
        compiler_params=_params(("arbitrary",)),
        name="final_norm",
    )(t_arr, g.reshape(1, d))


def kernel(x, c, ctx, c_ctx, mod_w, mod_b, norm_mix, norm_ffn, conv_w_in, conv_w_dw, conv_w_out,
           diff_w_qkv, diff_lambda, diff_subln, diff_w_out, gla_w_in, gla_gate_w1, gla_gate_w2,
           gla_gate_b, gla_onorm, gla_w_out, router_w, exp_w_gate, exp_w_up, exp_w_down, final_norm):
    batch, s_lat, d = x.shape
    cx = ctx.shape[1]
    depth = mod_w.shape[0]
    assert batch == 1 and s_lat % GRID_W == 0
    nt = s_lat + cx
    t_arr = jnp.concatenate([x[0], ctx[0]], axis=0)
    cvec8 = jnp.concatenate([c, c_ctx[None, :], jnp.zeros((SUBLANES - 2, d), F32)], axis=0)
    mods = _modulation(cvec8, mod_w, mod_b)
    cos_t, sin_t = _rope_tables(nt, s_lat)

    for i in range(depth):
        kind, j = i % N_MIXERS, i // N_MIXERS
        ctx_next = i < depth - 1
        ctx_read = ctx_next or kind != 0
        n_rows = nt if ctx_read else s_lat
        if t_arr.shape[0] != n_rows:
            t_arr = t_arr[:n_rows]
        mod = mods[i]
        if kind == 0:
            t_arr = _short_conv_layer(t_arr, n_rows, s_lat, mod, norm_mix[i],
                                      conv_w_in[j], conv_w_dw[j], conv_w_out[j])
        elif kind == 1:
            lambda_init = DIFF_LAMBDA_A - DIFF_LAMBDA_B * math.exp(-DIFF_LAMBDA_C * i)
            t_arr = _diff_layer(t_arr, s_lat, mod, norm_mix[i], diff_w_qkv[j], diff_lambda[j],
                                diff_subln[j], diff_w_out[j], lambda_init, cos_t, sin_t)
        else:
            t_arr = _gla_layer(t_arr, s_lat, mod, norm_mix[i], gla_w_in[j], gla_gate_w1[j],
                               gla_gate_w2[j], gla_gate_b[j], gla_onorm[j], gla_w_out[j])
        n_moe = nt if ctx_next else s_lat
        if t_arr.shape[0] != n_moe:
            t_arr = t_arr[:n_moe]
        t_arr = _moe_layer(t_arr, n_moe, s_lat, mod, norm_ffn[i], router_w[i],
                           exp_w_gate, exp_w_up, exp_w_down, i)
    return _final_norm(t_arr, s_lat, final_norm)[None]
```

```python
import functools
import math

import jax
import jax.numpy as jnp
from jax import lax
from jax.experimental import pallas as pl
from jax.experimental.pallas import tpu as pltpu

F32 = jnp.float32
BF16 = jnp.bfloat16
I32 = jnp.int32

GRID_W = 64
N_MIXERS = 3
RMS_EPS = 1e-6
DIFF_HEAD_DIM = 128
DIFF_SUBLN_EPS = 1e-5
DIFF_LAMBDA_A = 0.8
DIFF_LAMBDA_B = 0.6
DIFF_LAMBDA_C = 0.3
ROPE_THETA = 10000.0
GLA_HEADS = 4
GLA_TAU = 16.0
GLA_CHUNK = 64
GLA_SUB = 16
GLA_FACTOR_BOUND = 60.0
CAPACITY_FACTOR = 2

LANES = 128
SUBLANES = 8
BF16_ROWS = 16
MXU_DIM = 256
VMEM_LIMIT_BYTES = 56 * 1024 * 1024
ROW_CHUNK = 128
MATMUL_ROWS = 768
ATTN_QUERIES = 256
ATTN_KEYS = 512


def _params(sem, vmem=VMEM_LIMIT_BYTES):
    return pltpu.CompilerParams(dimension_semantics=sem, vmem_limit_bytes=vmem)


def _pick(n, cap, mult):
    best = None
    for d in range(mult, min(n, cap) + 1, mult):
        if n % d == 0:
            best = d
    assert best is not None, (n, cap, mult)
    return best


def _dot(a, b):
    return jnp.dot(a, b, preferred_element_type=F32)


def _dot_nt(a, b):
    return lax.dot_general(a, b, (((1,), (1,)), ((), ())), preferred_element_type=F32)


def _dot_tn(a, b):
    return lax.dot_general(a, b, (((0,), (0,)), ((), ())), preferred_element_type=F32)


def _split2(x):
    hi = x.astype(BF16)
    lo = (x - hi.astype(F32)).astype(BF16)
    return hi, lo


def _split3(x):
    a = x.astype(BF16)
    r = x - a.astype(F32)
    b = r.astype(BF16)
    c = (r - b.astype(F32)).astype(BF16)
    return a, b, c


def _sigmoid(x):
    return 1.0 / (1.0 + jnp.exp(-x))


def _silu(x):
    return x * _sigmoid(x)


def _for_rows(n_rows, chunk, fn):
    def body(r, carry):
        fn(pl.multiple_of(r * chunk, chunk))
        return carry

    lax.fori_loop(0, n_rows // chunk, body, 0)


def _mod_vectors(g, mod, shift_col, scale_col):
    d = g.shape[-1]
    return g * (1.0 + mod[0:2, scale_col:scale_col + d]), mod[0:2, shift_col:shift_col + d]


def _norm_mod(x, gain, shift, is_ctx):
    ms = jnp.mean(x * x, axis=-1, keepdims=True)
    gain = jnp.where(is_ctx, gain[1:2], gain[0:1])
    shift = jnp.where(is_ctx, shift[1:2], shift[0:1])
    return x * lax.rsqrt(ms + RMS_EPS) * gain + shift


def _mod_kernel(c_ref, w_ref, b_ref, o_ref):
    c = c_ref[...]
    s_hi, s_lo = _split2(_silu(c))
    w_hi, w_lo = _split2(w_ref[0])
    acc = _dot(s_hi, w_hi) + _dot(s_lo, w_hi) + _dot(s_hi, w_lo)
    o_ref[0] = acc + b_ref[0]


def _modulation(cvec8, mod_w, mod_b):
    depth, d, n6 = mod_w.shape
    tn = _pick(n6, 1024, LANES)
    return pl.pallas_call(
        _mod_kernel,
        grid=(depth, n6 // tn),
        in_specs=[
            pl.BlockSpec((SUBLANES, d), lambda l, j: (0, 0)),
            pl.BlockSpec((1, d, tn), lambda l, j: (l, 0, j)),
            pl.BlockSpec((1, 1, tn), lambda l, j: (l, 0, j)),
        ],
        out_specs=pl.BlockSpec((1, SUBLANES, tn), lambda l, j: (l, 0, j)),
        out_shape=jax.ShapeDtypeStruct((depth, SUBLANES, n6), F32),
        compiler_params=_params(("arbitrary", "arbitrary")),
        name="modulation",
    )(cvec8, mod_w, mod_b.reshape(depth, 1, n6))


def _rope_kernel(inv_ref, sgn_ref, cos_ref, sin_ref, *, s_lat, tm):
    i = pl.program_id(0)
    t = i * tm + lax.broadcasted_iota(I32, (tm, LANES), 0)
    lane = lax.broadcasted_iota(I32, (tm, LANES), 1)
    row = t // GRID_W
    col = t % GRID_W
    pos = jnp.where(lane < DIFF_HEAD_DIM // 2, row, col)
    pos = jnp.where(t < s_lat, pos, 0)
    ang = pos.astype(F32) * inv_ref[...]
    cos_ref[...] = jnp.cos(ang)
    sin_ref[...] = jnp.sin(ang) * sgn_ref[...]


def _rope_tables(nt, s_lat):
    quarter = DIFF_HEAD_DIM // 4
    inv = ROPE_THETA ** (-jnp.arange(quarter, dtype=F32) / quarter)
    inv128 = jnp.tile(inv, 4).reshape(1, LANES)
    sgn = jnp.tile(jnp.concatenate([-jnp.ones((quarter,), F32), jnp.ones((quarter,), F32)]), 2)
    tm = _pick(nt, 1024, LANES)
    return pl.pallas_call(
        functools.partial(_rope_kernel, s_lat=s_lat, tm=tm),
        grid=(nt // tm,),
        in_specs=[pl.BlockSpec((1, LANES), lambda i: (0, 0)),
                  pl.BlockSpec((1, LANES), lambda i: (0, 0))],
        out_specs=[pl.BlockSpec((tm, LANES), lambda i: (i, 0)),
                   pl.BlockSpec((tm, LANES), lambda i: (i, 0))],
        out_shape=[jax.ShapeDtypeStruct((nt, LANES), F32)] * 2,
        compiler_params=_params(("arbitrary",)),
        name="rope_tables",
    )(inv128, sgn.reshape(1, LANES))


def _nmm_kernel(*refs, n_w, n_extra, n_out, s_lat, tm, mc, shift_col, scale_col, epilogue):
    x_ref, g_ref, mod_ref = refs[:3]
    w_refs = refs[3:3 + n_w]
    extra = refs[3 + n_w:3 + n_w + n_extra]
    outs = refs[3 + n_w + n_extra:3 + n_w + n_extra + n_out]
    h_ref = refs[-1]
    i = pl.program_id(0)
    j = pl.program_id(1)

    @pl.when(j == 0)
    def _():
        gain, shift = _mod_vectors(g_ref[...], mod_ref[...], shift_col, scale_col)

        def slab(r0):
            h = _norm_mod(x_ref[pl.ds(r0, ROW_CHUNK), :], gain, shift, i * tm + r0 >= s_lat)
            h_ref[pl.ds(r0, ROW_CHUNK), :] = h.astype(BF16)

        _for_rows(tm, ROW_CHUNK, slab)

    for applies, ep in epilogue:
        @pl.when(applies(j))
        def _(ep=ep):
            for r in range(tm // mc):
                rows = pl.ds(r * mc, mc)
                h = h_ref[rows, :]
                ep([_dot(h, w[...]) for w in w_refs], extra, outs, j, rows)


def _always(j):
    return j >= 0


def _norm_mod_matmul(t_arr, n_rows, s_lat, g, mod, shift_col, scale_col, w, w_col_blocks, tn,
                     n_tiles, epilogue, extra, extra_specs, out_shapes, out_specs, name):
    d = t_arr.shape[1]
    assert w.dtype == BF16
    tm = _pick(n_rows, 1408, ROW_CHUNK)
    mc = _pick(tm, MATMUL_ROWS, BF16_ROWS)
    w_specs = [pl.BlockSpec((d, tn), functools.partial(lambda i, j, o: (0, o + j), o=o))
               for o in w_col_blocks]
    kern = functools.partial(
        _nmm_kernel, n_w=len(w_col_blocks), n_extra=len(extra), n_out=len(out_shapes),
        s_lat=s_lat, tm=tm, mc=mc, shift_col=shift_col, scale_col=scale_col, epilogue=epilogue)
    return pl.pallas_call(
        kern,
        grid=(n_rows // tm, n_tiles),
        in_specs=[pl.BlockSpec((tm, d), lambda i, j: (i, 0)),
                  pl.BlockSpec((1, d), lambda i, j: (0, 0)),
                  pl.BlockSpec(mod.shape, lambda i, j: (0, 0))]
                 + w_specs + [s(tm) for s in extra_specs],
        out_specs=[s(tm) for s in out_specs],
        out_shape=out_shapes,
        scratch_shapes=[pltpu.VMEM((tm, d), BF16)],
        compiler_params=_params(("arbitrary", "arbitrary")),
        name=name,
    )(t_arr, g.reshape(1, d), mod, *([w] * len(w_col_blocks)), *extra)


def _gated_residual_rows(a_ref, w_ref, x_ref, mod_ref, o_ref, i, s_lat, tm, mc):
    mod = mod_ref[...]
    for r in range(tm // mc):
        rows = pl.ds(r * mc, mc)
        acc = _dot(a_ref[rows, :], w_ref[...])
        tok = i * tm + r * mc + lax.broadcasted_iota(I32, (mc, 1), 0)
        gate = jnp.where(tok >= s_lat, mod[1:2, :], mod[0:1, :])
        o_ref[rows, :] = x_ref[rows, :] + gate * acc


def _mmres_kernel(a_ref, w_ref, x_ref, mod_ref, o_ref, *, s_lat, tm, mc):
    _gated_residual_rows(a_ref, w_ref, x_ref, mod_ref, o_ref, pl.program_id(0), s_lat, tm, mc)


def _matmul_residual(a, w, t_arr, n_rows, s_lat, mod, gate_col, name):
    k = a.shape[1]
    d = t_arr.shape[1]
    assert w.dtype == BF16
    tm = _pick(n_rows, 1408, ROW_CHUNK)
    mc = _pick(tm, MATMUL_ROWS, BF16_ROWS)
    tn = _pick(d, 512, LANES)
    gblk = gate_col // tn
    return pl.pallas_call(
        functools.partial(_mmres_kernel, s_lat=s_lat, tm=tm, mc=mc),
        grid=(n_rows // tm, d // tn),
        in_specs=[pl.BlockSpec((tm, k), lambda i, j: (i, 0)),
                  pl.BlockSpec((k, tn), lambda i, j: (0, j)),
                  pl.BlockSpec((tm, tn), lambda i, j: (i, j)),
                  pl.BlockSpec((SUBLANES, tn), lambda i, j: (0, gblk + j))],
        out_specs=pl.BlockSpec((tm, tn), lambda i, j: (i, j)),
        out_shape=jax.ShapeDtypeStruct(t_arr.shape, F32),
        input_output_aliases={2: 0},
        compiler_params=_params(("arbitrary", "arbitrary")),
        name=name,
    )(a, w, t_arr, mod)


def _conv_in_epilogue(accs, extra, outs, j, rows):
    b, c, u = accs
    outs[0][rows, :] = b.astype(BF16)
    outs[1][rows, :] = c * u


def _conv_out_kernel(b_ref, v_ref, vp_ref, vn_ref, dw_ref, w_ref, x_ref, mod_ref, o_ref, a_ref, buf_ref,
                     *, s_lat, n_rows, tm, mc):
    i = pl.program_id(0)
    j = pl.program_id(1)

    @pl.when(j == 0)
    def _():
        buf_ref[0:SUBLANES, :] = vp_ref[...]
        buf_ref[SUBLANES + tm:2 * SUBLANES + tm, :] = vn_ref[...]

        def copy(r0):
            buf_ref[pl.ds(SUBLANES + r0, ROW_CHUNK), :] = v_ref[pl.ds(r0, ROW_CHUNK), :]

        _for_rows(tm, ROW_CHUNK, copy)
        dw = dw_ref[...]

        def slab(r0):
            win = buf_ref[pl.ds(r0, ROW_CHUNK + 2 * SUBLANES), :]
            prev = win[SUBLANES - 1:SUBLANES - 1 + ROW_CHUNK]
            cur = win[SUBLANES:SUBLANES + ROW_CHUNK]
            nxt = win[SUBLANES + 1:SUBLANES + 1 + ROW_CHUNK]
            rows = i * tm + r0 + lax.broadcasted_iota(I32, (ROW_CHUNK, 1), 0)
            has_prev = (rows != 0) & (rows != s_lat)
            has_next = (rows != s_lat - 1) & (rows != n_rows - 1)
            conv = (jnp.where(has_prev, prev, 0.0) * dw[0:1] + cur * dw[1:2]
                    + jnp.where(has_next, nxt, 0.0) * dw[2:3])
            a = b_ref[pl.ds(r0, ROW_CHUNK), :].astype(F32) * conv
            a_ref[pl.ds(r0, ROW_CHUNK), :] = a.astype(BF16)

        _for_rows(tm, ROW_CHUNK, slab)

    _gated_residual_rows(a_ref, w_ref, x_ref, mod_ref, o_ref, i, s_lat, tm, mc)


def _short_conv_layer(t_arr, n_rows, s_lat, mod, g, w_in, w_dw, w_out):
    nt, d = t_arr.shape
    w_in, w_out = w_in.astype(BF16), w_out.astype(BF16)
    tn = _pick(d, 512, LANES)
    nblk = d // tn
    b, v = _norm_mod_matmul(
        t_arr, n_rows, s_lat, g, mod, 0, d, w_in, [0, nblk, 2 * nblk], tn, nblk,
        [(_always, _conv_in_epilogue)], [], [],
        [jax.ShapeDtypeStruct((n_rows, d), BF16), jax.ShapeDtypeStruct((n_rows, d), F32)],
        [lambda tm: pl.BlockSpec((tm, tn), lambda i, j: (i, j))] * 2, "conv_in")
    tm = _pick(n_rows, 1024, ROW_CHUNK)
    mc = _pick(tm, MATMUL_ROWS, BF16_ROWS)
    hb = tm // SUBLANES
    last = n_rows // SUBLANES - 1
    gblk = (2 * d) // tn
    return pl.pallas_call(
        functools.partial(_conv_out_kernel, s_lat=s_lat, n_rows=n_rows, tm=tm, mc=mc),
        grid=(n_rows // tm, nblk),
        in_specs=[pl.BlockSpec((tm, d), lambda i, j: (i, 0)),
                  pl.BlockSpec((tm, d), lambda i, j: (i, 0)),
                  pl.BlockSpec((SUBLANES, d), lambda i, j: (jnp.maximum(i * hb - 1, 0), 0)),
                  pl.BlockSpec((SUBLANES, d), lambda i, j: (jnp.minimum((i + 1) * hb, last), 0)),
                  pl.BlockSpec(w_dw.shape, lambda i, j: (0, 0)),
                  pl.BlockSpec((d, tn), lambda i, j: (0, j)),
                  pl.BlockSpec((tm, tn), lambda i, j: (i, j)),
                  pl.BlockSpec((SUBLANES, tn), lambda i, j: (0, gblk + j))],
        out_specs=pl.BlockSpec((tm, tn), lambda i, j: (i, j)),
        out_shape=jax.ShapeDtypeStruct((nt, d), F32),
        input_output_aliases={6: 0},
        scratch_shapes=[pltpu.VMEM((tm, d), BF16),
                        pltpu.VMEM((tm + 2 * SUBLANES, d), F32)],
        compiler_params=_params(("arbitrary", "arbitrary")),
        name="conv_out",
    )(b, v, v, v, w_dw, w_out, t_arr, mod)


def _rope_rot(a):
    half = DIFF_HEAD_DIM // 4
    lane = lax.broadcasted_iota(I32, a.shape, 1)
    up = pltpu.roll(a, LANES - half, axis=1)
    dn = pltpu.roll(a, half, axis=1)
    return jnp.where(lane % (2 * half) < half, up, dn)


def _rope_epilogue(accs, extra, outs, j, rows, *, n_q, tn):
    acc = accs[0]
    cos_ref, sin_ref = extra
    scale = jnp.where(j < n_q, DIFF_HEAD_DIM ** -0.5 * math.log2(math.e), 1.0).astype(F32)
    cos = cos_ref[rows, :] * scale
    sin = sin_ref[rows, :] * scale
    for gidx in range(tn // LANES):
        a = acc[:, gidx * LANES:(gidx + 1) * LANES]
        outs[0][rows, gidx * LANES:(gidx + 1) * LANES] = (a * cos + _rope_rot(a) * sin).astype(BF16)


def _cast_epilogue(accs, extra, outs, j, rows):
    outs[0][rows, :] = accs[0].astype(outs[0].dtype)


def _attn_body(q_ref, k_ref, v_ref, lam_ref, sub_ref, o_ref, k_lo, k_hi, lambda_init):
    hd = DIFF_HEAD_DIM
    tq = q_ref.shape[0]
    chunks = [slice(lo, min(lo + ATTN_KEYS, k_hi)) for lo in range(k_lo, k_hi, ATTN_KEYS)]
    m = jnp.full((2 * tq, 1), -jnp.inf, F32)
    l = jnp.zeros((2 * tq, 1), F32)
    acc = jnp.zeros((2 * tq, 2 * hd), F32)
    for sl in chunks:
        s = jnp.concatenate([_dot_nt(q_ref[:, t * hd:(t + 1) * hd], k_ref[sl, t * hd:(t + 1) * hd])
                             for t in range(2)], axis=0)
        m_new = jnp.maximum(m, jnp.max(s, axis=1, keepdims=True))
        alpha = jnp.exp2(m - m_new)
        p = jnp.exp2(s - m_new)
        l = alpha * l + jnp.sum(p, axis=1, keepdims=True)
        acc = alpha * acc + _dot(p.astype(BF16), v_ref[sl, :])
        m = m_new
    l0, l1, acc0, acc1 = l[:tq], l[tq:], acc[:tq], acc[tq:]
    lp = lam_ref[...]
    lam = (jnp.exp(jnp.sum(lp[0:1] * lp[1:2], axis=1, keepdims=True))
           - jnp.exp(jnp.sum(lp[2:3] * lp[3:4], axis=1, keepdims=True)) + lambda_init)
    o = acc0 * (1.0 / l0) - acc1 * (lam / l1)
    ms = jnp.mean(o * o, axis=1, keepdims=True)
    o = o * lax.rsqrt(ms + DIFF_SUBLN_EPS) * sub_ref[...] * (1.0 - lambda_init)
    o_ref[...] = o.astype(BF16)


def _attn_kernel(q_ref, k_ref, v_ref, lam_ref, sub_ref, o_ref, *, s_lat, lambda_init):
    tq = q_ref.shape[0]
    nt = k_ref.shape[0]
    n_lat_blocks = s_lat // tq
    i = pl.program_id(1)

    @pl.when(i < n_lat_blocks)
    def _():
        _attn_body(q_ref, k_ref, v_ref, lam_ref, sub_ref, o_ref, 0, nt, lambda_init)

    @pl.when(i >= n_lat_blocks)
    def _():
        _attn_body(q_ref, k_ref, v_ref, lam_ref, sub_ref, o_ref, s_lat, nt, lambda_init)


def _diff_attention(qkv, s_lat, lam_p, subln, lambda_init, d):
    nt = qkv.shape[0]
    hw = 2 * DIFF_HEAD_DIM
    heads = d // hw
    tq = ATTN_QUERIES
    assert s_lat % tq == 0 and (nt - s_lat) % tq == 0
    return pl.pallas_call(
        functools.partial(_attn_kernel, s_lat=s_lat, lambda_init=lambda_init),
        grid=(heads, nt // tq),
        in_specs=[pl.BlockSpec((tq, hw), lambda h, i: (i, h)),
                  pl.BlockSpec((nt, hw), lambda h, i: (0, heads + h)),
                  pl.BlockSpec((nt, hw), lambda h, i: (0, 2 * heads + h)),
                  pl.BlockSpec(lam_p.shape, lambda h, i: (0, 0)),
                  pl.BlockSpec((1, hw), lambda h, i: (0, 0))],
        out_specs=pl.BlockSpec((tq, hw), lambda h, i: (i, h)),
        out_shape=jax.ShapeDtypeStruct((nt, d), BF16),
        compiler_params=_params(("arbitrary", "arbitrary")),
        name="diff_attn",
    )(qkv, qkv, qkv, lam_p, subln.reshape(1, hw))


def _diff_layer(t_arr, s_lat, mod, g, w_qkv, lam_p, subln, w_out, lambda_init, cos_t, sin_t):
    nt, d = t_arr.shape
    w_qkv, w_out = w_qkv.astype(BF16), w_out.astype(BF16)
    tn = _pick(d, 512, LANES)
    nblk = d // tn
    ep = [(lambda j: j < 2 * nblk, functools.partial(_rope_epilogue, n_q=nblk, tn=tn)),
          (lambda j: j >= 2 * nblk, _cast_epilogue)]
    (qkv,) = _norm_mod_matmul(
        t_arr, nt, s_lat, g, mod, 0, d, w_qkv, [0], tn, 3 * nblk, ep,
        [cos_t, sin_t], [lambda tm: pl.BlockSpec((tm, LANES), lambda i, j: (i, 0))] * 2,
        [jax.ShapeDtypeStruct((nt, 3 * d), BF16)],
        [lambda tm: pl.BlockSpec((tm, tn), lambda i, j: (i, j))], "diff_qkv")
    o = _diff_attention(qkv, s_lat, lam_p, subln, lambda_init, d)
    return _matmul_residual(o, w_out, t_arr, nt, s_lat, mod, 2 * d, "diff_out")


def _gla_proj_epilogue(accs, extra, outs, j, rows, *, n_q, dk):
    scale = jnp.where(j < n_q, dk ** -0.5, 1.0).astype(F32)
    outs[0][rows, :] = accs[0] * scale


def _log_sigmoid(z):
    return jnp.minimum(z, 0.0) - jnp.log(1.0 + jnp.exp(-jnp.abs(z)))


def _gla_time(c, rev):
    rowi = lax.broadcasted_iota(I32, (c, 1), 0)
    coli = lax.broadcasted_iota(I32, (1, c), 1)
    return rowi, coli, ((c - 1 - rowi) if rev else rowi), ((c - 1 - coli) if rev else coli)


def _gla_decay_kernel(z_ref, w2_ref, bias_ref, bf_ref, bb_ref, *, c):
    rows = z_ref.shape[0]
    grp = _pick(rows, MXU_DIM, c)
    z_hi, z_lo = _split2(z_ref[...])
    ri = lax.broadcasted_iota(I32, (grp, grp), 0)
    ci = lax.broadcasted_iota(I32, (grp, grp), 1)
    same_chunk = (ri // c) == (ci // c)
    for t, (o_ref, rev) in enumerate(((bf_ref, False), (bb_ref, True))):
        w_hi, w_lo = _split2(w2_ref[t])
        z = _dot(z_hi, w_hi) + _dot(z_lo, w_hi) + _dot(z_hi, w_lo) + bias_ref[t]
        g = _log_sigmoid(z) * (1.0 / GLA_TAU)
        tri = (same_chunk & ((ci >= ri) if rev else (ci <= ri))).astype(BF16)
        for m in range(rows // grp):
            g1, g2, g3 = _split3(g[m * grp:(m + 1) * grp])
            o_ref[m * grp:(m + 1) * grp, :] = _dot(tri, g1) + _dot(tri, g2) + _dot(tri, g3)


def _gla_block_first(b, rev):
    c = b.shape[0]
    sub = GLA_SUB
    parts = []
    for m in range(c // sub):
        r = m * sub + (sub - 1 if rev else 0)
        parts.append(jnp.broadcast_to(b[r:r + 1, :], (sub, b.shape[1])))
    return jnp.concatenate(parts, axis=0)


def _gla_scores_factored(q, k, b, b_first, rev):
    c = q.shape[0]
    sub = GLA_SUB
    _, _, tau, tau_col = _gla_time(c, rev)
    qt = q * jnp.exp(b - b_first)
    q_parts, k_parts = [], []
    for blk in range(c // sub):
        rb = (c - 1 - sub * blk) if rev else sub * blk
        b_at = b[rb:rb + 1, :]
        in_blk = (tau // sub) == blk
        upto = tau < sub * (blk + 1)
        q_parts.append(jnp.where(in_blk, qt, 0.0).astype(BF16))
        k_parts.append(jnp.where(upto, k * jnp.exp(jnp.where(upto, b_at - b, 0.0)), 0.0).astype(BF16))
    a = _dot_nt(jnp.concatenate(q_parts, axis=1), jnp.concatenate(k_parts, axis=1))
    return jnp.where(tau_col <= tau, a, 0.0)


def _gla_scores_exact(q, k, b, kpad_ref, bpad_ref, rev):
    c = q.shape[0]
    sub = GLA_SUB
    rowi, coli, tau, _ = _gla_time(c, rev)
    kpad_ref[sub:sub + c, :] = k
    bpad_ref[sub:sub + c, :] = b
    a_mat = jnp.zeros((c, c), F32)
    for dlt in range(sub):
        off = sub + dlt if rev else sub - dlt
        k_sh = kpad_ref[off:off + c, :]
        b_sh = bpad_ref[off:off + c, :]
        valid = (tau % sub) >= dlt
        e = jnp.exp(jnp.where(valid, b - b_sh, 0.0))
        dsum = jnp.sum(q * k_sh * e, axis=1, keepdims=True)
        partner = (rowi + dlt) if rev else (rowi - dlt)
        a_mat = a_mat + jnp.where((coli == partner) & valid, dsum, 0.0)
    q_parts, k_parts = [], []
    for blk in range(1, c // sub):
        ref_row = (c - sub * blk) if rev else sub * blk - 1
        b_at = b[ref_row:ref_row + 1, :]
        in_blk = (tau // sub) == blk
        earlier = tau < sub * blk
        qt = jnp.where(in_blk, q * jnp.exp(jnp.where(in_blk, b - b_at, 0.0)), 0.0)
        kt = jnp.where(earlier, k * jnp.exp(jnp.where(earlier, b_at - b, 0.0)), 0.0)
        q_parts.append(qt.astype(BF16))
        k_parts.append(kt.astype(BF16))
    return a_mat + _dot_nt(jnp.concatenate(q_parts, axis=1), jnp.concatenate(k_parts, axis=1))


def _gla_chunk_output(q, k, v, b, a_mat, s_ref, rev):
    c = q.shape[0]
    rowi = lax.broadcasted_iota(I32, (c, 1), 0)
    end_row = 0 if rev else c - 1
    b_end = b[end_row:end_row + 1, :]
    vb = v.astype(BF16)
    s_old = s_ref[...]
    o = _dot(a_mat.astype(BF16), vb) + _dot((q * jnp.exp(b)).astype(BF16), s_old.astype(BF16))
    upd = _dot_tn((k * jnp.exp(b_end - b)).astype(BF16), vb)
    d1, d2, d3 = _split3(jnp.where(rowi == end_row, b, 0.0))
    ones = jnp.ones((c, LANES), BF16)
    decay_col = _dot_tn(d1, ones) + _dot_tn(d2, ones) + _dot_tn(d3, ones)
    s_ref[...] = jnp.exp(decay_col[:, 0:1]) * s_old + upd
    return o


def _gla_kernel(qf_ref, kf_ref, vf_ref, bf_ref, qb_ref, kb_ref, vb_ref, bb_ref,
                of_ref, ob_ref, s_ref, b1_ref, pad_ref, *, heads):
    @pl.when(pl.program_id(0) == 0)
    def _():
        s_ref[...] = jnp.zeros_like(s_ref)
        pad_ref[...] = jnp.zeros_like(pad_ref)

    dk = qf_ref.shape[1] // heads
    dv = vf_ref.shape[1] // heads
    dirs = ((qf_ref, kf_ref, vf_ref, bf_ref, of_ref, False), (qb_ref, kb_ref, vb_ref, bb_ref, ob_ref, True))
    excess = jnp.zeros((1, 1), F32)
    for t, (_, _, _, b_ref, _, rev) in enumerate(dirs):
        b = b_ref[...]
        b_first = _gla_block_first(b, rev)
        b1_ref[t] = b_first
        excess = jnp.maximum(excess, jnp.max(jnp.max(b_first - b, axis=1, keepdims=True), axis=0, keepdims=True))
    mild = jnp.max(excess) <= GLA_FACTOR_BOUND

    def run(exact):
        for t, (q_ref, k_ref, v_ref, b_ref, o_ref, rev) in enumerate(dirs):
            for h in range(heads):
                ks = slice(h * dk, (h + 1) * dk)
                vs = slice(h * dv, (h + 1) * dv)
                q, k, v, b = q_ref[:, ks], k_ref[:, ks], v_ref[:, vs], b_ref[:, ks]
                if exact:
                    a_mat = _gla_scores_exact(q, k, b, pad_ref.at[t, h, 0], pad_ref.at[t, h, 1], rev)
                else:
                    a_mat = _gla_scores_factored(q, k, b, b1_ref[t, :, ks], rev)
                o_ref[:, vs] = _gla_chunk_output(q, k, v, b, a_mat, s_ref.at[t, h], rev)

    pl.when(mild)(lambda: run(False))
    pl.when(jnp.logical_not(mild))(lambda: run(True))


def _gla_post_kernel(of_ref, ob_ref, g_ref, on_ref, a_ref, *, dv):
    o = of_ref[...] + ob_ref[...]
    gate = _silu(g_ref[...])
    for h in range(o.shape[1] // dv):
        oh = o[:, h * dv:(h + 1) * dv]
        ms = jnp.mean(oh * oh, axis=1, keepdims=True)
        y = oh * lax.rsqrt(ms + RMS_EPS) * on_ref[...] * gate[:, h * dv:(h + 1) * dv]
        a_ref[:, h * dv:(h + 1) * dv] = y.astype(BF16)


def _gla_layer(t_arr, s_lat, mod, g, w_in, gate_w1, gate_w2, gate_b, onorm, w_out):
    nt, d = t_arr.shape
    w_in, w_out = w_in.astype(BF16), w_out.astype(BF16)
    cx = nt - s_lat
    heads = GLA_HEADS
    dk = d // (2 * heads)
    dv = d // heads
    rank = gate_w1.shape[2]
    n_proj = w_in.shape[1]
    tn = _pick(d, 512, LANES)
    nq_tiles = (heads * dk) // tn
    proj_ep = functools.partial(_gla_proj_epilogue, n_q=nq_tiles, dk=dk)
    (proj,) = _norm_mod_matmul(
        t_arr, nt, s_lat, g, mod, 0, d, w_in, [0], tn, n_proj // tn, [(_always, proj_ep)], [], [],
        [jax.ShapeDtypeStruct((nt, n_proj), F32)],
        [lambda tm: pl.BlockSpec((tm, tn), lambda i, j: (i, j))], "gla_proj")
    w1cat = jnp.concatenate([gate_w1[0], gate_w1[1]], axis=1)
    w1pad = jnp.pad(w1cat, ((0, 0), (0, LANES - 2 * rank))).astype(BF16)
    (z1,) = _norm_mod_matmul(
        t_arr, nt, s_lat, g, mod, 0, d, w1pad, [0], LANES, 1, [(_always, _cast_epilogue)], [], [],
        [jax.ShapeDtypeStruct((nt, LANES), F32)],
        [lambda tm: pl.BlockSpec((tm, LANES), lambda i, j: (i, 0))], "gla_gate_rank")
    w2pad = jnp.zeros((2, LANES, heads * dk), F32)
    w2pad = w2pad.at[0, 0:rank].set(gate_w2[0]).at[1, rank:2 * rank].set(gate_w2[1])
    bias = gate_b.reshape(2, 1, heads * dk)

    c = GLA_CHUNK
    n_lat, n_ctx = s_lat // c, cx // c
    nch = n_lat + n_ctx

    def fwd(s):
        return jnp.where(s < n_ctx, n_lat + s, s - n_ctx)

    def bwd(s):
        return nch - 1 - s

    tz = _pick(nt, 512, c)
    b_f, b_b = pl.pallas_call(
        functools.partial(_gla_decay_kernel, c=c),
        grid=(nt // tz,),
        in_specs=[pl.BlockSpec((tz, LANES), lambda i: (i, 0)),
                  pl.BlockSpec(w2pad.shape, lambda i: (0, 0, 0)),
                  pl.BlockSpec(bias.shape, lambda i: (0, 0, 0))],
        out_specs=[pl.BlockSpec((tz, heads * dk), lambda i: (i, 0))] * 2,
        out_shape=[jax.ShapeDtypeStruct((nt, heads * dk), F32)] * 2,
        compiler_params=_params(("arbitrary",)),
        name="gla_decay",
    )(z1, w2pad, bias)

    def specs(row_of):
        return [pl.BlockSpec((c, heads * dk), lambda s: (row_of(s), 0)),
                pl.BlockSpec((c, heads * dk), lambda s: (row_of(s), 1)),
                pl.BlockSpec((c, heads * dv), lambda s: (row_of(s), (2 * heads * dk) // (heads * dv))),
                pl.BlockSpec((c, heads * dk), lambda s: (row_of(s), 0))]

    pad_rows = c + 2 * GLA_SUB
    of, ob = pl.pallas_call(
        functools.partial(_gla_kernel, heads=heads),
        grid=(nch,),
        in_specs=specs(fwd) + specs(bwd),
        out_specs=[pl.BlockSpec((c, heads * dv), lambda s: (fwd(s), 0)),
                   pl.BlockSpec((c, heads * dv), lambda s: (bwd(s), 0))],
        out_shape=[jax.ShapeDtypeStruct((nt, d), F32)] * 2,
        scratch_shapes=[pltpu.VMEM((2, heads, dk, dv), F32),
                        pltpu.VMEM((2, c, heads * dk), F32),
                        pltpu.VMEM((2, heads, 2, pad_rows, dk), F32)],
        compiler_params=_params(("arbitrary",)),
        name="gla_scan",
    )(proj, proj, proj, b_f, proj, proj, proj, b_b)

    tm = _pick(nt, 512, ROW_CHUNK)
    gblk = (2 * heads * dk + heads * dv) // d
    a = pl.pallas_call(
        functools.partial(_gla_post_kernel, dv=dv),
        grid=(nt // tm,),
        in_specs=[pl.BlockSpec((tm, d), lambda i: (i, 0)),
                  pl.BlockSpec((tm, d), lambda i: (i, 0)),
                  pl.BlockSpec((tm, d), lambda i: (i, gblk)),
                  pl.BlockSpec((1, dv), lambda i: (0, 0))],
        out_specs=pl.BlockSpec((tm, d), lambda i: (i, 0)),
        out_shape=jax.ShapeDtypeStruct((nt, d), BF16),
        compiler_params=_params(("arbitrary",)),
        name="gla_post",
    )(of, ob, proj, onorm.reshape(1, dv))
    return _matmul_residual(a, w_out, t_arr, nt, s_lat, mod, 2 * d, "gla_out")


def _router_kernel(x_ref, g_ref, mod_ref, rw_ref, h_ref, aff_ref, *, s_lat, tm, n_exp, shift_col, scale_col):
    i = pl.program_id(0)
    gain, shift = _mod_vectors(g_ref[...], mod_ref[...], shift_col, scale_col)
    rw_hi, rw_lo = _split2(rw_ref[...])
    rw2 = (rw_hi.astype(F32) + pltpu.roll(rw_lo.astype(F32), n_exp, axis=1)).astype(BF16)

    def slab(r0):
        h = _norm_mod(x_ref[pl.ds(r0, ROW_CHUNK), :], gain, shift, i * tm + r0 >= s_lat)
        h_ref[pl.ds(r0, ROW_CHUNK), :] = h
        h_hi, h_lo = _split2(h)
        p = _dot(h_hi, rw2)
        p2 = _dot(h_lo, rw2)
        logits = p + pltpu.roll(p, LANES - n_exp, axis=1) + p2
        lt = jnp.transpose(logits)[0:n_exp, :]
        m = jnp.max(lt, axis=0, keepdims=True)
        e = jnp.exp(lt - m)
        aff_ref[r0 // ROW_CHUNK] = e / jnp.sum(e, axis=0, keepdims=True)

    _for_rows(tm, ROW_CHUNK, slab)


def _router(t_arr, n_rows, s_lat, g, mod, router_w):
    nt, d = t_arr.shape
    n_exp = router_w.shape[1]
    tm = _pick(n_rows, 1024, ROW_CHUNK)
    rw_pad = jnp.pad(router_w, ((0, 0), (0, LANES - n_exp)))
    hmod, aff = pl.pallas_call(
        functools.partial(_router_kernel, s_lat=s_lat, tm=tm, n_exp=n_exp,
                          shift_col=3 * d, scale_col=4 * d),
        grid=(n_rows // tm,),
        in_specs=[pl.BlockSpec((tm, d), lambda i: (i, 0)),
                  pl.BlockSpec((1, d), lambda i: (0, 0)),
                  pl.BlockSpec(mod.shape, lambda i: (0, 0)),
                  pl.BlockSpec((d, LANES), lambda i: (0, 0))],
        out_specs=[pl.BlockSpec((tm, d), lambda i: (i, 0)),
                   pl.BlockSpec((tm // ROW_CHUNK, n_exp, LANES), lambda i: (i, 0, 0))],
        out_shape=[jax.ShapeDtypeStruct((n_rows, d), F32),
                   jax.ShapeDtypeStruct((n_rows // ROW_CHUNK, n_exp, LANES), F32)],
        compiler_params=_params(("arbitrary",)),
        name="moe_router",
    )(t_arr, g.reshape(1, d), mod, rw_pad)
    return hmod, aff


def _select_kernel(aff_ref, idx_ref, gate_ref, *, cap):
    n_exp, nb, _ = aff_ref.shape
    aff = aff_ref[...]
    prefix = jnp.zeros((n_exp, 1, 1), I32)
    for bit in range(30, -1, -1):
        cand = prefix | (1 << bit)
        ge = aff >= lax.bitcast_convert_type(cand, F32)
        cnt = jnp.sum(jnp.sum(ge.astype(F32), axis=2, keepdims=True), axis=1, keepdims=True)
        prefix = jnp.where(cnt >= cap, cand, prefix)
    kth = lax.bitcast_convert_type(prefix, F32)
    gt = (aff > kth).astype(F32)
    eq = (aff == kth).astype(F32)
    n_gt = jnp.sum(jnp.sum(gt, axis=2, keepdims=True), axis=1, keepdims=True)
    need = cap - n_gt

    li = lax.broadcasted_iota(I32, (LANES, LANES), 0)
    lj = lax.broadcasted_iota(I32, (LANES, LANES), 1)
    upper = (li <= lj).astype(BF16)
    bi = lax.broadcasted_iota(I32, (nb, nb), 0)
    bj = lax.broadcasted_iota(I32, (nb, nb), 1)
    lower_strict = (bj < bi).astype(BF16)
    upper_nb = (bi <= bj).astype(BF16)
    ones_rows = jnp.ones((BF16_ROWS, LANES), BF16)
    pcol = lax.broadcasted_iota(I32, (cap, 1), 0).astype(F32)
    brow = lax.broadcasted_iota(I32, (1, nb), 1).astype(F32)
    lane_row = lax.broadcasted_iota(I32, (1, LANES), 1).astype(F32)

    for e in range(n_exp):
        eq_e = eq[e].astype(BF16)
        rank = _dot(eq_e, upper) + jnp.sum(_dot(lower_strict, eq_e), axis=1, keepdims=True)
        mask = jnp.maximum(gt[e], eq[e] * (rank <= need[e]).astype(F32))
        m_bf = mask.astype(BF16)
        lcs = _dot(m_bf, upper)
        tot = _dot_nt(ones_rows, m_bf)
        cb_row = _dot(tot.astype(BF16), upper_nb)[0:1, :]
        le = cb_row <= pcol
        blk = jnp.sum(le.astype(F32), axis=1, keepdims=True)
        lt = pcol - jnp.max(jnp.where(le, cb_row, 0.0), axis=1, keepdims=True)
        onehot = (brow == blk).astype(BF16)
        rowcs = _dot(onehot, lcs.astype(BF16))
        j = jnp.sum((rowcs <= lt).astype(F32), axis=1, keepdims=True)
        idx_ref[e] = (blk * LANES + j).astype(I32)
        a1, a2, a3 = _split3(aff[e])
        arow = _dot(onehot, a1) + _dot(onehot, a2) + _dot(onehot, a3)
        gate_ref[e] = jnp.sum(jnp.where(lane_row == j, arow, 0.0), axis=1, keepdims=True)


def _select(aff, cap):
    n_exp = aff.shape[0]
    return pl.pallas_call(
        functools.partial(_select_kernel, cap=cap),
        grid=(1,),
        in_specs=[pl.BlockSpec(aff.shape, lambda i: (0, 0, 0))],
        out_specs=[pl.BlockSpec((n_exp, cap, 1), lambda i: (0, 0, 0)),
                   pl.BlockSpec((n_exp, cap, 1), lambda i: (0, 0, 0))],
        out_shape=[jax.ShapeDtypeStruct((n_exp, cap, 1), I32),
                   jax.ShapeDtypeStruct((n_exp, cap, 1), F32)],
        compiler_params=_params(("arbitrary",)),
        name="moe_select",
    )(aff)


def _row_copy(hbm, buf, idx_ref, p, gather, sem):
    row = idx_ref[0, 0, p]
    src, dst = hbm.at[pl.ds(row, 1), :], buf.at[pl.ds(p, 1), :]
    if not gather:
        src, dst = dst, src
    return pltpu.make_async_copy(src, dst, sem)


def _start_row_copies(hbm, buf, idx_ref, sem, n_rows, gather):
    def body(p, carry):
        _row_copy(hbm, buf, idx_ref, p, gather, sem).start()
        return carry

    lax.fori_loop(0, n_rows, body, 0, unroll=8)


def _wait_row_copies(buf, sem):
    pltpu.make_async_copy(buf, buf, sem).wait()


def _ffn_kernel(idx_prev_ref, idx_ref, idx_next_ref, gate_ref, h_hbm, x_hbm, wg_ref, wu_ref, wd_ref,
                mod_ref, o_hbm, xg_ref, xb_ref, y_ref, acc_ref, wgb_ref, wub_ref, wdb_ref, sem,
                *, k_lat, kt, nf, mc, n_exp, n_s, g0, n_g, n_x):
    del x_hbm
    e = pl.program_id(0)
    f = pl.program_id(1)
    x_sem, acc_sem, out_sem = sem.at[0], sem.at[1], sem.at[2]

    @pl.when(f == 0)
    def _():
        @pl.when(e == 0)
        def _():
            _start_row_copies(h_hbm, xg_ref, idx_ref, x_sem, kt, True)
            _start_row_copies(o_hbm, acc_ref, idx_ref, acc_sem, kt, True)
            _wait_row_copies(acc_ref, acc_sem)

        _wait_row_copies(xg_ref, x_sem)

        def cast(r0):
            xb_ref[pl.ds(r0, mc), :] = xg_ref[pl.ds(r0, mc), :].astype(BF16)

        _for_rows(kt, mc, cast)

    wgb_ref[...] = wg_ref[0, 0].astype(BF16)
    wub_ref[...] = wu_ref[0, 0].astype(BF16)
    wdb_ref[...] = wd_ref[0, 0].astype(BF16)

    chunks = kt // mc
    for fv in range(nf):
        jobs = []
        if fv < n_x:
            per = kt // (n_x * chunks)
            jobs.append((fv * chunks * per, per,
                         lambda p: _row_copy(h_hbm, xg_ref, idx_next_ref, p, True, x_sem).start()))
        if fv < n_s:
            per = kt // (n_s * chunks)
            jobs.append((fv * chunks * per, per,
                         lambda p: _row_copy(o_hbm, acc_ref, idx_prev_ref, p, False, out_sem).start()))
        if g0 <= fv < g0 + n_g:
            per = kt // (n_g * chunks)
            jobs.append(((fv - g0) * chunks * per, per,
                         lambda p: _row_copy(o_hbm, acc_ref, idx_ref, p, True, acc_sem).start()))

        @pl.when(f == fv)
        def _(fv=fv, jobs=jobs):
            if fv == g0:
                _wait_row_copies(acc_ref, out_sem)

            def ffn(r0):
                xb = xb_ref[pl.ds(r0, mc), :]
                a = _dot(xb, wgb_ref[...])
                u = _dot(xb, wub_ref[...])
                for first, per, issue in jobs:
                    base = first + (r0 // mc) * per
                    for k in range(per):
                        issue(base + k)
                part = _dot((_silu(a) * u).astype(BF16), wdb_ref[...])
                if fv == 0:
                    y_ref[pl.ds(r0, mc), :] = part
                else:
                    y_ref[pl.ds(r0, mc), :] += part

            _for_rows(kt, mc, ffn)

    @pl.when(f == nf - 1)
    def _():
        _wait_row_copies(acc_ref, acc_sem)
        mod = mod_ref[...]

        def rmw(r0):
            rows = r0 + lax.broadcasted_iota(I32, (mc, 1), 0)
            g2 = jnp.where(rows >= k_lat, mod[1:2, :], mod[0:1, :])
            upd = g2 * (y_ref[pl.ds(r0, mc), :] * gate_ref[0, pl.ds(r0, mc), :])
            acc_ref[pl.ds(r0, mc), :] = acc_ref[pl.ds(r0, mc), :] + upd

        _for_rows(kt, mc, rmw)

        @pl.when(e == n_exp - 1)
        def _():
            _start_row_copies(o_hbm, acc_ref, idx_ref, out_sem, kt, False)
            _wait_row_copies(acc_ref, out_sem)
            _wait_row_copies(xg_ref, x_sem)


def _moe_ffn(t_arr, hmod, idx, gate, mod, w_gate, w_up, w_down, layer, k_lat):
    nt, d = t_arr.shape
    _, n_exp, _, ff = w_gate.shape
    kt = idx.shape[1]
    tf = _pick(ff, MXU_DIM, LANES)
    nf = ff // tf
    mc = _pick(kt, 384, BF16_ROWS)
    assert nf >= 3
    n_s = max(1, nf // 3)
    g0 = min(n_s + 1, nf - 2)
    n_g = max(1, min(n_s, nf - 1 - g0))
    n_x = n_s + n_g
    chunks = kt // mc
    assert kt % (n_s * chunks) == 0 and kt % (n_g * chunks) == 0 and kt % (n_x * chunks) == 0
    idx3 = idx.reshape(n_exp, 1, kt)
    return pl.pallas_call(
        functools.partial(_ffn_kernel, k_lat=k_lat, kt=kt, nf=nf, mc=mc, n_exp=n_exp,
                          n_s=n_s, g0=g0, n_g=n_g, n_x=n_x),
        grid=(n_exp, nf),
        in_specs=[pl.BlockSpec((1, 1, kt), lambda e, f: (jnp.maximum(e - 1, 0), 0, 0),
                               memory_space=pltpu.SMEM),
                  pl.BlockSpec((1, 1, kt), lambda e, f: (e, 0, 0), memory_space=pltpu.SMEM),
                  pl.BlockSpec((1, 1, kt), lambda e, f: (jnp.minimum(e + 1, n_exp - 1), 0, 0),
                               memory_space=pltpu.SMEM),
                  pl.BlockSpec((1, kt, 1), lambda e, f: (e, 0, 0)),
                  pl.BlockSpec(memory_space=pl.ANY),
                  pl.BlockSpec(memory_space=pl.ANY),
                  pl.BlockSpec((1, 1, d, tf), lambda e, f: (layer, e, 0, f)),
                  pl.BlockSpec((1, 1, d, tf), lambda e, f: (layer, e, 0, f)),
                  pl.BlockSpec((1, 1, tf, d), lambda e, f: (layer, e, f, 0)),
                  pl.BlockSpec((SUBLANES, d), lambda e, f: (0, 5))],
        out_specs=pl.BlockSpec(memory_space=pl.ANY),
        out_shape=jax.ShapeDtypeStruct((nt, d), F32),
        input_output_aliases={5: 0},
        scratch_shapes=[pltpu.VMEM((kt, d), F32), pltpu.VMEM((kt, d), BF16), pltpu.VMEM((kt, d), F32),
                        pltpu.VMEM((kt, d), F32),
                        pltpu.VMEM((d, tf), BF16), pltpu.VMEM((d, tf), BF16), pltpu.VMEM((tf, d), BF16),
                        pltpu.SemaphoreType.DMA((3,))],
        compiler_params=_params(("arbitrary", "arbitrary")),
        name="moe_ffn",
    )(idx3, idx3, idx3, gate, hmod, t_arr, w_gate, w_up, w_down, mod)


def _pad_blocks(aff):
    nb = aff.shape[1]
    pad = (-nb) % BF16_ROWS
    if pad:
        aff = jnp.concatenate([aff, jnp.full((aff.shape[0], pad, LANES), -1.0, F32)], axis=1)
    return aff


def _moe_layer(t_arr, n_rows, s_lat, mod, g, router_w, w_gate, w_up, w_down, layer):
    n_exp = router_w.shape[1]
    hmod, aff = _router(t_arr, n_rows, s_lat, g, mod, router_w)
    aff = jnp.transpose(aff, (1, 0, 2))
    nb_lat = s_lat // LANES
    cap_lat = max(1, (CAPACITY_FACTOR * s_lat) // n_exp)
    idx, gate = _select(_pad_blocks(aff[:, :nb_lat]), cap_lat)
    idx = idx.reshape(n_exp, cap_lat)
    if n_rows > s_lat:
        cx = n_rows - s_lat
        cap_ctx = max(1, (CAPACITY_FACTOR * cx) // n_exp)
        idx_c, gate_c = _select(_pad_blocks(aff[:, nb_lat:]), cap_ctx)
        idx = jnp.concatenate([idx, idx_c.reshape(n_exp, cap_ctx) + s_lat], axis=1)
        gate = jnp.concatenate([gate, gate_c], axis=1)
    return _moe_ffn(t_arr, hmod, idx, gate, mod, w_gate, w_up, w_down, layer, cap_lat)


def _final_kernel(x_ref, g_ref, o_ref):
    x = x_ref[...]
    ms = jnp.mean(x * x, axis=-1, keepdims=True)
    o_ref[...] = x * lax.rsqrt(ms + RMS_EPS) * g_ref[...]


def _final_norm(t_arr, s_lat, g):
    d = t_arr.shape[1]
    tm = _pick(s_lat, 512, ROW_CHUNK)
    return pl.pallas_call(
        _final_kernel,
        grid=(s_lat // tm,),
        in_specs=[pl.BlockSpec((tm, d), lambda i: (i, 0)), pl.BlockSpec((1, d), lambda i: (0, 0))],
        out_specs=pl.BlockSpec((tm, d), lambda i: (i, 0)),
        out_shape=jax.ShapeDtypeStruct((s_lat, d), F32),
        compiler_params=_params(("arbitrary",)),
        name="final_norm",
    )(t_arr, g.reshape(1, d))


def kernel(x, c, ctx, c_ctx, mod_w, mod_b, norm_mix, norm_ffn, conv_w_in, conv_w_dw, conv_w_out,
           diff_w_qkv, diff_lambda, diff_subln, diff_w_out, gla_w_in, gla_gate_w1, gla_gate_w2,
           gla_gate_b, gla_onorm, gla_w_out, router_w, exp_w_gate, exp_w_up, exp_w_down, final_norm):
    batch, s_lat, d = x.shape
    cx = ctx.shape[1]
    depth = mod_w.shape[0]
    assert batch == 1 and s_lat % GRID_W == 0 and s_lat % ROW_CHUNK == 0
    nt = s_lat + cx
    t_arr = jnp.concatenate([x[0], ctx[0]], axis=0)
    cvec8 = jnp.concatenate([c, c_ctx[None, :], jnp.zeros((SUBLANES - 2, d), F32)], axis=0)
    mods = _modulation(cvec8, mod_w, mod_b)
    cos_t, sin_t = _rope_tables(nt, s_lat)

    for i in range(depth):
        kind, j = i % N_MIXERS, i // N_MIXERS
        ctx_next = i < depth - 1
        ctx_read = ctx_next or kind != 0
        n_rows = nt if ctx_read else s_lat
        if t_arr.shape[0] != n_rows:
            t_arr = t_arr[:n_rows]
        mod = mods[i]
        if kind == 0:
            t_arr = _short_conv_layer(t_arr, n_rows, s_lat, mod, norm_mix[i],
                                      conv_w_in[j], conv_w_dw[j], conv_w_out[j])
        elif kind == 1:
            lambda_init = DIFF_LAMBDA_A - DIFF_LAMBDA_B * math.exp(-DIFF_LAMBDA_C * i)
            t_arr = _diff_layer(t_arr, s_lat, mod, norm_mix[i], diff_w_qkv[j], diff_lambda[j],
                                diff_subln[j], diff_w_out[j], lambda_init, cos_t, sin_t)
        else:
            t_arr = _gla_layer(t_arr, s_lat, mod, norm_mix[i], gla_w_in[j], gla_gate_w1[j],
                               gla_gate_w2[j], gla_gate_b[j], gla_onorm[j], gla_w_out[j])
        n_moe = nt if ctx_next else s_lat
        if t_arr.shape[0] != n_moe:
            t_arr = t_arr[:n_moe]
        t_arr = _moe_layer(t_arr, n_moe, s_lat, mod, norm_ffn[i], router_w[i],
                           exp_w_gate, exp_w_up, exp_w_down, i)
    return _final_norm(t_arr, s_lat, final_norm)[None]
```

```python
import functools
import math

import jax
import jax.numpy as jnp
from jax import lax
from jax.experimental import pallas as pl
from jax.experimental.pallas import tpu as pltpu

F32 = jnp.float32
BF16 = jnp.bfloat16
I32 = jnp.int32

GRID_W = 64
N_MIXERS = 3
RMS_EPS = 1e-6
DIFF_HEAD_DIM = 128
DIFF_SUBLN_EPS = 1e-5
DIFF_LAMBDA_A = 0.8
DIFF_LAMBDA_B = 0.6
DIFF_LAMBDA_C = 0.3
ROPE_THETA = 10000.0
GLA_HEADS = 4
GLA_TAU = 16.0
GLA_CHUNK = 128
GLA_SUB = 16
GLA_FACTOR_BOUND = 60.0
CAPACITY_FACTOR = 2

LANES = 128
SUBLANES = 8
BF16_ROWS = 16
MXU_DIM = 256
VMEM_LIMIT_BYTES = 56 * 1024 * 1024
ROW_CHUNK = 128
MATMUL_ROWS = 768
ATTN_QUERIES = 256
ATTN_KEYS = 512


def _params(sem, vmem=VMEM_LIMIT_BYTES):
    return pltpu.CompilerParams(dimension_semantics=sem, vmem_limit_bytes=vmem)


def _pick(n, cap, mult):
    best = None
    for d in range(mult, min(n, cap) + 1, mult):
        if n % d == 0:
            best = d
    assert best is not None, (n, cap, mult)
    return best


def _dot(a, b):
    return jnp.dot(a, b, preferred_element_type=F32)


def _dot_nt(a, b):
    return lax.dot_general(a, b, (((1,), (1,)), ((), ())), preferred_element_type=F32)


def _dot_tn(a, b):
    return lax.dot_general(a, b, (((0,), (0,)), ((), ())), preferred_element_type=F32)


def _split2(x):
    hi = x.astype(BF16)
    lo = (x - hi.astype(F32)).astype(BF16)
    return hi, lo


def _split3(x):
    a = x.astype(BF16)
    r = x - a.astype(F32)
    b = r.astype(BF16)
    c = (r - b.astype(F32)).astype(BF16)
    return a, b, c


def _sigmoid(x):
    return 1.0 / (1.0 + jnp.exp(-x))


def _silu(x):
    return x * _sigmoid(x)


def _for_rows(n_rows, chunk, fn):
    def body(r, carry):
        fn(pl.multiple_of(r * chunk, chunk))
        return carry

    lax.fori_loop(0, n_rows // chunk, body, 0)


def _mod_vectors(g, mod, shift_col, scale_col):
    d = g.shape[-1]
    return g * (1.0 + mod[0:2, scale_col:scale_col + d]), mod[0:2, shift_col:shift_col + d]


def _norm_mod(x, gain, shift, is_ctx):
    ms = jnp.mean(x * x, axis=-1, keepdims=True)
    gain = jnp.where(is_ctx, gain[1:2], gain[0:1])
    shift = jnp.where(is_ctx, shift[1:2], shift[0:1])
    return x * lax.rsqrt(ms + RMS_EPS) * gain + shift


def _mod_kernel(c_ref, w_ref, b_ref, o_ref):
    c = c_ref[...]
    s_hi, s_lo = _split2(_silu(c))
    w_hi, w_lo = _split2(w_ref[0])
    acc = _dot(s_hi, w_hi) + _dot(s_lo, w_hi) + _dot(s_hi, w_lo)
    o_ref[0] = acc + b_ref[0]


def _modulation(cvec8, mod_w, mod_b):
    depth, d, n6 = mod_w.shape
    tn = _pick(n6, 1024, LANES)
    return pl.pallas_call(
        _mod_kernel,
        grid=(depth, n6 // tn),
        in_specs=[
            pl.BlockSpec((SUBLANES, d), lambda l, j: (0, 0)),
            pl.BlockSpec((1, d, tn), lambda l, j: (l, 0, j)),
            pl.BlockSpec((1, 1, tn), lambda l, j: (l, 0, j)),
        ],
        out_specs=pl.BlockSpec((1, SUBLANES, tn), lambda l, j: (l, 0, j)),
        out_shape=jax.ShapeDtypeStruct((depth, SUBLANES, n6), F32),
        compiler_params=_params(("arbitrary", "arbitrary")),
        name="modulation",
    )(cvec8, mod_w, mod_b.reshape(depth, 1, n6))


def _rope_kernel(inv_ref, sgn_ref, cos_ref, sin_ref, *, s_lat, tm):
    i = pl.program_id(0)
    t = i * tm + lax.broadcasted_iota(I32, (tm, LANES), 0)
    lane = lax.broadcasted_iota(I32, (tm, LANES), 1)
    row = t // GRID_W
    col = t % GRID_W
    pos = jnp.where(lane < DIFF_HEAD_DIM // 2, row, col)
    pos = jnp.where(t < s_lat, pos, 0)
    ang = pos.astype(F32) * inv_ref[...]
    cos_ref[...] = jnp.cos(ang)
    sin_ref[...] = jnp.sin(ang) * sgn_ref[...]


def _rope_tables(nt, s_lat):
    quarter = DIFF_HEAD_DIM // 4
    inv = ROPE_THETA ** (-jnp.arange(quarter, dtype=F32) / quarter)
    inv128 = jnp.tile(inv, 4).reshape(1, LANES)
    sgn = jnp.tile(jnp.concatenate([-jnp.ones((quarter,), F32), jnp.ones((quarter,), F32)]), 2)
    tm = _pick(nt, 1024, LANES)
    return pl.pallas_call(
        functools.partial(_rope_kernel, s_lat=s_lat, tm=tm),
        grid=(nt // tm,),
        in_specs=[pl.BlockSpec((1, LANES), lambda i: (0, 0)),
                  pl.BlockSpec((1, LANES), lambda i: (0, 0))],
        out_specs=[pl.BlockSpec((tm, LANES), lambda i: (i, 0)),
                   pl.BlockSpec((tm, LANES), lambda i: (i, 0))],
        out_shape=[jax.ShapeDtypeStruct((nt, LANES), F32)] * 2,
        compiler_params=_params(("arbitrary",)),
        name="rope_tables",
    )(inv128, sgn.reshape(1, LANES))


def _nmm_kernel(*refs, n_w, n_extra, n_out, s_lat, tm, mc, shift_col, scale_col, epilogue):
    x_ref, g_ref, mod_ref = refs[:3]
    w_refs = refs[3:3 + n_w]
    extra = refs[3 + n_w:3 + n_w + n_extra]
    outs = refs[3 + n_w + n_extra:3 + n_w + n_extra + n_out]
    h_ref = refs[-1]
    i = pl.program_id(0)
    j = pl.program_id(1)

    @pl.when(j == 0)
    def _():
        gain, shift = _mod_vectors(g_ref[...], mod_ref[...], shift_col, scale_col)

        def slab(r0):
            h = _norm_mod(x_ref[pl.ds(r0, ROW_CHUNK), :], gain, shift, i * tm + r0 >= s_lat)
            h_ref[pl.ds(r0, ROW_CHUNK), :] = h.astype(BF16)

        _for_rows(tm, ROW_CHUNK, slab)

    for applies, ep in epilogue:
        @pl.when(applies(j))
        def _(ep=ep):
            for r in range(tm // mc):
                rows = pl.ds(r * mc, mc)
                h = h_ref[rows, :]
                ep([_dot(h, w[...]) for w in w_refs], extra, outs, j, rows, h)


def _always(j):
    return j >= 0


def _norm_mod_matmul(t_arr, n_rows, s_lat, g, mod, shift_col, scale_col, w, w_col_blocks, tn,
                     n_tiles, epilogue, extra, extra_specs, out_shapes, out_specs, name):
    d = t_arr.shape[1]
    assert w.dtype == BF16
    tm = _pick(n_rows, 1408, ROW_CHUNK)
    mc = _pick(tm, MATMUL_ROWS, BF16_ROWS)
    w_specs = [pl.BlockSpec((d, tn), functools.partial(lambda i, j, o: (0, o + j), o=o))
               for o in w_col_blocks]
    kern = functools.partial(
        _nmm_kernel, n_w=len(w_col_blocks), n_extra=len(extra), n_out=len(out_shapes),
        s_lat=s_lat, tm=tm, mc=mc, shift_col=shift_col, scale_col=scale_col, epilogue=epilogue)
    return pl.pallas_call(
        kern,
        grid=(n_rows // tm, n_tiles),
        in_specs=[pl.BlockSpec((tm, d), lambda i, j: (i, 0)),
                  pl.BlockSpec((1, d), lambda i, j: (0, 0)),
                  pl.BlockSpec(mod.shape, lambda i, j: (0, 0))]
                 + w_specs + [s(tm) for s in extra_specs],
        out_specs=[s(tm) for s in out_specs],
        out_shape=out_shapes,
        scratch_shapes=[pltpu.VMEM((tm, d), BF16)],
        compiler_params=_params(("arbitrary", "arbitrary")),
        name=name,
    )(t_arr, g.reshape(1, d), mod, *([w] * len(w_col_blocks)), *extra)


def _gated_residual_rows(a_ref, w_ref, x_ref, mod_ref, o_ref, i, s_lat, tm, mc):
    mod = mod_ref[...]
    for r in range(tm // mc):
        rows = pl.ds(r * mc, mc)
        acc = _dot(a_ref[rows, :], w_ref[...])
        tok = i * tm + r * mc + lax.broadcasted_iota(I32, (mc, 1), 0)
        gate = jnp.where(tok >= s_lat, mod[1:2, :], mod[0:1, :])
        o_ref[rows, :] = x_ref[rows, :] + gate * acc


def _mmres_kernel(a_ref, w_ref, x_ref, mod_ref, o_ref, *, s_lat, tm, mc):
    _gated_residual_rows(a_ref, w_ref, x_ref, mod_ref, o_ref, pl.program_id(0), s_lat, tm, mc)


def _matmul_residual(a, w, t_arr, n_rows, s_lat, mod, gate_col, name):
    k = a.shape[1]
    d = t_arr.shape[1]
    assert w.dtype == BF16
    tm = _pick(n_rows, 1408, ROW_CHUNK)
    mc = _pick(tm, MATMUL_ROWS, BF16_ROWS)
    tn = _pick(d, 512, LANES)
    gblk = gate_col // tn
    return pl.pallas_call(
        functools.partial(_mmres_kernel, s_lat=s_lat, tm=tm, mc=mc),
        grid=(n_rows // tm, d // tn),
        in_specs=[pl.BlockSpec((tm, k), lambda i, j: (i, 0)),
                  pl.BlockSpec((k, tn), lambda i, j: (0, j)),
                  pl.BlockSpec((tm, tn), lambda i, j: (i, j)),
                  pl.BlockSpec((SUBLANES, tn), lambda i, j: (0, gblk + j))],
        out_specs=pl.BlockSpec((tm, tn), lambda i, j: (i, j)),
        out_shape=jax.ShapeDtypeStruct(t_arr.shape, F32),
        input_output_aliases={2: 0},
        compiler_params=_params(("arbitrary", "arbitrary")),
        name=name,
    )(a, w, t_arr, mod)


def _conv_in_epilogue(accs, extra, outs, j, rows, h):
    b, c, u = accs
    outs[0][rows, :] = b.astype(BF16)
    outs[1][rows, :] = c * u


def _conv_out_kernel(b_ref, v_ref, vp_ref, vn_ref, dw_ref, w_ref, x_ref, mod_ref, o_ref, a_ref, buf_ref,
                     *, s_lat, n_rows, tm, mc):
    i = pl.program_id(0)
    j = pl.program_id(1)

    @pl.when(j == 0)
    def _():
        buf_ref[0:SUBLANES, :] = vp_ref[...]
        buf_ref[SUBLANES + tm:2 * SUBLANES + tm, :] = vn_ref[...]

        def copy(r0):
            buf_ref[pl.ds(SUBLANES + r0, ROW_CHUNK), :] = v_ref[pl.ds(r0, ROW_CHUNK), :]

        _for_rows(tm, ROW_CHUNK, copy)
        dw = dw_ref[...]

        def slab(r0):
            win = buf_ref[pl.ds(r0, ROW_CHUNK + 2 * SUBLANES), :]
            prev = win[SUBLANES - 1:SUBLANES - 1 + ROW_CHUNK]
            cur = win[SUBLANES:SUBLANES + ROW_CHUNK]
            nxt = win[SUBLANES + 1:SUBLANES + 1 + ROW_CHUNK]
            rows = i * tm + r0 + lax.broadcasted_iota(I32, (ROW_CHUNK, 1), 0)
            has_prev = (rows != 0) & (rows != s_lat)
            has_next = (rows != s_lat - 1) & (rows != n_rows - 1)
            conv = (jnp.where(has_prev, prev, 0.0) * dw[0:1] + cur * dw[1:2]
                    + jnp.where(has_next, nxt, 0.0) * dw[2:3])
            a = b_ref[pl.ds(r0, ROW_CHUNK), :].astype(F32) * conv
            a_ref[pl.ds(r0, ROW_CHUNK), :] = a.astype(BF16)

        _for_rows(tm, ROW_CHUNK, slab)

    _gated_residual_rows(a_ref, w_ref, x_ref, mod_ref, o_ref, i, s_lat, tm, mc)


def _short_conv_layer(t_arr, n_rows, s_lat, mod, g, w_in, w_dw, w_out):
    nt, d = t_arr.shape
    w_in, w_out = w_in.astype(BF16), w_out.astype(BF16)
    tn = _pick(d, 512, LANES)
    nblk = d // tn
    b, v = _norm_mod_matmul(
        t_arr, n_rows, s_lat, g, mod, 0, d, w_in, [0, nblk, 2 * nblk], tn, nblk,
        [(_always, _conv_in_epilogue)], [], [],
        [jax.ShapeDtypeStruct((n_rows, d), BF16), jax.ShapeDtypeStruct((n_rows, d), F32)],
        [lambda tm: pl.BlockSpec((tm, tn), lambda i, j: (i, j))] * 2, "conv_in")
    tm = _pick(n_rows, 1024, ROW_CHUNK)
    mc = _pick(tm, MATMUL_ROWS, BF16_ROWS)
    hb = tm // SUBLANES
    last = n_rows // SUBLANES - 1
    gblk = (2 * d) // tn
    return pl.pallas_call(
        functools.partial(_conv_out_kernel, s_lat=s_lat, n_rows=n_rows, tm=tm, mc=mc),
        grid=(n_rows // tm, nblk),
        in_specs=[pl.BlockSpec((tm, d), lambda i, j: (i, 0)),
                  pl.BlockSpec((tm, d), lambda i, j: (i, 0)),
                  pl.BlockSpec((SUBLANES, d), lambda i, j: (jnp.maximum(i * hb - 1, 0), 0)),
                  pl.BlockSpec((SUBLANES, d), lambda i, j: (jnp.minimum((i + 1) * hb, last), 0)),
                  pl.BlockSpec(w_dw.shape, lambda i, j: (0, 0)),
                  pl.BlockSpec((d, tn), lambda i, j: (0, j)),
                  pl.BlockSpec((tm, tn), lambda i, j: (i, j)),
                  pl.BlockSpec((SUBLANES, tn), lambda i, j: (0, gblk + j))],
        out_specs=pl.BlockSpec((tm, tn), lambda i, j: (i, j)),
        out_shape=jax.ShapeDtypeStruct((nt, d), F32),
        input_output_aliases={6: 0},
        scratch_shapes=[pltpu.VMEM((tm, d), BF16),
                        pltpu.VMEM((tm + 2 * SUBLANES, d), F32)],
        compiler_params=_params(("arbitrary", "arbitrary")),
        name="conv_out",
    )(b, v, v, v, w_dw, w_out, t_arr, mod)


def _rope_rot(a):
    half = DIFF_HEAD_DIM // 4
    lane = lax.broadcasted_iota(I32, a.shape, 1)
    up = pltpu.roll(a, LANES - half, axis=1)
    dn = pltpu.roll(a, half, axis=1)
    return jnp.where(lane % (2 * half) < half, up, dn)


def _rope_epilogue(accs, extra, outs, j, rows, h, *, n_q, tn):
    acc = accs[0]
    cos_ref, sin_ref = extra
    scale = jnp.where(j < n_q, DIFF_HEAD_DIM ** -0.5 * math.log2(math.e), 1.0).astype(F32)
    cos = cos_ref[rows, :] * scale
    sin = sin_ref[rows, :] * scale
    for gidx in range(tn // LANES):
        a = acc[:, gidx * LANES:(gidx + 1) * LANES]
        outs[0][rows, gidx * LANES:(gidx + 1) * LANES] = (a * cos + _rope_rot(a) * sin).astype(BF16)


def _cast_epilogue(accs, extra, outs, j, rows, h):
    outs[0][rows, :] = accs[0].astype(outs[0].dtype)


def _attn_body(q_ref, k_ref, v_ref, lam_ref, sub_ref, o_ref, k_lo, k_hi, lambda_init):
    hd = DIFF_HEAD_DIM
    tq = q_ref.shape[0]
    chunks = [slice(lo, min(lo + ATTN_KEYS, k_hi)) for lo in range(k_lo, k_hi, ATTN_KEYS)]
    m = jnp.full((2 * tq, 1), -jnp.inf, F32)
    l = jnp.zeros((2 * tq, 1), F32)
    acc = jnp.zeros((2 * tq, 2 * hd), F32)
    for sl in chunks:
        s = jnp.concatenate([_dot_nt(q_ref[:, t * hd:(t + 1) * hd], k_ref[sl, t * hd:(t + 1) * hd])
                             for t in range(2)], axis=0)
        m_new = jnp.maximum(m, jnp.max(s, axis=1, keepdims=True))
        alpha = jnp.exp2(m - m_new)
        p = jnp.exp2(s - m_new)
        l = alpha * l + jnp.sum(p, axis=1, keepdims=True)
        acc = alpha * acc + _dot(p.astype(BF16), v_ref[sl, :])
        m = m_new
    l0, l1, acc0, acc1 = l[:tq], l[tq:], acc[:tq], acc[tq:]
    lp = lam_ref[...]
    lam = (jnp.exp(jnp.sum(lp[0:1] * lp[1:2], axis=1, keepdims=True))
           - jnp.exp(jnp.sum(lp[2:3] * lp[3:4], axis=1, keepdims=True)) + lambda_init)
    o = acc0 * (1.0 / l0) - acc1 * (lam / l1)
    ms = jnp.mean(o * o, axis=1, keepdims=True)
    o = o * lax.rsqrt(ms + DIFF_SUBLN_EPS) * sub_ref[...] * (1.0 - lambda_init)
    o_ref[...] = o.astype(BF16)


def _attn_kernel(q_ref, k_ref, v_ref, lam_ref, sub_ref, o_ref, *, s_lat, lambda_init):
    tq = q_ref.shape[0]
    nt = k_ref.shape[0]
    n_lat_blocks = s_lat // tq
    i = pl.program_id(1)

    @pl.when(i < n_lat_blocks)
    def _():
        _attn_body(q_ref, k_ref, v_ref, lam_ref, sub_ref, o_ref, 0, nt, lambda_init)

    @pl.when(i >= n_lat_blocks)
    def _():
        _attn_body(q_ref, k_ref, v_ref, lam_ref, sub_ref, o_ref, s_lat, nt, lambda_init)


def _diff_attention(qkv, s_lat, lam_p, subln, lambda_init, d):
    nt = qkv.shape[0]
    hw = 2 * DIFF_HEAD_DIM
    heads = d // hw
    tq = ATTN_QUERIES
    assert s_lat % tq == 0 and (nt - s_lat) % tq == 0
    return pl.pallas_call(
        functools.partial(_attn_kernel, s_lat=s_lat, lambda_init=lambda_init),
        grid=(heads, nt // tq),
        in_specs=[pl.BlockSpec((tq, hw), lambda h, i: (i, h)),
                  pl.BlockSpec((nt, hw), lambda h, i: (0, heads + h)),
                  pl.BlockSpec((nt, hw), lambda h, i: (0, 2 * heads + h)),
                  pl.BlockSpec(lam_p.shape, lambda h, i: (0, 0)),
                  pl.BlockSpec((1, hw), lambda h, i: (0, 0))],
        out_specs=pl.BlockSpec((tq, hw), lambda h, i: (i, h)),
        out_shape=jax.ShapeDtypeStruct((nt, d), BF16),
        compiler_params=_params(("arbitrary", "arbitrary")),
        name="diff_attn",
    )(qkv, qkv, qkv, lam_p, subln.reshape(1, hw))


def _diff_layer(t_arr, s_lat, mod, g, w_qkv, lam_p, subln, w_out, lambda_init, cos_t, sin_t):
    nt, d = t_arr.shape
    w_qkv, w_out = w_qkv.astype(BF16), w_out.astype(BF16)
    tn = _pick(d, 512, LANES)
    nblk = d // tn
    ep = [(lambda j: j < 2 * nblk, functools.partial(_rope_epilogue, n_q=nblk, tn=tn)),
          (lambda j: j >= 2 * nblk, _cast_epilogue)]
    (qkv,) = _norm_mod_matmul(
        t_arr, nt, s_lat, g, mod, 0, d, w_qkv, [0], tn, 3 * nblk, ep,
        [cos_t, sin_t], [lambda tm: pl.BlockSpec((tm, LANES), lambda i, j: (i, 0))] * 2,
        [jax.ShapeDtypeStruct((nt, 3 * d), BF16)],
        [lambda tm: pl.BlockSpec((tm, tn), lambda i, j: (i, j))], "diff_qkv")
    o = _diff_attention(qkv, s_lat, lam_p, subln, lambda_init, d)
    return _matmul_residual(o, w_out, t_arr, nt, s_lat, mod, 2 * d, "diff_out")


def _gla_proj_epilogue(accs, extra, outs, j, rows, h, *, n_q, dk, with_rank):
    scale = jnp.where(j < n_q, dk ** -0.5, 1.0).astype(F32)
    outs[0][rows, :] = accs[0] * scale
    if with_rank:
        outs[1][rows, :] = _dot(h, extra[0][...])


def _log_sigmoid(z):
    return jnp.minimum(z, 0.0) - jnp.log(1.0 + jnp.exp(-jnp.abs(z)))


def _gla_time(c, rev):
    rowi = lax.broadcasted_iota(I32, (c, 1), 0)
    coli = lax.broadcasted_iota(I32, (1, c), 1)
    return rowi, coli, ((c - 1 - rowi) if rev else rowi), ((c - 1 - coli) if rev else coli)


def _gla_decay_kernel(z_ref, w2_ref, bias_ref, bf_ref, bb_ref, *, c):
    rows = z_ref.shape[0]
    grp = _pick(rows, MXU_DIM, c)
    z_hi, z_lo = _split2(z_ref[...])
    ri = lax.broadcasted_iota(I32, (grp, grp), 0)
    ci = lax.broadcasted_iota(I32, (grp, grp), 1)
    same_chunk = (ri // c) == (ci // c)
    for t, (o_ref, rev) in enumerate(((bf_ref, False), (bb_ref, True))):
        w_hi, w_lo = _split2(w2_ref[t])
        z = _dot(z_hi, w_hi) + _dot(z_lo, w_hi) + _dot(z_hi, w_lo) + bias_ref[t]
        g = _log_sigmoid(z) * (1.0 / GLA_TAU)
        tri = (same_chunk & ((ci >= ri) if rev else (ci <= ri))).astype(BF16)
        for m in range(rows // grp):
            g1, g2, g3 = _split3(g[m * grp:(m + 1) * grp])
            o_ref[m * grp:(m + 1) * grp, :] = _dot(tri, g1) + _dot(tri, g2) + _dot(tri, g3)


def _gla_block_first(b, rev):
    c = b.shape[0]
    sub = GLA_SUB
    parts = []
    for m in range(c // sub):
        r = m * sub + (sub - 1 if rev else 0)
        parts.append(jnp.broadcast_to(b[r:r + 1, :], (sub, b.shape[1])))
    return jnp.concatenate(parts, axis=0)


def _gla_scores_factored(q, k, b, b_first, rev):
    c = q.shape[0]
    sub = GLA_SUB
    _, _, tau, tau_col = _gla_time(c, rev)
    qt = q * jnp.exp(b - b_first)
    q_parts, k_parts = [], []
    for blk in range(c // sub):
        rb = (c - 1 - sub * blk) if rev else sub * blk
        b_at = b[rb:rb + 1, :]
        in_blk = (tau // sub) == blk
        upto = tau < sub * (blk + 1)
        q_parts.append(jnp.where(in_blk, qt, 0.0).astype(BF16))
        k_parts.append(jnp.where(upto, k * jnp.exp(jnp.where(upto, b_at - b, 0.0)), 0.0).astype(BF16))
    a = _dot_nt(jnp.concatenate(q_parts, axis=1), jnp.concatenate(k_parts, axis=1))
    return jnp.where(tau_col <= tau, a, 0.0)


def _gla_scores_exact(q, k, b, kpad_ref, bpad_ref, rev):
    c = q.shape[0]
    sub = GLA_SUB
    rowi, coli, tau, _ = _gla_time(c, rev)
    kpad_ref[sub:sub + c, :] = k
    bpad_ref[sub:sub + c, :] = b
    a_mat = jnp.zeros((c, c), F32)
    for dlt in range(sub):
        off = sub + dlt if rev else sub - dlt
        k_sh = kpad_ref[off:off + c, :]
        b_sh = bpad_ref[off:off + c, :]
        valid = (tau % sub) >= dlt
        e = jnp.exp(jnp.where(valid, b - b_sh, 0.0))
        dsum = jnp.sum(q * k_sh * e, axis=1, keepdims=True)
        partner = (rowi + dlt) if rev else (rowi - dlt)
        a_mat = a_mat + jnp.where((coli == partner) & valid, dsum, 0.0)
    q_parts, k_parts = [], []
    for blk in range(1, c // sub):
        ref_row = (c - sub * blk) if rev else sub * blk - 1
        b_at = b[ref_row:ref_row + 1, :]
        in_blk = (tau // sub) == blk
        earlier = tau < sub * blk
        qt = jnp.where(in_blk, q * jnp.exp(jnp.where(in_blk, b - b_at, 0.0)), 0.0)
        kt = jnp.where(earlier, k * jnp.exp(jnp.where(earlier, b_at - b, 0.0)), 0.0)
        q_parts.append(qt.astype(BF16))
        k_parts.append(kt.astype(BF16))
    return a_mat + _dot_nt(jnp.concatenate(q_parts, axis=1), jnp.concatenate(k_parts, axis=1))


def _gla_chunk_output(q, k, v, b, a_mat, s_ref, rev):
    c = q.shape[0]
    rowi = lax.broadcasted_iota(I32, (c, 1), 0)
    end_row = 0 if rev else c - 1
    b_end = b[end_row:end_row + 1, :]
    vb = v.astype(BF16)
    s_old = s_ref[...]
    o = _dot(a_mat.astype(BF16), vb) + _dot((q * jnp.exp(b)).astype(BF16), s_old.astype(BF16))
    upd = _dot_tn((k * jnp.exp(b_end - b)).astype(BF16), vb)
    d1, d2, d3 = _split3(jnp.where(rowi == end_row, b, 0.0))
    ones = jnp.ones((c, LANES), BF16)
    decay_col = _dot_tn(d1, ones) + _dot_tn(d2, ones) + _dot_tn(d3, ones)
    s_ref[...] = jnp.exp(decay_col[:, 0:1]) * s_old + upd
    return o


def _gla_kernel(qf_ref, kf_ref, vf_ref, bf_ref, qb_ref, kb_ref, vb_ref, bb_ref,
                of_ref, ob_ref, s_ref, b1_ref, pad_ref, *, heads):
    @pl.when(pl.program_id(0) == 0)
    def _():
        s_ref[...] = jnp.zeros_like(s_ref)
        pad_ref[...] = jnp.zeros_like(pad_ref)

    dk = qf_ref.shape[1] // heads
    dv = vf_ref.shape[1] // heads
    dirs = ((qf_ref, kf_ref, vf_ref, bf_ref, of_ref, False), (qb_ref, kb_ref, vb_ref, bb_ref, ob_ref, True))
    excess = jnp.zeros((1, 1), F32)
    for t, (_, _, _, b_ref, _, rev) in enumerate(dirs):
        b = b_ref[...]
        b_first = _gla_block_first(b, rev)
        b1_ref[t] = b_first
        excess = jnp.maximum(excess, jnp.max(jnp.max(b_first - b, axis=1, keepdims=True), axis=0, keepdims=True))
    mild = jnp.max(excess) <= GLA_FACTOR_BOUND

    def run(exact):
        for t, (q_ref, k_ref, v_ref, b_ref, o_ref, rev) in enumerate(dirs):
            for h in range(heads):
                ks = slice(h * dk, (h + 1) * dk)
                vs = slice(h * dv, (h + 1) * dv)
                q, k, v, b = q_ref[:, ks], k_ref[:, ks], v_ref[:, vs], b_ref[:, ks]
                if exact:
                    a_mat = _gla_scores_exact(q, k, b, pad_ref.at[t, h, 0], pad_ref.at[t, h, 1], rev)
                else:
                    a_mat = _gla_scores_factored(q, k, b, b1_ref[t, :, ks], rev)
                o_ref[:, vs] = _gla_chunk_output(q, k, v, b, a_mat, s_ref.at[t, h], rev)

    pl.when(mild)(lambda: run(False))
    pl.when(jnp.logical_not(mild))(lambda: run(True))


def _gla_post_kernel(of_ref, ob_ref, g_ref, on_ref, a_ref, *, dv):
    o = of_ref[...] + ob_ref[...]
    gate = _silu(g_ref[...])
    for h in range(o.shape[1] // dv):
        oh = o[:, h * dv:(h + 1) * dv]
        ms = jnp.mean(oh * oh, axis=1, keepdims=True)
        y = oh * lax.rsqrt(ms + RMS_EPS) * on_ref[...] * gate[:, h * dv:(h + 1) * dv]
        a_ref[:, h * dv:(h + 1) * dv] = y.astype(BF16)


def _gla_layer(t_arr, s_lat, mod, g, w_in, gate_w1, gate_w2, gate_b, onorm, w_out):
    nt, d = t_arr.shape
    w_in, w_out = w_in.astype(BF16), w_out.astype(BF16)
    cx = nt - s_lat
    heads = GLA_HEADS
    dk = d // (2 * heads)
    dv = d // heads
    rank = gate_w1.shape[2]
    n_proj = w_in.shape[1]
    tn = _pick(d, 512, LANES)
    nq_tiles = (heads * dk) // tn
    w1cat = jnp.concatenate([gate_w1[0], gate_w1[1]], axis=1)
    w1pad = jnp.pad(w1cat, ((0, 0), (0, LANES - 2 * rank))).astype(BF16)
    proj_ep = functools.partial(_gla_proj_epilogue, n_q=nq_tiles, dk=dk)
    proj, z1 = _norm_mod_matmul(
        t_arr, nt, s_lat, g, mod, 0, d, w_in, [0], tn, n_proj // tn,
        [(lambda j: j == 0, functools.partial(proj_ep, with_rank=True)),
         (lambda j: j > 0, functools.partial(proj_ep, with_rank=False))],
        [w1pad], [lambda tm: pl.BlockSpec((d, LANES), lambda i, j: (0, 0))],
        [jax.ShapeDtypeStruct((nt, n_proj), F32), jax.ShapeDtypeStruct((nt, LANES), F32)],
        [lambda tm: pl.BlockSpec((tm, tn), lambda i, j: (i, j)),
         lambda tm: pl.BlockSpec((tm, LANES), lambda i, j: (i, 0))], "gla_proj")
    w2pad = jnp.zeros((2, LANES, heads * dk), F32)
    w2pad = w2pad.at[0, 0:rank].set(gate_w2[0]).at[1, rank:2 * rank].set(gate_w2[1])
    bias = gate_b.reshape(2, 1, heads * dk)

    c = GLA_CHUNK
    n_lat, n_ctx = s_lat // c, cx // c
    nch = n_lat + n_ctx

    def fwd(s):
        return jnp.where(s < n_ctx, n_lat + s, s - n_ctx)

    def bwd(s):
        return nch - 1 - s

    tz = _pick(nt, 512, c)
    b_f, b_b = pl.pallas_call(
        functools.partial(_gla_decay_kernel, c=c),
        grid=(nt // tz,),
        in_specs=[pl.BlockSpec((tz, LANES), lambda i: (i, 0)),
                  pl.BlockSpec(w2pad.shape, lambda i: (0, 0, 0)),
                  pl.BlockSpec(bias.shape, lambda i: (0, 0, 0))],
        out_specs=[pl.BlockSpec((tz, heads * dk), lambda i: (i, 0))] * 2,
        out_shape=[jax.ShapeDtypeStruct((nt, heads * dk), F32)] * 2,
        compiler_params=_params(("arbitrary",)),
        name="gla_decay",
    )(z1, w2pad, bias)

    def specs(row_of):
        return [pl.BlockSpec((c, heads * dk), lambda s: (row_of(s), 0)),
                pl.BlockSpec((c, heads * dk), lambda s: (row_of(s), 1)),
                pl.BlockSpec((c, heads * dv), lambda s: (row_of(s), (2 * heads * dk) // (heads * dv))),
                pl.BlockSpec((c, heads * dk), lambda s: (row_of(s), 0))]

    pad_rows = c + 2 * GLA_SUB
    of, ob = pl.pallas_call(
        functools.partial(_gla_kernel, heads=heads),
        grid=(nch,),
        in_specs=specs(fwd) + specs(bwd),
        out_specs=[pl.BlockSpec((c, heads * dv), lambda s: (fwd(s), 0)),
                   pl.BlockSpec((c, heads * dv), lambda s: (bwd(s), 0))],
        out_shape=[jax.ShapeDtypeStruct((nt, d), F32)] * 2,
        scratch_shapes=[pltpu.VMEM((2, heads, dk, dv), F32),
                        pltpu.VMEM((2, c, heads * dk), F32),
                        pltpu.VMEM((2, heads, 2, pad_rows, dk), F32)],
        compiler_params=_params(("arbitrary",)),
        name="gla_scan",
    )(proj, proj, proj, b_f, proj, proj, proj, b_b)

    tm = _pick(nt, 512, ROW_CHUNK)
    gblk = (2 * heads * dk + heads * dv) // d
    a = pl.pallas_call(
        functools.partial(_gla_post_kernel, dv=dv),
        grid=(nt // tm,),
        in_specs=[pl.BlockSpec((tm, d), lambda i: (i, 0)),
                  pl.BlockSpec((tm, d), lambda i: (i, 0)),
                  pl.BlockSpec((tm, d), lambda i: (i, gblk)),
                  pl.BlockSpec((1, dv), lambda i: (0, 0))],
        out_specs=pl.BlockSpec((tm, d), lambda i: (i, 0)),
        out_shape=jax.ShapeDtypeStruct((nt, d), BF16),
        compiler_params=_params(("arbitrary",)),
        name="gla_post",
    )(of, ob, proj, onorm.reshape(1, dv))
    return _matmul_residual(a, w_out, t_arr, nt, s_lat, mod, 2 * d, "gla_out")


def _router_kernel(x_ref, g_ref, mod_ref, rw_ref, h_ref, aff_ref, *, s_lat, tm, n_exp, shift_col, scale_col):
    i = pl.program_id(0)
    gain, shift = _mod_vectors(g_ref[...], mod_ref[...], shift_col, scale_col)
    rw_hi, rw_lo = _split2(rw_ref[...])
    rw2 = (rw_hi.astype(F32) + pltpu.roll(rw_lo.astype(F32), n_exp, axis=1)).astype(BF16)

    def slab(r0):
        h = _norm_mod(x_ref[pl.ds(r0, ROW_CHUNK), :], gain, shift, i * tm + r0 >= s_lat)
        h_ref[pl.ds(r0, ROW_CHUNK), :] = h
        h_hi, h_lo = _split2(h)
        p = _dot(h_hi, rw2)
        p2 = _dot(h_lo, rw2)
        logits = p + pltpu.roll(p, LANES - n_exp, axis=1) + p2
        lt = jnp.transpose(logits)[0:n_exp, :]
        m = jnp.max(lt, axis=0, keepdims=True)
        e = jnp.exp(lt - m)
        aff_ref[r0 // ROW_CHUNK] = e / jnp.sum(e, axis=0, keepdims=True)

    _for_rows(tm, ROW_CHUNK, slab)


def _router(t_arr, n_rows, s_lat, g, mod, router_w):
    nt, d = t_arr.shape
    n_exp = router_w.shape[1]
    tm = _pick(n_rows, 1024, ROW_CHUNK)
    rw_pad = jnp.pad(router_w, ((0, 0), (0, LANES - n_exp)))
    hmod, aff = pl.pallas_call(
        functools.partial(_router_kernel, s_lat=s_lat, tm=tm, n_exp=n_exp,
                          shift_col=3 * d, scale_col=4 * d),
        grid=(n_rows // tm,),
        in_specs=[pl.BlockSpec((tm, d), lambda i: (i, 0)),
                  pl.BlockSpec((1, d), lambda i: (0, 0)),
                  pl.BlockSpec(mod.shape, lambda i: (0, 0)),
                  pl.BlockSpec((d, LANES), lambda i: (0, 0))],
        out_specs=[pl.BlockSpec((tm, d), lambda i: (i, 0)),
                   pl.BlockSpec((tm // ROW_CHUNK, n_exp, LANES), lambda i: (i, 0, 0))],
        out_shape=[jax.ShapeDtypeStruct((n_rows, d), F32),
                   jax.ShapeDtypeStruct((n_rows // ROW_CHUNK, n_exp, LANES), F32)],
        compiler_params=_params(("arbitrary",)),
        name="moe_router",
    )(t_arr, g.reshape(1, d), mod, rw_pad)
    return hmod, aff


def _select_kernel(aff_ref, idx_ref, gate_ref, *, cap):
    n_exp, nb, _ = aff_ref.shape
    aff = aff_ref[...]
    prefix = jnp.zeros((n_exp, 1, 1), I32)
    for bit in range(30, -1, -1):
        cand = prefix | (1 << bit)
        ge = aff >= lax.bitcast_convert_type(cand, F32)
        cnt = jnp.sum(jnp.sum(ge.astype(F32), axis=2, keepdims=True), axis=1, keepdims=True)
        prefix = jnp.where(cnt >= cap, cand, prefix)
    kth = lax.bitcast_convert_type(prefix, F32)
    gt = (aff > kth).astype(F32)
    eq = (aff == kth).astype(F32)
    n_gt = jnp.sum(jnp.sum(gt, axis=2, keepdims=True), axis=1, keepdims=True)
    need = cap - n_gt

    li = lax.broadcasted_iota(I32, (LANES, LANES), 0)
    lj = lax.broadcasted_iota(I32, (LANES, LANES), 1)
    upper = (li <= lj).astype(BF16)
    bi = lax.broadcasted_iota(I32, (nb, nb), 0)
    bj = lax.broadcasted_iota(I32, (nb, nb), 1)
    lower_strict = (bj < bi).astype(BF16)
    upper_nb = (bi <= bj).astype(BF16)
    ones_rows = jnp.ones((BF16_ROWS, LANES), BF16)
    pcol = lax.broadcasted_iota(I32, (cap, 1), 0).astype(F32)
    brow = lax.broadcasted_iota(I32, (1, nb), 1).astype(F32)
    lane_row = lax.broadcasted_iota(I32, (1, LANES), 1).astype(F32)

    for e in range(n_exp):
        eq_e = eq[e].astype(BF16)
        rank = _dot(eq_e, upper) + jnp.sum(_dot(lower_strict, eq_e), axis=1, keepdims=True)
        mask = jnp.maximum(gt[e], eq[e] * (rank <= need[e]).astype(F32))
        m_bf = mask.astype(BF16)
        lcs = _dot(m_bf, upper)
        tot = _dot_nt(ones_rows, m_bf)
        cb_row = _dot(tot.astype(BF16), upper_nb)[0:1, :]
        le = cb_row <= pcol
        blk = jnp.sum(le.astype(F32), axis=1, keepdims=True)
        lt = pcol - jnp.max(jnp.where(le, cb_row, 0.0), axis=1, keepdims=True)
        onehot = (brow == blk).astype(BF16)
        rowcs = _dot(onehot, lcs.astype(BF16))
        j = jnp.sum((rowcs <= lt).astype(F32), axis=1, keepdims=True)
        idx_ref[e] = (blk * LANES + j).astype(I32)
        a1, a2, a3 = _split3(aff[e])
        arow = _dot(onehot, a1) + _dot(onehot, a2) + _dot(onehot, a3)
        gate_ref[e] = jnp.sum(jnp.where(lane_row == j, arow, 0.0), axis=1, keepdims=True)


def _select(aff, cap):
    n_exp = aff.shape[0]
    return pl.pallas_call(
        functools.partial(_select_kernel, cap=cap),
        grid=(1,),
        in_specs=[pl.BlockSpec(aff.shape, lambda i: (0, 0, 0))],
        out_specs=[pl.BlockSpec((n_exp, cap, 1), lambda i: (0, 0, 0)),
                   pl.BlockSpec((n_exp, cap, 1), lambda i: (0, 0, 0))],
        out_shape=[jax.ShapeDtypeStruct((n_exp, cap, 1), I32),
                   jax.ShapeDtypeStruct((n_exp, cap, 1), F32)],
        compiler_params=_params(("arbitrary",)),
        name="moe_select",
    )(aff)


def _row_copy(hbm, buf, rows_ref, p, buf_tile, buf_sub, gather, sem):
    tiles_ref, subs_ref = rows_ref
    src = hbm.at[tiles_ref[0, 0, p], pl.ds(subs_ref[0, 0, p], 1), :]
    dst = buf.at[buf_tile, pl.ds(buf_sub, 1), :]
    if not gather:
        src, dst = dst, src
    return pltpu.make_async_copy(src, dst, sem)


def _start_row_copies(hbm, buf, rows_ref, sem, n_rows, gather):
    def body(p, carry):
        tile = lax.shift_right_logical(p, SUBLANES.bit_length() - 1)
        _row_copy(hbm, buf, rows_ref, p, tile, p & (SUBLANES - 1), gather, sem).start()
        return carry

    lax.fori_loop(0, n_rows, body, 0, unroll=8)


def _wait_row_copies(buf, sem):
    pltpu.make_async_copy(buf, buf, sem).wait()


def _ffn_kernel(tp_ref, sp_ref, tc_ref, sc_ref, tn_ref, sn_ref, gate_ref, h_hbm, x_hbm, wg_ref, wu_ref,
                wd_ref, mod_ref, o_hbm, xg_ref, xb_ref, y_ref, acc_ref, wgb_ref, wub_ref, wdb_ref, sem,
                *, k_lat, kt, nf, mc, n_exp, s_tiles, g_tiles, x_tiles):
    del x_hbm
    e = pl.program_id(0)
    f = pl.program_id(1)
    x_sem, acc_sem, out_sem = sem.at[0], sem.at[1], sem.at[2]
    rows_prev, rows_cur, rows_next = (tp_ref, sp_ref), (tc_ref, sc_ref), (tn_ref, sn_ref)
    d = xb_ref.shape[1]
    mc8 = mc // SUBLANES

    def tiles_of(r0):
        return pl.ds(pl.multiple_of(r0 // SUBLANES, mc8), mc8)

    @pl.when(f == 0)
    def _():
        @pl.when(e == 0)
        def _():
            _start_row_copies(h_hbm, xg_ref, rows_cur, x_sem, kt, True)
            _start_row_copies(o_hbm, acc_ref, rows_cur, acc_sem, kt, True)
            _wait_row_copies(acc_ref, acc_sem)

        _wait_row_copies(xg_ref, x_sem)

        def cast(r0):
            xb_ref[pl.ds(r0, mc), :] = xg_ref[tiles_of(r0)].reshape(mc, d).astype(BF16)

        _for_rows(kt, mc, cast)

    wgb_ref[...] = wg_ref[0, 0].astype(BF16)
    wub_ref[...] = wu_ref[0, 0].astype(BF16)
    wdb_ref[...] = wd_ref[0, 0].astype(BF16)

    chunks = kt // mc
    streams = (
        (x_tiles, h_hbm, xg_ref, rows_next, True, x_sem),
        (s_tiles, o_hbm, acc_ref, rows_prev, False, out_sem),
        (g_tiles, o_hbm, acc_ref, rows_cur, True, acc_sem),
    )
    for fv in range(nf):
        jobs = []
        for stream in streams:
            tiles = stream[0]
            if fv in tiles:
                per = kt // (len(tiles) * chunks)
                assert per % SUBLANES == 0
                jobs.append((tiles.index(fv) * chunks * per, per, stream[1:]))

        @pl.when(f == fv)
        def _(fv=fv, jobs=jobs):
            if fv == g_tiles[0]:
                _wait_row_copies(acc_ref, out_sem)

            def ffn(r0):
                xb = xb_ref[pl.ds(r0, mc), :]
                a = _dot(xb, wgb_ref[...])
                u = _dot(xb, wub_ref[...])
                for first, per, (hbm, buf, rows_ref, gather, sem_) in jobs:
                    base = first + (r0 // mc) * per
                    base_tile = base // SUBLANES
                    for k in range(per):
                        _row_copy(hbm, buf, rows_ref, base + k, base_tile + k // SUBLANES, k % SUBLANES,
                                  gather, sem_).start()
                part = _dot((_silu(a) * u).astype(BF16), wdb_ref[...])
                if fv == 0:
                    y_ref[pl.ds(r0, mc), :] = part
                else:
                    y_ref[pl.ds(r0, mc), :] += part

            _for_rows(kt, mc, ffn)

    @pl.when(f == nf - 1)
    def _():
        _wait_row_copies(acc_ref, acc_sem)
        mod = mod_ref[...]

        def rmw(r0):
            rows = r0 + lax.broadcasted_iota(I32, (mc, 1), 0)
            g2 = jnp.where(rows >= k_lat, mod[1:2, :], mod[0:1, :])
            upd = g2 * (y_ref[pl.ds(r0, mc), :] * gate_ref[0, pl.ds(r0, mc), :])
            acc_ref[tiles_of(r0)] = acc_ref[tiles_of(r0)] + upd.reshape(mc8, SUBLANES, d)

        _for_rows(kt, mc, rmw)

        @pl.when(e == n_exp - 1)
        def _():
            _start_row_copies(o_hbm, acc_ref, rows_cur, out_sem, kt, False)
            _wait_row_copies(acc_ref, out_sem)
            _wait_row_copies(xg_ref, x_sem)


def _moe_ffn(t_arr, hmod, idx, gate, mod, w_gate, w_up, w_down, layer, k_lat):
    nt, d = t_arr.shape
    _, n_exp, _, ff = w_gate.shape
    kt = idx.shape[1]
    tf = _pick(ff, MXU_DIM, LANES)
    nf = ff // tf
    mc = _pick(kt, 528, BF16_ROWS)
    assert nf >= 3 and nt % SUBLANES == 0 and kt % SUBLANES == 0
    n_s = max(1, nf // 3)
    g0 = min(n_s + 1, nf - 2)
    s_tiles = tuple(range(n_s))
    g_tiles = tuple(range(g0, g0 + max(1, min(n_s, nf - 1 - g0))))
    x_tiles = tuple(t for t in range(nf) if t not in s_tiles + g_tiles)
    chunks = kt // mc
    assert all(kt % (len(t) * chunks) == 0 for t in (s_tiles, g_tiles, x_tiles))
    tile3 = lax.shift_right_logical(idx, SUBLANES.bit_length() - 1).reshape(n_exp, 1, kt)
    sub3 = (idx & (SUBLANES - 1)).reshape(n_exp, 1, kt)
    smem = functools.partial(pl.BlockSpec, (1, 1, kt), memory_space=pltpu.SMEM)
    prev_e = lambda e, f: (jnp.maximum(e - 1, 0), 0, 0)
    cur_e = lambda e, f: (e, 0, 0)
    next_e = lambda e, f: (jnp.minimum(e + 1, n_exp - 1), 0, 0)
    out = pl.pallas_call(
        functools.partial(_ffn_kernel, k_lat=k_lat, kt=kt, nf=nf, mc=mc, n_exp=n_exp,
                          s_tiles=s_tiles, g_tiles=g_tiles, x_tiles=x_tiles),
        grid=(n_exp, nf),
        in_specs=[smem(prev_e), smem(prev_e), smem(cur_e), smem(cur_e), smem(next_e), smem(next_e),
                  pl.BlockSpec((1, kt, 1), lambda e, f: (e, 0, 0)),
                  pl.BlockSpec(memory_space=pl.ANY),
                  pl.BlockSpec(memory_space=pl.ANY),
                  pl.BlockSpec((1, 1, d, tf), lambda e, f: (layer, e, 0, f)),
                  pl.BlockSpec((1, 1, d, tf), lambda e, f: (layer, e, 0, f)),
                  pl.BlockSpec((1, 1, tf, d), lambda e, f: (layer, e, f, 0)),
                  pl.BlockSpec((SUBLANES, d), lambda e, f: (0, 5))],
        out_specs=pl.BlockSpec(memory_space=pl.ANY),
        out_shape=jax.ShapeDtypeStruct((nt // SUBLANES, SUBLANES, d), F32),
        input_output_aliases={8: 0},
        scratch_shapes=[pltpu.VMEM((kt // SUBLANES, SUBLANES, d), F32), pltpu.VMEM((kt, d), BF16),
                        pltpu.VMEM((kt, d), F32), pltpu.VMEM((kt // SUBLANES, SUBLANES, d), F32),
                        pltpu.VMEM((d, tf), BF16), pltpu.VMEM((d, tf), BF16), pltpu.VMEM((tf, d), BF16),
                        pltpu.SemaphoreType.DMA((3,))],
        compiler_params=_params(("arbitrary", "arbitrary")),
        name="moe_ffn",
    )(tile3, sub3, tile3, sub3, tile3, sub3, gate, hmod.reshape(-1, SUBLANES, d),
      t_arr.reshape(nt // SUBLANES, SUBLANES, d), w_gate, w_up, w_down, mod)
    return out.reshape(nt, d)


def _pad_blocks(aff):
    nb = aff.shape[1]
    pad = (-nb) % BF16_ROWS
    if pad:
        aff = jnp.concatenate([aff, jnp.full((aff.shape[0], pad, LANES), -1.0, F32)], axis=1)
    return aff


def _moe_layer(t_arr, n_rows, s_lat, mod, g, router_w, w_gate, w_up, w_down, layer):
    n_exp = router_w.shape[1]
    hmod, aff = _router(t_arr, n_rows, s_lat, g, mod, router_w)
    aff = jnp.transpose(aff, (1, 0, 2))
    nb_lat = s_lat // LANES
    cap_lat = max(1, (CAPACITY_FACTOR * s_lat) // n_exp)
    idx, gate = _select(_pad_blocks(aff[:, :nb_lat]), cap_lat)
    idx = idx.reshape(n_exp, cap_lat)
    if n_rows > s_lat:
        cx = n_rows - s_lat
        cap_ctx = max(1, (CAPACITY_FACTOR * cx) // n_exp)
        idx_c, gate_c = _select(_pad_blocks(aff[:, nb_lat:]), cap_ctx)
        idx = jnp.concatenate([idx, idx_c.reshape(n_exp, cap_ctx) + s_lat], axis=1)
        gate = jnp.concatenate([gate, gate_c], axis=1)
    return _moe_ffn(t_arr, hmod, idx, gate, mod, w_gate, w_up, w_down, layer, cap_lat)


def _final_kernel(x_ref, g_ref, o_ref):
    x = x_ref[...]
    ms = jnp.mean(x * x, axis=-1, keepdims=True)
    o_ref[...] = x * lax.rsqrt(ms + RMS_EPS) * g_ref[...]


def _final_norm(t_arr, s_lat, g):
    d = t_arr.shape[1]
    tm = _pick(s_lat, 512, ROW_CHUNK)
    return pl.pallas_call(
        _final_kernel,
        grid=(s_lat // tm,),
        in_specs=[pl.BlockSpec((tm, d), lambda i: (i, 0)), pl.BlockSpec((1, d), lambda i: (0, 0))],
        out_specs=pl.BlockSpec((tm, d), lambda i: (i, 0)),
        out_shape=jax.ShapeDtypeStruct((s_lat, d), F32),
        compiler_params=_params(("arbitrary",)),
        name="final_norm",
    )(t_arr, g.reshape(1, d))


def kernel(x, c, ctx, c_ctx, mod_w, mod_b, norm_mix, norm_ffn, conv_w_in, conv_w_dw, conv_w_out,
           diff_w_qkv, diff_lambda, diff_subln, diff_w_out, gla_w_in, gla_gate_w1, gla_gate_w2,
           gla_gate_b, gla_onorm, gla_w_out, router_w, exp_w_gate, exp_w_up, exp_w_down, final_norm):
    batch, s_lat, d = x.shape
    cx = ctx.shape[1]
    depth = mod_w.shape[0]
    assert batch == 1 and s_lat % GRID_W == 0 and s_lat % ROW_CHUNK == 0
    nt = s_lat + cx
    t_arr = jnp.concatenate([x[0], ctx[0]], axis=0)
    cvec8 = jnp.concatenate([c, c_ctx[None, :], jnp.zeros((SUBLANES - 2, d), F32)], axis=0)
    mods = _modulation(cvec8, mod_w, mod_b)
    cos_t, sin_t = _rope_tables(nt, s_lat)

    for i in range(depth):
        kind, j = i % N_MIXERS, i // N_MIXERS
        ctx_next = i < depth - 1
        ctx_read = ctx_next or kind != 0
        n_rows = nt if ctx_read else s_lat
        if t_arr.shape[0] != n_rows:
            t_arr = t_arr[:n_rows]
        mod = mods[i]
        if kind == 0:
            t_arr = _short_conv_layer(t_arr, n_rows, s_lat, mod, norm_mix[i],
                                      conv_w_in[j], conv_w_dw[j], conv_w_out[j])
        elif kind == 1:
            lambda_init = DIFF_LAMBDA_A - DIFF_LAMBDA_B * math.exp(-DIFF_LAMBDA_C * i)
            t_arr = _diff_layer(t_arr, s_lat, mod, norm_mix[i], diff_w_qkv[j], diff_lambda[j],
                                diff_subln[j], diff_w_out[j], lambda_init, cos_t, sin_t)
        else:
            t_arr = _gla_layer(t_arr, s_lat, mod, norm_mix[i], gla_w_in[j], gla_gate_w1[j],
                               gla_gate_w2[j], gla_gate_b[j], gla_onorm[j], gla_w_out[j])
        n_moe = nt if ctx_next else s_lat
        if t_arr.shape[0] != n_moe:
            t_arr = t_arr[:n_moe]
        t_arr = _moe_layer(t_arr, n_moe, s_lat, mod, norm_ffn[i], router_w[i],
                           exp_w_gate, exp_w_up, exp_w_down, i)
    return _final_norm(t_arr, s_lat, final_norm)[None]
```

```python
import functools
import math

import jax
import jax.numpy as jnp
from jax import lax
from jax.experimental import pallas as pl
from jax.experimental.pallas import tpu as pltpu

F32 = jnp.float32
BF16 = jnp.bfloat16
I32 = jnp.int32

GRID_W = 64
N_MIXERS = 3
RMS_EPS = 1e-6
DIFF_HEAD_DIM = 128
DIFF_SUBLN_EPS = 1e-5
DIFF_LAMBDA_A = 0.8
DIFF_LAMBDA_B = 0.6
DIFF_LAMBDA_C = 0.3
ROPE_THETA = 10000.0
GLA_HEADS = 4
GLA_TAU = 16.0
GLA_CHUNK = 128
GLA_SUB = 16
GLA_FACTOR_BOUND = 60.0
CAPACITY_FACTOR = 2

LANES = 128
SUBLANES = 8
BF16_ROWS = 16
MXU_DIM = 256
VMEM_LIMIT_BYTES = 56 * 1024 * 1024
ROW_CHUNK = 128
MATMUL_ROWS = 768
ATTN_QUERIES = 256
ATTN_KEYS = 512
CAST_COPIES = 48


def _params(sem, vmem=VMEM_LIMIT_BYTES):
    return pltpu.CompilerParams(dimension_semantics=sem, vmem_limit_bytes=vmem)


def _pick(n, cap, mult):
    best = None
    for d in range(mult, min(n, cap) + 1, mult):
        if n % d == 0:
            best = d
    assert best is not None, (n, cap, mult)
    return best


def _dot(a, b):
    return jnp.dot(a, b, preferred_element_type=F32)


def _dot_nt(a, b):
    return lax.dot_general(a, b, (((1,), (1,)), ((), ())), preferred_element_type=F32)


def _dot_tn(a, b):
    return lax.dot_general(a, b, (((0,), (0,)), ((), ())), preferred_element_type=F32)


def _split2(x):
    hi = x.astype(BF16)
    lo = (x - hi.astype(F32)).astype(BF16)
    return hi, lo


def _split3(x):
    a = x.astype(BF16)
    r = x - a.astype(F32)
    b = r.astype(BF16)
    c = (r - b.astype(F32)).astype(BF16)
    return a, b, c


def _sigmoid(x):
    return 1.0 / (1.0 + jnp.exp(-x))


def _silu(x):
    return x * _sigmoid(x)


def _for_rows(n_rows, chunk, fn):
    def body(r, carry):
        fn(pl.multiple_of(r * chunk, chunk))
        return carry

    lax.fori_loop(0, n_rows // chunk, body, 0)


def _mod_vectors(g, mod, shift_col, scale_col):
    d = g.shape[-1]
    return g * (1.0 + mod[0:2, scale_col:scale_col + d]), mod[0:2, shift_col:shift_col + d]


def _norm_mod(x, gain, shift, is_ctx):
    ms = jnp.mean(x * x, axis=-1, keepdims=True)
    gain = jnp.where(is_ctx, gain[1:2], gain[0:1])
    shift = jnp.where(is_ctx, shift[1:2], shift[0:1])
    return x * lax.rsqrt(ms + RMS_EPS) * gain + shift


def _mod_kernel(c_ref, w_ref, b_ref, o_ref):
    @pl.when(pl.program_id(1) == 0)
    def _():
        o_ref[0] = jnp.broadcast_to(b_ref[0], o_ref.shape[1:])

    s_hi, s_lo = _split2(_silu(c_ref[...]))
    w_hi, w_lo = _split2(w_ref[0])
    o_ref[0] += _dot(s_hi, w_hi) + _dot(s_lo, w_hi) + _dot(s_hi, w_lo)


def _modulation(cvec8, mod_w, mod_b):
    depth, d, n6 = mod_w.shape
    tk = _pick(d, 256, LANES)
    return pl.pallas_call(
        _mod_kernel,
        grid=(depth, d // tk),
        in_specs=[
            pl.BlockSpec((SUBLANES, tk), lambda l, k: (0, k)),
            pl.BlockSpec((1, tk, n6), lambda l, k: (l, k, 0)),
            pl.BlockSpec((1, 1, n6), lambda l, k: (l, 0, 0)),
        ],
        out_specs=pl.BlockSpec((1, SUBLANES, n6), lambda l, k: (l, 0, 0)),
        out_shape=jax.ShapeDtypeStruct((depth, SUBLANES, n6), F32),
        compiler_params=_params(("arbitrary", "arbitrary")),
        name="modulation",
    )(cvec8, mod_w, mod_b.reshape(depth, 1, n6))


def _rope_kernel(inv_ref, sgn_ref, cos_ref, sin_ref, *, s_lat, tm):
    i = pl.program_id(0)
    t = i * tm + lax.broadcasted_iota(I32, (tm, LANES), 0)
    lane = lax.broadcasted_iota(I32, (tm, LANES), 1)
    row = t // GRID_W
    col = t % GRID_W
    pos = jnp.where(lane < DIFF_HEAD_DIM // 2, row, col)
    pos = jnp.where(t < s_lat, pos, 0)
    ang = pos.astype(F32) * inv_ref[...]
    cos_ref[...] = jnp.cos(ang)
    sin_ref[...] = jnp.sin(ang) * sgn_ref[...]


def _rope_tables(nt, s_lat):
    quarter = DIFF_HEAD_DIM // 4
    inv = ROPE_THETA ** (-jnp.arange(quarter, dtype=F32) / quarter)
    inv128 = jnp.tile(inv, 4).reshape(1, LANES)
    sgn = jnp.tile(jnp.concatenate([-jnp.ones((quarter,), F32), jnp.ones((quarter,), F32)]), 2)
    tm = _pick(nt, 1024, LANES)
    return pl.pallas_call(
        functools.partial(_rope_kernel, s_lat=s_lat, tm=tm),
        grid=(nt // tm,),
        in_specs=[pl.BlockSpec((1, LANES), lambda i: (0, 0)),
                  pl.BlockSpec((1, LANES), lambda i: (0, 0))],
        out_specs=[pl.BlockSpec((tm, LANES), lambda i: (i, 0)),
                   pl.BlockSpec((tm, LANES), lambda i: (i, 0))],
        out_shape=[jax.ShapeDtypeStruct((nt, LANES), F32)] * 2,
        compiler_params=_params(("arbitrary",)),
        name="rope_tables",
    )(inv128, sgn.reshape(1, LANES))


def _nmm_kernel(*refs, n_w, n_extra, n_out, s_lat, tm, mc, shift_col, scale_col, epilogue):
    x_ref, g_ref, mod_ref = refs[:3]
    w_refs = refs[3:3 + n_w]
    extra = refs[3 + n_w:3 + n_w + n_extra]
    outs = refs[3 + n_w + n_extra:3 + n_w + n_extra + n_out]
    h_ref = refs[-1]
    i = pl.program_id(0)
    j = pl.program_id(1)

    @pl.when(j == 0)
    def _():
        gain, shift = _mod_vectors(g_ref[...], mod_ref[...], shift_col, scale_col)

        def slab(r0):
            h = _norm_mod(x_ref[pl.ds(r0, ROW_CHUNK), :], gain, shift, i * tm + r0 >= s_lat)
            h_ref[pl.ds(r0, ROW_CHUNK), :] = h.astype(BF16)

        _for_rows(tm, ROW_CHUNK, slab)

    for applies, ep in epilogue:
        @pl.when(applies(j))
        def _(ep=ep):
            for r in range(tm // mc):
                rows = pl.ds(r * mc, mc)
                h = h_ref[rows, :]
                ep([_dot(h, w[...]) for w in w_refs], extra, outs, j, rows, h)


def _always(j):
    return j >= 0


def _norm_mod_matmul(t_arr, n_rows, s_lat, g, mod, shift_col, scale_col, w, w_col_blocks, tn,
                     n_tiles, epilogue, extra, extra_specs, out_shapes, out_specs, name):
    d = t_arr.shape[1]
    assert w.dtype == BF16
    tm = _pick(n_rows, 1408, ROW_CHUNK)
    mc = _pick(tm, MATMUL_ROWS, BF16_ROWS)
    w_specs = [pl.BlockSpec((d, tn), functools.partial(lambda i, j, o: (0, o + j), o=o))
               for o in w_col_blocks]
    kern = functools.partial(
        _nmm_kernel, n_w=len(w_col_blocks), n_extra=len(extra), n_out=len(out_shapes),
        s_lat=s_lat, tm=tm, mc=mc, shift_col=shift_col, scale_col=scale_col, epilogue=epilogue)
    return pl.pallas_call(
        kern,
        grid=(n_rows // tm, n_tiles),
        in_specs=[pl.BlockSpec((tm, d), lambda i, j: (i, 0)),
                  pl.BlockSpec((1, d), lambda i, j: (0, 0)),
                  pl.BlockSpec(mod.shape, lambda i, j: (0, 0))]
                 + w_specs + [s(tm) for s in extra_specs],
        out_specs=[s(tm) for s in out_specs],
        out_shape=out_shapes,
        scratch_shapes=[pltpu.VMEM((tm, d), BF16)],
        compiler_params=_params(("arbitrary", "arbitrary")),
        name=name,
    )(t_arr, g.reshape(1, d), mod, *([w] * len(w_col_blocks)), *extra)


def _gated_residual_rows(a_ref, w_ref, x_ref, mod_ref, o_ref, i, s_lat, tm, mc):
    mod = mod_ref[...]
    w = w_ref[...].astype(BF16)
    for r in range(tm // mc):
        rows = pl.ds(r * mc, mc)
        acc = _dot(a_ref[rows, :], w)
        tok = i * tm + r * mc + lax.broadcasted_iota(I32, (mc, 1), 0)
        gate = jnp.where(tok >= s_lat, mod[1:2, :], mod[0:1, :])
        o_ref[rows, :] = x_ref[rows, :] + gate * acc


def _mmres_kernel(a_ref, w_ref, x_ref, mod_ref, o_ref, *, s_lat, tm, mc):
    _gated_residual_rows(a_ref, w_ref, x_ref, mod_ref, o_ref, pl.program_id(0), s_lat, tm, mc)


def _matmul_residual(a, w, t_arr, n_rows, s_lat, mod, gate_col, name):
    k = a.shape[1]
    d = t_arr.shape[1]
    tm = _pick(n_rows, 1408, ROW_CHUNK)
    mc = _pick(tm, MATMUL_ROWS, BF16_ROWS)
    tn = _pick(d, 512, LANES)
    gblk = gate_col // tn
    return pl.pallas_call(
        functools.partial(_mmres_kernel, s_lat=s_lat, tm=tm, mc=mc),
        grid=(n_rows // tm, d // tn),
        in_specs=[pl.BlockSpec((tm, k), lambda i, j: (i, 0)),
                  pl.BlockSpec((k, tn), lambda i, j: (0, j)),
                  pl.BlockSpec((tm, tn), lambda i, j: (i, j)),
                  pl.BlockSpec((SUBLANES, tn), lambda i, j: (0, gblk + j))],
        out_specs=pl.BlockSpec((tm, tn), lambda i, j: (i, j)),
        out_shape=jax.ShapeDtypeStruct(t_arr.shape, F32),
        input_output_aliases={2: 0},
        compiler_params=_params(("arbitrary", "arbitrary")),
        name=name,
    )(a, w, t_arr, mod)


def _conv_in_epilogue(accs, extra, outs, j, rows, h):
    b, c, u = accs
    outs[0][rows, :] = b.astype(BF16)
    outs[1][rows, :] = c * u


def _conv_out_kernel(b_ref, v_ref, vp_ref, vn_ref, dw_ref, w_ref, x_ref, mod_ref, o_ref, a_ref, buf_ref,
                     *, s_lat, n_rows, tm, mc):
    i = pl.program_id(0)
    j = pl.program_id(1)

    @pl.when(j == 0)
    def _():
        buf_ref[0:SUBLANES, :] = vp_ref[...]
        buf_ref[SUBLANES + tm:2 * SUBLANES + tm, :] = vn_ref[...]

        def copy(r0):
            buf_ref[pl.ds(SUBLANES + r0, ROW_CHUNK), :] = v_ref[pl.ds(r0, ROW_CHUNK), :]

        _for_rows(tm, ROW_CHUNK, copy)
        dw = dw_ref[...]

        def slab(r0):
            win = buf_ref[pl.ds(r0, ROW_CHUNK + 2 * SUBLANES), :]
            prev = win[SUBLANES - 1:SUBLANES - 1 + ROW_CHUNK]
            cur = win[SUBLANES:SUBLANES + ROW_CHUNK]
            nxt = win[SUBLANES + 1:SUBLANES + 1 + ROW_CHUNK]
            rows = i * tm + r0 + lax.broadcasted_iota(I32, (ROW_CHUNK, 1), 0)
            has_prev = (rows != 0) & (rows != s_lat)
            has_next = (rows != s_lat - 1) & (rows != n_rows - 1)
            conv = (jnp.where(has_prev, prev, 0.0) * dw[0:1] + cur * dw[1:2]
                    + jnp.where(has_next, nxt, 0.0) * dw[2:3])
            a = b_ref[pl.ds(r0, ROW_CHUNK), :].astype(F32) * conv
            a_ref[pl.ds(r0, ROW_CHUNK), :] = a.astype(BF16)

        _for_rows(tm, ROW_CHUNK, slab)

    _gated_residual_rows(a_ref, w_ref, x_ref, mod_ref, o_ref, i, s_lat, tm, mc)


def _short_conv_layer(t_arr, n_rows, s_lat, mod, g, w_in, w_dw, w_out):
    nt, d = t_arr.shape
    w_in = w_in.astype(BF16)
    tn = _pick(d, 512, LANES)
    nblk = d // tn
    b, v = _norm_mod_matmul(
        t_arr, n_rows, s_lat, g, mod, 0, d, w_in, [0, nblk, 2 * nblk], tn, nblk,
        [(_always, _conv_in_epilogue)], [], [],
        [jax.ShapeDtypeStruct((n_rows, d), BF16), jax.ShapeDtypeStruct((n_rows, d), F32)],
        [lambda tm: pl.BlockSpec((tm, tn), lambda i, j: (i, j))] * 2, "conv_in")
    tm = _pick(n_rows, 1024, ROW_CHUNK)
    mc = _pick(tm, MATMUL_ROWS, BF16_ROWS)
    hb = tm // SUBLANES
    last = n_rows // SUBLANES - 1
    gblk = (2 * d) // tn
    return pl.pallas_call(
        functools.partial(_conv_out_kernel, s_lat=s_lat, n_rows=n_rows, tm=tm, mc=mc),
        grid=(n_rows // tm, nblk),
        in_specs=[pl.BlockSpec((tm, d), lambda i, j: (i, 0)),
                  pl.BlockSpec((tm, d), lambda i, j: (i, 0)),
                  pl.BlockSpec((SUBLANES, d), lambda i, j: (jnp.maximum(i * hb - 1, 0), 0)),
                  pl.BlockSpec((SUBLANES, d), lambda i, j: (jnp.minimum((i + 1) * hb, last), 0)),
                  pl.BlockSpec(w_dw.shape, lambda i, j: (0, 0)),
                  pl.BlockSpec((d, tn), lambda i, j: (0, j)),
                  pl.BlockSpec((tm, tn), lambda i, j: (i, j)),
                  pl.BlockSpec((SUBLANES, tn), lambda i, j: (0, gblk + j))],
        out_specs=pl.BlockSpec((tm, tn), lambda i, j: (i, j)),
        out_shape=jax.ShapeDtypeStruct((nt, d), F32),
        input_output_aliases={6: 0},
        scratch_shapes=[pltpu.VMEM((tm, d), BF16),
                        pltpu.VMEM((tm + 2 * SUBLANES, d), F32)],
        compiler_params=_params(("arbitrary", "arbitrary")),
        name="conv_out",
    )(b, v, v, v, w_dw, w_out, t_arr, mod)


def _rope_rot(a):
    half = DIFF_HEAD_DIM // 4
    lane = lax.broadcasted_iota(I32, a.shape, 1)
    up = pltpu.roll(a, LANES - half, axis=1)
    dn = pltpu.roll(a, half, axis=1)
    return jnp.where(lane % (2 * half) < half, up, dn)


def _rope_epilogue(accs, extra, outs, j, rows, h, *, n_q, tn):
    acc = accs[0]
    cos_ref, sin_ref = extra
    scale = jnp.where(j < n_q, DIFF_HEAD_DIM ** -0.5 * math.log2(math.e), 1.0).astype(F32)
    cos = cos_ref[rows, :] * scale
    sin = sin_ref[rows, :] * scale
    for gidx in range(tn // LANES):
        a = acc[:, gidx * LANES:(gidx + 1) * LANES]
        outs[0][rows, gidx * LANES:(gidx + 1) * LANES] = (a * cos + _rope_rot(a) * sin).astype(BF16)


def _cast_epilogue(accs, extra, outs, j, rows, h):
    outs[0][rows, :] = accs[0].astype(outs[0].dtype)


def _attn_body(q_ref, k_ref, v_ref, lam_ref, sub_ref, o_ref, k_lo, k_hi, lambda_init):
    hd = DIFF_HEAD_DIM
    tq = q_ref.shape[0]
    chunks = [slice(lo, min(lo + ATTN_KEYS, k_hi)) for lo in range(k_lo, k_hi, ATTN_KEYS)]
    m = jnp.full((2 * tq, 1), -jnp.inf, F32)
    l = jnp.zeros((2 * tq, 1), F32)
    acc = jnp.zeros((2 * tq, 2 * hd), F32)
    for sl in chunks:
        s = jnp.concatenate([_dot_nt(q_ref[:, t * hd:(t + 1) * hd], k_ref[sl, t * hd:(t + 1) * hd])
                             for t in range(2)], axis=0)
        m_new = jnp.maximum(m, jnp.max(s, axis=1, keepdims=True))
        alpha = jnp.exp2(m - m_new)
        p = jnp.exp2(s - m_new)
        l = alpha * l + jnp.sum(p, axis=1, keepdims=True)
        acc = alpha * acc + _dot(p.astype(BF16), v_ref[sl, :])
        m = m_new
    l0, l1, acc0, acc1 = l[:tq], l[tq:], acc[:tq], acc[tq:]
    lp = lam_ref[...]
    lam = (jnp.exp(jnp.sum(lp[0:1] * lp[1:2], axis=1, keepdims=True))
           - jnp.exp(jnp.sum(lp[2:3] * lp[3:4], axis=1, keepdims=True)) + lambda_init)
    o = acc0 * (1.0 / l0) - acc1 * (lam / l1)
    ms = jnp.mean(o * o, axis=1, keepdims=True)
    o = o * lax.rsqrt(ms + DIFF_SUBLN_EPS) * sub_ref[...] * (1.0 - lambda_init)
    o_ref[...] = o.astype(BF16)


def _attn_kernel(q_ref, k_ref, v_ref, lam_ref, sub_ref, o_ref, *, s_lat, lambda_init):
    tq = q_ref.shape[0]
    nt = k_ref.shape[0]
    n_lat_blocks = s_lat // tq
    i = pl.program_id(1)

    @pl.when(i < n_lat_blocks)
    def _():
        _attn_body(q_ref, k_ref, v_ref, lam_ref, sub_ref, o_ref, 0, nt, lambda_init)

    @pl.when(i >= n_lat_blocks)
    def _():
        _attn_body(q_ref, k_ref, v_ref, lam_ref, sub_ref, o_ref, s_lat, nt, lambda_init)


def _diff_attention(qkv, s_lat, lam_p, subln, lambda_init, d):
    nt = qkv.shape[0]
    hw = 2 * DIFF_HEAD_DIM
    heads = d // hw
    tq = ATTN_QUERIES
    assert s_lat % tq == 0 and (nt - s_lat) % tq == 0
    return pl.pallas_call(
        functools.partial(_attn_kernel, s_lat=s_lat, lambda_init=lambda_init),
        grid=(heads, nt // tq),
        in_specs=[pl.BlockSpec((tq, hw), lambda h, i: (i, h)),
                  pl.BlockSpec((nt, hw), lambda h, i: (0, heads + h)),
                  pl.BlockSpec((nt, hw), lambda h, i: (0, 2 * heads + h)),
                  pl.BlockSpec(lam_p.shape, lambda h, i: (0, 0)),
                  pl.BlockSpec((1, hw), lambda h, i: (0, 0))],
        out_specs=pl.BlockSpec((tq, hw), lambda h, i: (i, h)),
        out_shape=jax.ShapeDtypeStruct((nt, d), BF16),
        compiler_params=_params(("arbitrary", "arbitrary")),
        name="diff_attn",
    )(qkv, qkv, qkv, lam_p, subln.reshape(1, hw))


def _diff_layer(t_arr, s_lat, mod, g, w_qkv, lam_p, subln, w_out, lambda_init, cos_t, sin_t):
    nt, d = t_arr.shape
    w_qkv = w_qkv.astype(BF16)
    tn = _pick(d, 512, LANES)
    nblk = d // tn
    ep = [(lambda j: j < 2 * nblk, functools.partial(_rope_epilogue, n_q=nblk, tn=tn)),
          (lambda j: j >= 2 * nblk, _cast_epilogue)]
    (qkv,) = _norm_mod_matmul(
        t_arr, nt, s_lat, g, mod, 0, d, w_qkv, [0], tn, 3 * nblk, ep,
        [cos_t, sin_t], [lambda tm: pl.BlockSpec((tm, LANES), lambda i, j: (i, 0))] * 2,
        [jax.ShapeDtypeStruct((nt, 3 * d), BF16)],
        [lambda tm: pl.BlockSpec((tm, tn), lambda i, j: (i, j))], "diff_qkv")
    o = _diff_attention(qkv, s_lat, lam_p, subln, lambda_init, d)
    return _matmul_residual(o, w_out, t_arr, nt, s_lat, mod, 2 * d, "diff_out")


def _gla_proj_epilogue(accs, extra, outs, j, rows, h, *, n_q, dk, with_rank):
    scale = jnp.where(j < n_q, dk ** -0.5, 1.0).astype(F32)
    outs[0][rows, :] = accs[0] * scale
    if with_rank:
        outs[1][rows, :] = _dot(h, extra[0][...])


def _log_sigmoid(z):
    return jnp.minimum(z, 0.0) - jnp.log(1.0 + jnp.exp(-jnp.abs(z)))


def _gla_time(c, rev):
    rowi = lax.broadcasted_iota(I32, (c, 1), 0)
    coli = lax.broadcasted_iota(I32, (1, c), 1)
    return rowi, coli, ((c - 1 - rowi) if rev else rowi), ((c - 1 - coli) if rev else coli)


def _gla_decay_kernel(z_ref, w2_ref, bias_ref, bf_ref, bb_ref, *, c):
    rows = z_ref.shape[0]
    grp = _pick(rows, MXU_DIM, c)
    z_hi, z_lo = _split2(z_ref[...])
    ri = lax.broadcasted_iota(I32, (grp, grp), 0)
    ci = lax.broadcasted_iota(I32, (grp, grp), 1)
    same_chunk = (ri // c) == (ci // c)
    for t, (o_ref, rev) in enumerate(((bf_ref, False), (bb_ref, True))):
        w_hi, w_lo = _split2(w2_ref[t])
        z = _dot(z_hi, w_hi) + _dot(z_lo, w_hi) + _dot(z_hi, w_lo) + bias_ref[t]
        g = _log_sigmoid(z) * (1.0 / GLA_TAU)
        tri = (same_chunk & ((ci >= ri) if rev else (ci <= ri))).astype(BF16)
        for m in range(rows // grp):
            g1, g2, g3 = _split3(g[m * grp:(m + 1) * grp])
            o_ref[m * grp:(m + 1) * grp, :] = _dot(tri, g1) + _dot(tri, g2) + _dot(tri, g3)


def _gla_block_first(b, rev):
    c = b.shape[0]
    sub = GLA_SUB
    parts = []
    for m in range(c // sub):
        r = m * sub + (sub - 1 if rev else 0)
        parts.append(jnp.broadcast_to(b[r:r + 1, :], (sub, b.shape[1])))
    return jnp.concatenate(parts, axis=0)


def _gla_scores_factored(q, k, b, b_first, rev):
    c = q.shape[0]
    sub = GLA_SUB
    _, _, tau, tau_col = _gla_time(c, rev)
    qt = q * jnp.exp(b - b_first)
    q_parts, k_parts = [], []
    for blk in range(c // sub):
        rb = (c - 1 - sub * blk) if rev else sub * blk
        b_at = b[rb:rb + 1, :]
        in_blk = (tau // sub) == blk
        upto = tau < sub * (blk + 1)
        q_parts.append(jnp.where(in_blk, qt, 0.0).astype(BF16))
        k_parts.append(jnp.where(upto, k * jnp.exp(jnp.where(upto, b_at - b, 0.0)), 0.0).astype(BF16))
    a = _dot_nt(jnp.concatenate(q_parts, axis=1), jnp.concatenate(k_parts, axis=1))
    return jnp.where(tau_col <= tau, a, 0.0)


def _gla_scores_exact(q, k, b, kpad_ref, bpad_ref, rev):
    c = q.shape[0]
    sub = GLA_SUB
    rowi, coli, tau, _ = _gla_time(c, rev)
    kpad_ref[sub:sub + c, :] = k
    bpad_ref[sub:sub + c, :] = b
    a_mat = jnp.zeros((c, c), F32)
    for dlt in range(sub):
        off = sub + dlt if rev else sub - dlt
        k_sh = kpad_ref[off:off + c, :]
        b_sh = bpad_ref[off:off + c, :]
        valid = (tau % sub) >= dlt
        e = jnp.exp(jnp.where(valid, b - b_sh, 0.0))
        dsum = jnp.sum(q * k_sh * e, axis=1, keepdims=True)
        partner = (rowi + dlt) if rev else (rowi - dlt)
        a_mat = a_mat + jnp.where((coli == partner) & valid, dsum, 0.0)
    q_parts, k_parts = [], []
    for blk in range(1, c // sub):
        ref_row = (c - sub * blk) if rev else sub * blk - 1
        b_at = b[ref_row:ref_row + 1, :]
        in_blk = (tau // sub) == blk
        earlier = tau < sub * blk
        qt = jnp.where(in_blk, q * jnp.exp(jnp.where(in_blk, b - b_at, 0.0)), 0.0)
        kt = jnp.where(earlier, k * jnp.exp(jnp.where(earlier, b_at - b, 0.0)), 0.0)
        q_parts.append(qt.astype(BF16))
        k_parts.append(kt.astype(BF16))
    return a_mat + _dot_nt(jnp.concatenate(q_parts, axis=1), jnp.concatenate(k_parts, axis=1))


def _gla_chunk_output(q, k, v, b, a_mat, s_ref, rev):
    c = q.shape[0]
    rowi = lax.broadcasted_iota(I32, (c, 1), 0)
    end_row = 0 if rev else c - 1
    b_end = b[end_row:end_row + 1, :]
    vb = v.astype(BF16)
    s_old = s_ref[...]
    o = _dot(a_mat.astype(BF16), vb) + _dot((q * jnp.exp(b)).astype(BF16), s_old.astype(BF16))
    upd = _dot_tn((k * jnp.exp(b_end - b)).astype(BF16), vb)
    d1, d2, d3 = _split3(jnp.where(rowi == end_row, b, 0.0))
    ones = jnp.ones((c, LANES), BF16)
    decay_col = _dot_tn(d1, ones) + _dot_tn(d2, ones) + _dot_tn(d3, ones)
    s_ref[...] = jnp.exp(decay_col[:, 0:1]) * s_old + upd
    return o


def _gla_kernel(qf_ref, kf_ref, vf_ref, bf_ref, qb_ref, kb_ref, vb_ref, bb_ref,
                of_ref, ob_ref, s_ref, b1_ref, pad_ref, *, heads):
    @pl.when(pl.program_id(0) == 0)
    def _():
        s_ref[...] = jnp.zeros_like(s_ref)
        pad_ref[...] = jnp.zeros_like(pad_ref)

    dk = qf_ref.shape[1] // heads
    dv = vf_ref.shape[1] // heads
    dirs = ((qf_ref, kf_ref, vf_ref, bf_ref, of_ref, False), (qb_ref, kb_ref, vb_ref, bb_ref, ob_ref, True))
    excess = jnp.zeros((1, 1), F32)
    for t, (_, _, _, b_ref, _, rev) in enumerate(dirs):
        b = b_ref[...]
        b_first = _gla_block_first(b, rev)
        b1_ref[t] = b_first
        excess = jnp.maximum(excess, jnp.max(jnp.max(b_first - b, axis=1, keepdims=True), axis=0, keepdims=True))
    mild = jnp.max(excess) <= GLA_FACTOR_BOUND

    def run(exact):
        for t, (q_ref, k_ref, v_ref, b_ref, o_ref, rev) in enumerate(dirs):
            for h in range(heads):
                ks = slice(h * dk, (h + 1) * dk)
                vs = slice(h * dv, (h + 1) * dv)
                q, k, v, b = q_ref[:, ks], k_ref[:, ks], v_ref[:, vs], b_ref[:, ks]
                if exact:
                    a_mat = _gla_scores_exact(q, k, b, pad_ref.at[t, h, 0], pad_ref.at[t, h, 1], rev)
                else:
                    a_mat = _gla_scores_factored(q, k, b, b1_ref[t, :, ks], rev)
                o_ref[:, vs] = _gla_chunk_output(q, k, v, b, a_mat, s_ref.at[t, h], rev)

    pl.when(mild)(lambda: run(False))
    pl.when(jnp.logical_not(mild))(lambda: run(True))


def _gla_post_kernel(of_ref, ob_ref, g_ref, on_ref, a_ref, *, dv):
    o = of_ref[...] + ob_ref[...]
    gate = _silu(g_ref[...])
    for h in range(o.shape[1] // dv):
        oh = o[:, h * dv:(h + 1) * dv]
        ms = jnp.mean(oh * oh, axis=1, keepdims=True)
        y = oh * lax.rsqrt(ms + RMS_EPS) * on_ref[...] * gate[:, h * dv:(h + 1) * dv]
        a_ref[:, h * dv:(h + 1) * dv] = y.astype(BF16)


def _gla_layer(t_arr, s_lat, mod, g, w_in, gate_w1, gate_w2, gate_b, onorm, w_out):
    nt, d = t_arr.shape
    w_in = w_in.astype(BF16)
    cx = nt - s_lat
    heads = GLA_HEADS
    dk = d // (2 * heads)
    dv = d // heads
    rank = gate_w1.shape[2]
    n_proj = w_in.shape[1]
    tn = _pick(d, 512, LANES)
    nq_tiles = (heads * dk) // tn
    w1cat = jnp.concatenate([gate_w1[0], gate_w1[1]], axis=1)
    w1pad = jnp.pad(w1cat, ((0, 0), (0, LANES - 2 * rank))).astype(BF16)
    proj_ep = functools.partial(_gla_proj_epilogue, n_q=nq_tiles, dk=dk)
    proj, z1 = _norm_mod_matmul(
        t_arr, nt, s_lat, g, mod, 0, d, w_in, [0], tn, n_proj // tn,
        [(lambda j: j == 0, functools.partial(proj_ep, with_rank=True)),
         (lambda j: j > 0, functools.partial(proj_ep, with_rank=False))],
        [w1pad], [lambda tm: pl.BlockSpec((d, LANES), lambda i, j: (0, 0))],
        [jax.ShapeDtypeStruct((nt, n_proj), F32), jax.ShapeDtypeStruct((nt, LANES), F32)],
        [lambda tm: pl.BlockSpec((tm, tn), lambda i, j: (i, j)),
         lambda tm: pl.BlockSpec((tm, LANES), lambda i, j: (i, 0))], "gla_proj")
    w2pad = jnp.zeros((2, LANES, heads * dk), F32)
    w2pad = w2pad.at[0, 0:rank].set(gate_w2[0]).at[1, rank:2 * rank].set(gate_w2[1])
    bias = gate_b.reshape(2, 1, heads * dk)

    c = GLA_CHUNK
    n_lat, n_ctx = s_lat // c, cx // c
    nch = n_lat + n_ctx

    def fwd(s):
        return jnp.where(s < n_ctx, n_lat + s, s - n_ctx)

    def bwd(s):
        return nch - 1 - s

    tz = _pick(nt, 512, c)
    b_f, b_b = pl.pallas_call(
        functools.partial(_gla_decay_kernel, c=c),
        grid=(nt // tz,),
        in_specs=[pl.BlockSpec((tz, LANES), lambda i: (i, 0)),
                  pl.BlockSpec(w2pad.shape, lambda i: (0, 0, 0)),
                  pl.BlockSpec(bias.shape, lambda i: (0, 0, 0))],
        out_specs=[pl.BlockSpec((tz, heads * dk), lambda i: (i, 0))] * 2,
        out_shape=[jax.ShapeDtypeStruct((nt, heads * dk), F32)] * 2,
        compiler_params=_params(("arbitrary",)),
        name="gla_decay",
    )(z1, w2pad, bias)

    def specs(row_of):
        return [pl.BlockSpec((c, heads * dk), lambda s: (row_of(s), 0)),
                pl.BlockSpec((c, heads * dk), lambda s: (row_of(s), 1)),
                pl.BlockSpec((c, heads * dv), lambda s: (row_of(s), (2 * heads * dk) // (heads * dv))),
                pl.BlockSpec((c, heads * dk), lambda s: (row_of(s), 0))]

    pad_rows = c + 2 * GLA_SUB
    of, ob = pl.pallas_call(
        functools.partial(_gla_kernel, heads=heads),
        grid=(nch,),
        in_specs=specs(fwd) + specs(bwd),
        out_specs=[pl.BlockSpec((c, heads * dv), lambda s: (fwd(s), 0)),
                   pl.BlockSpec((c, heads * dv), lambda s: (bwd(s), 0))],
        out_shape=[jax.ShapeDtypeStruct((nt, d), F32)] * 2,
        scratch_shapes=[pltpu.VMEM((2, heads, dk, dv), F32),
                        pltpu.VMEM((2, c, heads * dk), F32),
                        pltpu.VMEM((2, heads, 2, pad_rows, dk), F32)],
        compiler_params=_params(("arbitrary",)),
        name="gla_scan",
    )(proj, proj, proj, b_f, proj, proj, proj, b_b)

    tm = _pick(nt, 512, ROW_CHUNK)
    gblk = (2 * heads * dk + heads * dv) // d
    a = pl.pallas_call(
        functools.partial(_gla_post_kernel, dv=dv),
        grid=(nt // tm,),
        in_specs=[pl.BlockSpec((tm, d), lambda i: (i, 0)),
                  pl.BlockSpec((tm, d), lambda i: (i, 0)),
                  pl.BlockSpec((tm, d), lambda i: (i, gblk)),
                  pl.BlockSpec((1, dv), lambda i: (0, 0))],
        out_specs=pl.BlockSpec((tm, d), lambda i: (i, 0)),
        out_shape=jax.ShapeDtypeStruct((nt, d), BF16),
        compiler_params=_params(("arbitrary",)),
        name="gla_post",
    )(of, ob, proj, onorm.reshape(1, dv))
    return _matmul_residual(a, w_out, t_arr, nt, s_lat, mod, 2 * d, "gla_out")


def _router_kernel(x_ref, g_ref, mod_ref, rw_ref, h_ref, aff_ref, *, s_lat, tm, n_exp, shift_col, scale_col):
    i = pl.program_id(0)
    gain, shift = _mod_vectors(g_ref[...], mod_ref[...], shift_col, scale_col)
    rw_hi, rw_lo = _split2(rw_ref[...])
    rw2 = (rw_hi.astype(F32) + pltpu.roll(rw_lo.astype(F32), n_exp, axis=1)).astype(BF16)

    def slab(r0):
        h = _norm_mod(x_ref[pl.ds(r0, ROW_CHUNK), :], gain, shift, i * tm + r0 >= s_lat)
        h_ref[pl.ds(r0, ROW_CHUNK), :] = h
        h_hi, h_lo = _split2(h)
        p = _dot(h_hi, rw2)
        p2 = _dot(h_lo, rw2)
        logits = p + pltpu.roll(p, LANES - n_exp, axis=1) + p2
        lt = jnp.transpose(logits)[0:n_exp, :]
        m = jnp.max(lt, axis=0, keepdims=True)
        e = jnp.exp(lt - m)
        aff_ref[r0 // ROW_CHUNK] = e / jnp.sum(e, axis=0, keepdims=True)

    _for_rows(tm, ROW_CHUNK, slab)


def _router(t_arr, n_rows, s_lat, g, mod, router_w):
    nt, d = t_arr.shape
    n_exp = router_w.shape[1]
    tm = _pick(n_rows, 1024, ROW_CHUNK)
    rw_pad = jnp.pad(router_w, ((0, 0), (0, LANES - n_exp)))
    hmod, aff = pl.pallas_call(
        functools.partial(_router_kernel, s_lat=s_lat, tm=tm, n_exp=n_exp,
                          shift_col=3 * d, scale_col=4 * d),
        grid=(n_rows // tm,),
        in_specs=[pl.BlockSpec((tm, d), lambda i: (i, 0)),
                  pl.BlockSpec((1, d), lambda i: (0, 0)),
                  pl.BlockSpec(mod.shape, lambda i: (0, 0)),
                  pl.BlockSpec((d, LANES), lambda i: (0, 0))],
        out_specs=[pl.BlockSpec((tm, d), lambda i: (i, 0)),
                   pl.BlockSpec((tm // ROW_CHUNK, n_exp, LANES), lambda i: (i, 0, 0))],
        out_shape=[jax.ShapeDtypeStruct((n_rows, d), F32),
                   jax.ShapeDtypeStruct((n_rows // ROW_CHUNK, n_exp, LANES), F32)],
        compiler_params=_params(("arbitrary",)),
        name="moe_router",
    )(t_arr, g.reshape(1, d), mod, rw_pad)
    return hmod, aff


def _select_kernel(aff_ref, idx_ref, gate_ref, *, cap):
    n_exp, nb, _ = aff_ref.shape
    aff = aff_ref[...]
    prefix = jnp.zeros((n_exp, 1, 1), I32)
    for bit in range(30, -1, -1):
        cand = prefix | (1 << bit)
        ge = aff >= lax.bitcast_convert_type(cand, F32)
        cnt = jnp.sum(jnp.sum(ge.astype(F32), axis=2, keepdims=True), axis=1, keepdims=True)
        prefix = jnp.where(cnt >= cap, cand, prefix)
    kth = lax.bitcast_convert_type(prefix, F32)
    gt = (aff > kth).astype(F32)
    eq = (aff == kth).astype(F32)
    n_gt = jnp.sum(jnp.sum(gt, axis=2, keepdims=True), axis=1, keepdims=True)
    need = cap - n_gt

    li = lax.broadcasted_iota(I32, (LANES, LANES), 0)
    lj = lax.broadcasted_iota(I32, (LANES, LANES), 1)
    upper = (li <= lj).astype(BF16)
    bi = lax.broadcasted_iota(I32, (nb, nb), 0)
    bj = lax.broadcasted_iota(I32, (nb, nb), 1)
    lower_strict = (bj < bi).astype(BF16)
    upper_nb = (bi <= bj).astype(BF16)
    ones_rows = jnp.ones((BF16_ROWS, LANES), BF16)
    pcol = lax.broadcasted_iota(I32, (cap, 1), 0).astype(F32)
    brow = lax.broadcasted_iota(I32, (1, nb), 1).astype(F32)
    lane_row = lax.broadcasted_iota(I32, (1, LANES), 1).astype(F32)

    for e in range(n_exp):
        eq_e = eq[e].astype(BF16)
        rank = _dot(eq_e, upper) + jnp.sum(_dot(lower_strict, eq_e), axis=1, keepdims=True)
        mask = jnp.maximum(gt[e], eq[e] * (rank <= need[e]).astype(F32))
        m_bf = mask.astype(BF16)
        lcs = _dot(m_bf, upper)
        tot = _dot_nt(ones_rows, m_bf)
        cb_row = _dot(tot.astype(BF16), upper_nb)[0:1, :]
        le = cb_row <= pcol
        blk = jnp.sum(le.astype(F32), axis=1, keepdims=True)
        lt = pcol - jnp.max(jnp.where(le, cb_row, 0.0), axis=1, keepdims=True)
        onehot = (brow == blk).astype(BF16)
        rowcs = _dot(onehot, lcs.astype(BF16))
        j = jnp.sum((rowcs <= lt).astype(F32), axis=1, keepdims=True)
        idx_ref[e] = (blk * LANES + j).astype(I32)
        a1, a2, a3 = _split3(aff[e])
        arow = _dot(onehot, a1) + _dot(onehot, a2) + _dot(onehot, a3)
        gate_ref[e] = jnp.sum(jnp.where(lane_row == j, arow, 0.0), axis=1, keepdims=True)


def _select(aff, cap):
    n_exp = aff.shape[0]
    return pl.pallas_call(
        functools.partial(_select_kernel, cap=cap),
        grid=(1,),
        in_specs=[pl.BlockSpec(aff.shape, lambda i: (0, 0, 0))],
        out_specs=[pl.BlockSpec((n_exp, cap, 1), lambda i: (0, 0, 0)),
                   pl.BlockSpec((n_exp, cap, 1), lambda i: (0, 0, 0))],
        out_shape=[jax.ShapeDtypeStruct((n_exp, cap, 1), I32),
                   jax.ShapeDtypeStruct((n_exp, cap, 1), F32)],
        compiler_params=_params(("arbitrary",)),
        name="moe_select",
    )(aff)


def _row_copy(hbm, buf, rows_ref, p, buf_tile, buf_sub, gather, sem):
    tiles_ref, subs_ref = rows_ref
    src = hbm.at[tiles_ref[0, 0, p], pl.ds(subs_ref[0, 0, p], 1), :]
    dst = buf.at[buf_tile, pl.ds(buf_sub, 1), :]
    if not gather:
        src, dst = dst, src
    return pltpu.make_async_copy(src, dst, sem)


def _start_row_copies(hbm, buf, rows_ref, sem, n_rows, gather):
    def body(p, carry):
        tile = lax.shift_right_logical(p, SUBLANES.bit_length() - 1)
        _row_copy(hbm, buf, rows_ref, p, tile, p & (SUBLANES - 1), gather, sem).start()
        return carry

    lax.fori_loop(0, n_rows, body, 0, unroll=8)


def _wait_row_copies(buf, sem):
    pltpu.make_async_copy(buf, buf, sem).wait()


def _ffn_kernel(tp_ref, sp_ref, tc_ref, sc_ref, tn_ref, sn_ref, gate_ref, h_hbm, x_hbm, wg_ref, wu_ref,
                wd_ref, mod_ref, o_hbm, xg_ref, xb_ref, y_ref, acc_ref, wgb_ref, wub_ref, wdb_ref, sem,
                *, k_lat, kt, nf, mc, n_exp, s_tiles, g_tiles, x_tiles):
    del x_hbm
    e = pl.program_id(0)
    f = pl.program_id(1)
    x_sem, acc_sem, out_sem = sem.at[0], sem.at[1], sem.at[2]
    rows_prev, rows_cur, rows_next = (tp_ref, sp_ref), (tc_ref, sc_ref), (tn_ref, sn_ref)
    d = xb_ref.shape[1]
    mc8 = mc // SUBLANES

    def tiles_of(r0):
        return pl.ds(pl.multiple_of(r0 // SUBLANES, mc8), mc8)

    @pl.when(f == 0)
    def _():
        @pl.when(e == 0)
        def _():
            _start_row_copies(h_hbm, xg_ref, rows_cur, x_sem, kt, True)
            _start_row_copies(o_hbm, acc_ref, rows_cur, acc_sem, kt, True)
            _wait_row_copies(acc_ref, acc_sem)

        _wait_row_copies(xg_ref, x_sem)

        def cast(r0):
            xb_ref[pl.ds(r0, mc), :] = xg_ref[tiles_of(r0)].reshape(mc, d).astype(BF16)

        _for_rows(kt, mc, cast)

    chunks = kt // mc
    streams = (
        (x_tiles, h_hbm, xg_ref, rows_next, True, x_sem),
        (s_tiles, o_hbm, acc_ref, rows_prev, False, out_sem),
        (g_tiles, o_hbm, acc_ref, rows_cur, True, acc_sem),
    )
    for fv in range(nf):
        jobs = []
        for stream in streams:
            tiles = stream[0]
            if fv in tiles:
                per_tile = kt // len(tiles)
                head = max(h for h in range(0, CAST_COPIES + 1, SUBLANES)
                           if (per_tile - h) % (chunks * SUBLANES) == 0)
                jobs.append((tiles.index(fv) * per_tile, head, (per_tile - head) // chunks, stream[1:]))

        @pl.when(f == fv)
        def _(fv=fv, jobs=jobs):
            if fv == g_tiles[0]:
                _wait_row_copies(acc_ref, out_sem)

            wgb_ref[...] = wg_ref[0, 0].astype(BF16)
            wub_ref[...] = wu_ref[0, 0].astype(BF16)
            wdb_ref[...] = wd_ref[0, 0].astype(BF16)
            for first, head, per, (hbm, buf, rows_ref, gather, sem_) in jobs:
                for k in range(head):
                    p = first + k
                    _row_copy(hbm, buf, rows_ref, p, p // SUBLANES, p % SUBLANES, gather, sem_).start()

            def ffn(r0):
                xb = xb_ref[pl.ds(r0, mc), :]
                a = _dot(xb, wgb_ref[...])
                u = _dot(xb, wub_ref[...])
                for first, head, per, (hbm, buf, rows_ref, gather, sem_) in jobs:
                    base = first + head + (r0 // mc) * per
                    base_tile = base // SUBLANES
                    for k in range(per):
                        _row_copy(hbm, buf, rows_ref, base + k, base_tile + k // SUBLANES, k % SUBLANES,
                                  gather, sem_).start()
                part = _dot((_silu(a) * u).astype(BF16), wdb_ref[...])
                if fv == 0:
                    y_ref[pl.ds(r0, mc), :] = part
                else:
                    y_ref[pl.ds(r0, mc), :] += part

            _for_rows(kt, mc, ffn)

    @pl.when(f == nf - 1)
    def _():
        _wait_row_copies(acc_ref, acc_sem)
        mod = mod_ref[...]

        def rmw(r0):
            rows = r0 + lax.broadcasted_iota(I32, (mc, 1), 0)
            g2 = jnp.where(rows >= k_lat, mod[1:2, :], mod[0:1, :])
            upd = g2 * (y_ref[pl.ds(r0, mc), :] * gate_ref[0, pl.ds(r0, mc), :])
            acc_ref[tiles_of(r0)] = acc_ref[tiles_of(r0)] + upd.reshape(mc8, SUBLANES, d)

        _for_rows(kt, mc, rmw)

        @pl.when(e == n_exp - 1)
        def _():
            _start_row_copies(o_hbm, acc_ref, rows_cur, out_sem, kt, False)
            _wait_row_copies(acc_ref, out_sem)
            _wait_row_copies(xg_ref, x_sem)


def _moe_ffn(t_arr, hmod, idx, gate, mod, w_gate, w_up, w_down, layer, k_lat):
    nt, d = t_arr.shape
    _, n_exp, _, ff = w_gate.shape
    kt = idx.shape[1]
    tf = _pick(ff, MXU_DIM, LANES)
    nf = ff // tf
    mc = _pick(kt, 528, BF16_ROWS)
    assert nf >= 3 and nt % SUBLANES == 0 and kt % SUBLANES == 0
    n_s = max(1, nf // 3)
    g0 = min(n_s + 1, nf - 2)
    s_tiles = tuple(range(n_s))
    g_tiles = tuple(range(g0, g0 + max(1, min(n_s, nf - 1 - g0))))
    x_tiles = tuple(t for t in range(nf) if t not in s_tiles + g_tiles)
    chunks = kt // mc
    assert all(kt % (len(t) * chunks) == 0 for t in (s_tiles, g_tiles, x_tiles))
    tile3 = lax.shift_right_logical(idx, SUBLANES.bit_length() - 1).reshape(n_exp, 1, kt)
    sub3 = (idx & (SUBLANES - 1)).reshape(n_exp, 1, kt)
    smem = functools.partial(pl.BlockSpec, (1, 1, kt), memory_space=pltpu.SMEM)
    prev_e = lambda e, f: (jnp.maximum(e - 1, 0), 0, 0)
    cur_e = lambda e, f: (e, 0, 0)
    next_e = lambda e, f: (jnp.minimum(e + 1, n_exp - 1), 0, 0)
    out = pl.pallas_call(
        functools.partial(_ffn_kernel, k_lat=k_lat, kt=kt, nf=nf, mc=mc, n_exp=n_exp,
                          s_tiles=s_tiles, g_tiles=g_tiles, x_tiles=x_tiles),
        grid=(n_exp, nf),
        in_specs=[smem(prev_e), smem(prev_e), smem(cur_e), smem(cur_e), smem(next_e), smem(next_e),
                  pl.BlockSpec((1, kt, 1), lambda e, f: (e, 0, 0)),
                  pl.BlockSpec(memory_space=pl.ANY),
                  pl.BlockSpec(memory_space=pl.ANY),
                  pl.BlockSpec((1, 1, d, tf), lambda e, f: (layer, e, 0, f)),
                  pl.BlockSpec((1, 1, d, tf), lambda e, f: (layer, e, 0, f)),
                  pl.BlockSpec((1, 1, tf, d), lambda e, f: (layer, e, f, 0)),
                  pl.BlockSpec((SUBLANES, d), lambda e, f: (0, 5))],
        out_specs=pl.BlockSpec(memory_space=pl.ANY),
        out_shape=jax.ShapeDtypeStruct((nt // SUBLANES, SUBLANES, d), F32),
        input_output_aliases={8: 0},
        scratch_shapes=[pltpu.VMEM((kt // SUBLANES, SUBLANES, d), F32), pltpu.VMEM((kt, d), BF16),
                        pltpu.VMEM((kt, d), F32), pltpu.VMEM((kt // SUBLANES, SUBLANES, d), F32),
                        pltpu.VMEM((d, tf), BF16), pltpu.VMEM((d, tf), BF16), pltpu.VMEM((tf, d), BF16),
                        pltpu.SemaphoreType.DMA((3,))],
        compiler_params=_params(("arbitrary", "arbitrary")),
        name="moe_ffn",
    )(tile3, sub3, tile3, sub3, tile3, sub3, gate, hmod.reshape(-1, SUBLANES, d),
      t_arr.reshape(nt // SUBLANES, SUBLANES, d), w_gate, w_up, w_down, mod)
    return out.reshape(nt, d)


def _pad_blocks(aff):
    nb = aff.shape[1]
    pad = (-nb) % BF16_ROWS
    if pad:
        aff = jnp.concatenate([aff, jnp.full((aff.shape[0], pad, LANES), -1.0, F32)], axis=1)
    return aff


def _moe_layer(t_arr, n_rows, s_lat, mod, g, router_w, w_gate, w_up, w_down, layer):
    n_exp = router_w.shape[1]
    hmod, aff = _router(t_arr, n_rows, s_lat, g, mod, router_w)
    aff = jnp.transpose(aff, (1, 0, 2))
    nb_lat = s_lat // LANES
    cap_lat = max(1, (CAPACITY_FACTOR * s_lat) // n_exp)
    idx, gate = _select(_pad_blocks(aff[:, :nb_lat]), cap_lat)
    idx = idx.reshape(n_exp, cap_lat)
    if n_rows > s_lat:
        cx = n_rows - s_lat
        cap_ctx = max(1, (CAPACITY_FACTOR * cx) // n_exp)
        idx_c, gate_c = _select(_pad_blocks(aff[:, nb_lat:]), cap_ctx)
        idx = jnp.concatenate([idx, idx_c.reshape(n_exp, cap_ctx) + s_lat], axis=1)
        gate = jnp.concatenate([gate, gate_c], axis=1)
    return _moe_ffn(t_arr, hmod, idx, gate, mod, w_gate, w_up, w_down, layer, cap_lat)


def _final_kernel(x_ref, g_ref, o_ref):
    x = x_ref[...]
    ms = jnp.mean(x * x, axis=-1, keepdims=True)
    o_ref[...] = x * lax.rsqrt(ms + RMS_EPS) * g_ref[...]


def _final_norm(t_arr, s_lat, g):
    d = t_arr.shape[1]
    tm = _pick(s_lat, 512, ROW_CHUNK)
    return pl.pallas_call(
        _final_kernel,
        grid=(s_lat // tm,),
        in_specs=[pl.BlockSpec((tm, d), lambda i: (i, 0)), pl.BlockSpec((1, d), lambda i: (0, 0))],
        out_specs=pl.BlockSpec((tm, d), lambda i: (i, 0)),
        out_shape=jax.ShapeDtypeStruct((s_lat, d), F32),
        compiler_params=_params(("arbitrary",)),
        name="final_norm",
    )(t_arr, g.reshape(1, d))


def kernel(x, c, ctx, c_ctx, mod_w, mod_b, norm_mix, norm_ffn, conv_w_in, conv_w_dw, conv_w_out,
           diff_w_qkv, diff_lambda, diff_subln, diff_w_out, gla_w_in, gla_gate_w1, gla_gate_w2,
           gla_gate_b, gla_onorm, gla_w_out, router_w, exp_w_gate, exp_w_up, exp_w_down, final_norm):
    batch, s_lat, d = x.shape
    cx = ctx.shape[1]
    depth = mod_w.shape[0]
    assert batch == 1 and s_lat % GRID_W == 0 and s_lat % ROW_CHUNK == 0
    nt = s_lat + cx
    t_arr = jnp.concatenate([x[0], ctx[0]], axis=0)
    cvec8 = jnp.concatenate([c, c_ctx[None, :], jnp.zeros((SUBLANES - 2, d), F32)], axis=0)
    mods = _modulation(cvec8, mod_w, mod_b)
    cos_t, sin_t = _rope_tables(nt, s_lat)

    for i in range(depth):
        kind, j = i % N_MIXERS, i // N_MIXERS
        ctx_next = i < depth - 1
        ctx_read = ctx_next or kind != 0
        n_rows = nt if ctx_read else s_lat
        if t_arr.shape[0] != n_rows:
            t_arr = t_arr[:n_rows]
        mod = mods[i]
        if kind == 0:
            t_arr = _short_conv_layer(t_arr, n_rows, s_lat, mod, norm_mix[i],
                                      conv_w_in[j], conv_w_dw[j], conv_w_out[j])
        elif kind == 1:
            lambda_init = DIFF_LAMBDA_A - DIFF_LAMBDA_B * math.exp(-DIFF_LAMBDA_C * i)
            t_arr = _diff_layer(t_arr, s_lat, mod, norm_mix[i], diff_w_qkv[j], diff_lambda[j],
                                diff_subln[j], diff_w_out[j], lambda_init, cos_t, sin_t)
        else:
            t_arr = _gla_layer(t_arr, s_lat, mod, norm_mix[i], gla_w_in[j], gla_gate_w1[j],
                               gla_gate_w2[j], gla_gate_b[j], gla_onorm[j], gla_w_out[j])
        n_moe = nt if ctx_next else s_lat
        if t_arr.shape[0] != n_moe:
            t_arr = t_arr[:n_moe]
        t_arr = _moe_layer(t_arr, n_moe, s_lat, mod, norm_ffn[i], router_w[i],
                           exp_w_gate, exp_w_up, exp_w_down, i)
    return _final_norm(t_arr, s_lat, final_norm)[None]
```

```python
import functools
import math

import jax
import jax.numpy as jnp
from jax import lax
from jax.experimental import pallas as pl
from jax.experimental.pallas import tpu as pltpu

F32 = jnp.float32
BF16 = jnp.bfloat16
I32 = jnp.int32

GRID_W = 64
N_MIXERS = 3
RMS_EPS = 1e-6
DIFF_HEAD_DIM = 128
DIFF_SUBLN_EPS = 1e-5
DIFF_LAMBDA_A = 0.8
DIFF_LAMBDA_B = 0.6
DIFF_LAMBDA_C = 0.3
ROPE_THETA = 10000.0
GLA_HEADS = 4
GLA_TAU = 16.0
GLA_CHUNK = 128
GLA_SUB = 16
GLA_FACTOR_ROWS = 128
GLA_FACTOR_BOUND = 60.0
CAPACITY_FACTOR = 2

LANES = 128
SUBLANES = 8
BF16_ROWS = 16
MXU_DIM = 256
VMEM_LIMIT_BYTES = 56 * 1024 * 1024
ROW_CHUNK = 128
MATMUL_ROWS = 768
ATTN_QUERIES = 256
ATTN_KEYS = 512
CAST_COPIES = 48


def _params(sem, vmem=VMEM_LIMIT_BYTES):
    return pltpu.CompilerParams(dimension_semantics=sem, vmem_limit_bytes=vmem)


def _pick(n, cap, mult):
    best = None
    for d in range(mult, min(n, cap) + 1, mult):
        if n % d == 0:
            best = d
    assert best is not None, (n, cap, mult)
    return best


def _dot(a, b):
    return jnp.dot(a, b, preferred_element_type=F32)


def _dot_nt(a, b):
    return lax.dot_general(a, b, (((1,), (1,)), ((), ())), preferred_element_type=F32)


def _dot_tn(a, b):
    return lax.dot_general(a, b, (((0,), (0,)), ((), ())), preferred_element_type=F32)


def _split2(x):
    hi = x.astype(BF16)
    lo = (x - hi.astype(F32)).astype(BF16)
    return hi, lo


def _split3(x):
    a = x.astype(BF16)
    r = x - a.astype(F32)
    b = r.astype(BF16)
    c = (r - b.astype(F32)).astype(BF16)
    return a, b, c


def _sigmoid(x):
    return 1.0 / (1.0 + jnp.exp(-x))


def _silu(x):
    return x * _sigmoid(x)


def _for_rows(n_rows, chunk, fn):
    def body(r, carry):
        fn(pl.multiple_of(r * chunk, chunk))
        return carry

    lax.fori_loop(0, n_rows // chunk, body, 0)


def _mod_vectors(g, mod, shift_col, scale_col):
    d = g.shape[-1]
    return g * (1.0 + mod[0:2, scale_col:scale_col + d]), mod[0:2, shift_col:shift_col + d]


def _norm_mod(x, gain, shift, is_ctx):
    ms = jnp.mean(x * x, axis=-1, keepdims=True)
    gain = jnp.where(is_ctx, gain[1:2], gain[0:1])
    shift = jnp.where(is_ctx, shift[1:2], shift[0:1])
    return x * lax.rsqrt(ms + RMS_EPS) * gain + shift


def _mod_kernel(c_ref, w_ref, b_ref, o_ref):
    @pl.when(pl.program_id(1) == 0)
    def _():
        o_ref[0] = jnp.broadcast_to(b_ref[0], o_ref.shape[1:])

    s_hi, s_lo = _split2(_silu(c_ref[...]))
    w_hi, w_lo = _split2(w_ref[0])
    o_ref[0] += _dot(s_hi, w_hi) + _dot(s_lo, w_hi) + _dot(s_hi, w_lo)


def _modulation(cvec8, mod_w, mod_b):
    depth, d, n6 = mod_w.shape
    tk = _pick(d, 256, LANES)
    return pl.pallas_call(
        _mod_kernel,
        grid=(depth, d // tk),
        in_specs=[
            pl.BlockSpec((SUBLANES, tk), lambda l, k: (0, k)),
            pl.BlockSpec((1, tk, n6), lambda l, k: (l, k, 0)),
            pl.BlockSpec((1, 1, n6), lambda l, k: (l, 0, 0)),
        ],
        out_specs=pl.BlockSpec((1, SUBLANES, n6), lambda l, k: (l, 0, 0)),
        out_shape=jax.ShapeDtypeStruct((depth, SUBLANES, n6), F32),
        compiler_params=_params(("arbitrary", "arbitrary")),
        name="modulation",
    )(cvec8, mod_w, mod_b.reshape(depth, 1, n6))


def _rope_kernel(inv_ref, sgn_ref, cos_ref, sin_ref, *, s_lat, tm):
    i = pl.program_id(0)
    t = i * tm + lax.broadcasted_iota(I32, (tm, LANES), 0)
    lane = lax.broadcasted_iota(I32, (tm, LANES), 1)
    row = t // GRID_W
    col = t % GRID_W
    pos = jnp.where(lane < DIFF_HEAD_DIM // 2, row, col)
    pos = jnp.where(t < s_lat, pos, 0)
    ang = pos.astype(F32) * inv_ref[...]
    cos_ref[...] = jnp.cos(ang)
    sin_ref[...] = jnp.sin(ang) * sgn_ref[...]


def _rope_tables(nt, s_lat):
    quarter = DIFF_HEAD_DIM // 4
    inv = ROPE_THETA ** (-jnp.arange(quarter, dtype=F32) / quarter)
    inv128 = jnp.tile(inv, 4).reshape(1, LANES)
    sgn = jnp.tile(jnp.concatenate([-jnp.ones((quarter,), F32), jnp.ones((quarter,), F32)]), 2)
    tm = _pick(nt, 1024, LANES)
    return pl.pallas_call(
        functools.partial(_rope_kernel, s_lat=s_lat, tm=tm),
        grid=(nt // tm,),
        in_specs=[pl.BlockSpec((1, LANES), lambda i: (0, 0)),
                  pl.BlockSpec((1, LANES), lambda i: (0, 0))],
        out_specs=[pl.BlockSpec((tm, LANES), lambda i: (i, 0)),
                   pl.BlockSpec((tm, LANES), lambda i: (i, 0))],
        out_shape=[jax.ShapeDtypeStruct((nt, LANES), F32)] * 2,
        compiler_params=_params(("arbitrary",)),
        name="rope_tables",
    )(inv128, sgn.reshape(1, LANES))


def _nmm_kernel(*refs, n_w, n_extra, n_out, s_lat, tm, mc, shift_col, scale_col, epilogue):
    x_ref, g_ref, mod_ref = refs[:3]
    w_refs = refs[3:3 + n_w]
    extra = refs[3 + n_w:3 + n_w + n_extra]
    outs = refs[3 + n_w + n_extra:3 + n_w + n_extra + n_out]
    h_ref = refs[-1]
    i = pl.program_id(0)
    j = pl.program_id(1)

    @pl.when(j == 0)
    def _():
        gain, shift = _mod_vectors(g_ref[...], mod_ref[...], shift_col, scale_col)

        def slab(r0):
            h = _norm_mod(x_ref[pl.ds(r0, ROW_CHUNK), :], gain, shift, i * tm + r0 >= s_lat)
            h_ref[pl.ds(r0, ROW_CHUNK), :] = h.astype(BF16)

        _for_rows(tm, ROW_CHUNK, slab)

    for applies, ep in epilogue:
        @pl.when(applies(j))
        def _(ep=ep):
            for r in range(tm // mc):
                rows = pl.ds(r * mc, mc)
                h = h_ref[rows, :]
                ep([_dot(h, w[...]) for w in w_refs], extra, outs, j, rows, h)


def _always(j):
    return j >= 0


def _norm_mod_matmul(t_arr, n_rows, s_lat, g, mod, shift_col, scale_col, w, w_col_blocks, tn,
                     n_tiles, epilogue, extra, extra_specs, out_shapes, out_specs, name):
    d = t_arr.shape[1]
    assert w.dtype == BF16
    tm = _pick(n_rows, 1408, ROW_CHUNK)
    mc = _pick(tm, MATMUL_ROWS, BF16_ROWS)
    w_specs = [pl.BlockSpec((d, tn), functools.partial(lambda i, j, o: (0, o + j), o=o))
               for o in w_col_blocks]
    kern = functools.partial(
        _nmm_kernel, n_w=len(w_col_blocks), n_extra=len(extra), n_out=len(out_shapes),
        s_lat=s_lat, tm=tm, mc=mc, shift_col=shift_col, scale_col=scale_col, epilogue=epilogue)
    return pl.pallas_call(
        kern,
        grid=(n_rows // tm, n_tiles),
        in_specs=[pl.BlockSpec((tm, d), lambda i, j: (i, 0)),
                  pl.BlockSpec((1, d), lambda i, j: (0, 0)),
                  pl.BlockSpec(mod.shape, lambda i, j: (0, 0))]
                 + w_specs + [s(tm) for s in extra_specs],
        out_specs=[s(tm) for s in out_specs],
        out_shape=out_shapes,
        scratch_shapes=[pltpu.VMEM((tm, d), BF16)],
        compiler_params=_params(("arbitrary", "arbitrary")),
        name=name,
    )(t_arr, g.reshape(1, d), mod, *([w] * len(w_col_blocks)), *extra)


def _gated_residual_rows(a_ref, w_ref, x_ref, mod_ref, o_ref, i, s_lat, tm, mc):
    mod = mod_ref[...]
    w = w_ref[...].astype(BF16)
    for r in range(tm // mc):
        rows = pl.ds(r * mc, mc)
        acc = _dot(a_ref[rows, :], w)
        tok = i * tm + r * mc + lax.broadcasted_iota(I32, (mc, 1), 0)
        gate = jnp.where(tok >= s_lat, mod[1:2, :], mod[0:1, :])
        o_ref[rows, :] = x_ref[rows, :] + gate * acc


def _mmres_kernel(a_ref, w_ref, x_ref, mod_ref, o_ref, *, s_lat, tm, mc):
    _gated_residual_rows(a_ref, w_ref, x_ref, mod_ref, o_ref, pl.program_id(0), s_lat, tm, mc)


def _matmul_residual(a, w, t_arr, n_rows, s_lat, mod, gate_col, name):
    k = a.shape[1]
    d = t_arr.shape[1]
    tm = _pick(n_rows, 1408, ROW_CHUNK)
    mc = _pick(tm, MATMUL_ROWS, BF16_ROWS)
    tn = _pick(d, 512, LANES)
    gblk = gate_col // tn
    return pl.pallas_call(
        functools.partial(_mmres_kernel, s_lat=s_lat, tm=tm, mc=mc),
        grid=(n_rows // tm, d // tn),
        in_specs=[pl.BlockSpec((tm, k), lambda i, j: (i, 0)),
                  pl.BlockSpec((k, tn), lambda i, j: (0, j)),
                  pl.BlockSpec((tm, tn), lambda i, j: (i, j)),
                  pl.BlockSpec((SUBLANES, tn), lambda i, j: (0, gblk + j))],
        out_specs=pl.BlockSpec((tm, tn), lambda i, j: (i, j)),
        out_shape=jax.ShapeDtypeStruct(t_arr.shape, F32),
        input_output_aliases={2: 0},
        compiler_params=_params(("arbitrary", "arbitrary")),
        name=name,
    )(a, w, t_arr, mod)


def _conv_in_epilogue(accs, extra, outs, j, rows, h):
    b, c, u = accs
    outs[0][rows, :] = b.astype(BF16)
    outs[1][rows, :] = c * u


def _conv_out_kernel(b_ref, v_ref, vp_ref, vn_ref, dw_ref, w_ref, x_ref, mod_ref, o_ref, a_ref, buf_ref,
                     *, s_lat, n_rows, tm, mc):
    i = pl.program_id(0)
    j = pl.program_id(1)

    @pl.when(j == 0)
    def _():
        buf_ref[0:SUBLANES, :] = vp_ref[...]
        buf_ref[SUBLANES + tm:2 * SUBLANES + tm, :] = vn_ref[...]

        def copy(r0):
            buf_ref[pl.ds(SUBLANES + r0, ROW_CHUNK), :] = v_ref[pl.ds(r0, ROW_CHUNK), :]

        _for_rows(tm, ROW_CHUNK, copy)
        dw = dw_ref[...]

        def slab(r0):
            win = buf_ref[pl.ds(r0, ROW_CHUNK + 2 * SUBLANES), :]
            prev = win[SUBLANES - 1:SUBLANES - 1 + ROW_CHUNK]
            cur = win[SUBLANES:SUBLANES + ROW_CHUNK]
            nxt = win[SUBLANES + 1:SUBLANES + 1 + ROW_CHUNK]
            rows = i * tm + r0 + lax.broadcasted_iota(I32, (ROW_CHUNK, 1), 0)
            has_prev = (rows != 0) & (rows != s_lat)
            has_next = (rows != s_lat - 1) & (rows != n_rows - 1)
            conv = (jnp.where(has_prev, prev, 0.0) * dw[0:1] + cur * dw[1:2]
                    + jnp.where(has_next, nxt, 0.0) * dw[2:3])
            a = b_ref[pl.ds(r0, ROW_CHUNK), :].astype(F32) * conv
            a_ref[pl.ds(r0, ROW_CHUNK), :] = a.astype(BF16)

        _for_rows(tm, ROW_CHUNK, slab)

    _gated_residual_rows(a_ref, w_ref, x_ref, mod_ref, o_ref, i, s_lat, tm, mc)


def _short_conv_layer(t_arr, n_rows, s_lat, mod, g, w_in, w_dw, w_out):
    nt, d = t_arr.shape
    w_in = w_in.astype(BF16)
    tn = _pick(d, 512, LANES)
    nblk = d // tn
    b, v = _norm_mod_matmul(
        t_arr, n_rows, s_lat, g, mod, 0, d, w_in, [0, nblk, 2 * nblk], tn, nblk,
        [(_always, _conv_in_epilogue)], [], [],
        [jax.ShapeDtypeStruct((n_rows, d), BF16), jax.ShapeDtypeStruct((n_rows, d), F32)],
        [lambda tm: pl.BlockSpec((tm, tn), lambda i, j: (i, j))] * 2, "conv_in")
    tm = _pick(n_rows, 1024, ROW_CHUNK)
    mc = _pick(tm, MATMUL_ROWS, BF16_ROWS)
    hb = tm // SUBLANES
    last = n_rows // SUBLANES - 1
    gblk = (2 * d) // tn
    return pl.pallas_call(
        functools.partial(_conv_out_kernel, s_lat=s_lat, n_rows=n_rows, tm=tm, mc=mc),
        grid=(n_rows // tm, nblk),
        in_specs=[pl.BlockSpec((tm, d), lambda i, j: (i, 0)),
                  pl.BlockSpec((tm, d), lambda i, j: (i, 0)),
                  pl.BlockSpec((SUBLANES, d), lambda i, j: (jnp.maximum(i * hb - 1, 0), 0)),
                  pl.BlockSpec((SUBLANES, d), lambda i, j: (jnp.minimum((i + 1) * hb, last), 0)),
                  pl.BlockSpec(w_dw.shape, lambda i, j: (0, 0)),
                  pl.BlockSpec((d, tn), lambda i, j: (0, j)),
                  pl.BlockSpec((tm, tn), lambda i, j: (i, j)),
                  pl.BlockSpec((SUBLANES, tn), lambda i, j: (0, gblk + j))],
        out_specs=pl.BlockSpec((tm, tn), lambda i, j: (i, j)),
        out_shape=jax.ShapeDtypeStruct((nt, d), F32),
        input_output_aliases={6: 0},
        scratch_shapes=[pltpu.VMEM((tm, d), BF16),
                        pltpu.VMEM((tm + 2 * SUBLANES, d), F32)],
        compiler_params=_params(("arbitrary", "arbitrary")),
        name="conv_out",
    )(b, v, v, v, w_dw, w_out, t_arr, mod)


def _rope_rot(a):
    half = DIFF_HEAD_DIM // 4
    lane = lax.broadcasted_iota(I32, a.shape, 1)
    up = pltpu.roll(a, LANES - half, axis=1)
    dn = pltpu.roll(a, half, axis=1)
    return jnp.where(lane % (2 * half) < half, up, dn)


def _rope_epilogue(accs, extra, outs, j, rows, h, *, n_q, tn):
    acc = accs[0]
    cos_ref, sin_ref = extra
    scale = jnp.where(j < n_q, DIFF_HEAD_DIM ** -0.5 * math.log2(math.e), 1.0).astype(F32)
    cos = cos_ref[rows, :] * scale
    sin = sin_ref[rows, :] * scale
    for gidx in range(tn // LANES):
        a = acc[:, gidx * LANES:(gidx + 1) * LANES]
        outs[0][rows, gidx * LANES:(gidx + 1) * LANES] = (a * cos + _rope_rot(a) * sin).astype(BF16)


def _cast_epilogue(accs, extra, outs, j, rows, h):
    outs[0][rows, :] = accs[0].astype(outs[0].dtype)


def _attn_body(q_ref, k_ref, v_ref, lam_ref, sub_ref, o_ref, k_lo, k_hi, lambda_init):
    hd = DIFF_HEAD_DIM
    tq = q_ref.shape[0]
    chunks = [slice(lo, min(lo + ATTN_KEYS, k_hi)) for lo in range(k_lo, k_hi, ATTN_KEYS)]
    m = jnp.full((2 * tq, 1), -jnp.inf, F32)
    l = jnp.zeros((2 * tq, 1), F32)
    acc = jnp.zeros((2 * tq, 2 * hd), F32)
    for sl in chunks:
        s = jnp.concatenate([_dot_nt(q_ref[:, t * hd:(t + 1) * hd], k_ref[sl, t * hd:(t + 1) * hd])
                             for t in range(2)], axis=0)
        m_new = jnp.maximum(m, jnp.max(s, axis=1, keepdims=True))
        alpha = jnp.exp2(m - m_new)
        p = jnp.exp2(s - m_new)
        l = alpha * l + jnp.sum(p, axis=1, keepdims=True)
        acc = alpha * acc + _dot(p.astype(BF16), v_ref[sl, :])
        m = m_new
    l0, l1, acc0, acc1 = l[:tq], l[tq:], acc[:tq], acc[tq:]
    lp = lam_ref[...]
    lam = (jnp.exp(jnp.sum(lp[0:1] * lp[1:2], axis=1, keepdims=True))
           - jnp.exp(jnp.sum(lp[2:3] * lp[3:4], axis=1, keepdims=True)) + lambda_init)
    o = acc0 * (1.0 / l0) - acc1 * (lam / l1)
    ms = jnp.mean(o * o, axis=1, keepdims=True)
    o = o * lax.rsqrt(ms + DIFF_SUBLN_EPS) * sub_ref[...] * (1.0 - lambda_init)
    o_ref[...] = o.astype(BF16)


def _attn_kernel(q_ref, k_ref, v_ref, lam_ref, sub_ref, o_ref, *, s_lat, lambda_init):
    tq = q_ref.shape[0]
    nt = k_ref.shape[0]
    n_lat_blocks = s_lat // tq
    i = pl.program_id(1)

    @pl.when(i < n_lat_blocks)
    def _():
        _attn_body(q_ref, k_ref, v_ref, lam_ref, sub_ref, o_ref, 0, nt, lambda_init)

    @pl.when(i >= n_lat_blocks)
    def _():
        _attn_body(q_ref, k_ref, v_ref, lam_ref, sub_ref, o_ref, s_lat, nt, lambda_init)


def _diff_attention(qkv, s_lat, lam_p, subln, lambda_init, d):
    nt = qkv.shape[0]
    hw = 2 * DIFF_HEAD_DIM
    heads = d // hw
    tq = ATTN_QUERIES
    assert s_lat % tq == 0 and (nt - s_lat) % tq == 0
    return pl.pallas_call(
        functools.partial(_attn_kernel, s_lat=s_lat, lambda_init=lambda_init),
        grid=(heads, nt // tq),
        in_specs=[pl.BlockSpec((tq, hw), lambda h, i: (i, h)),
                  pl.BlockSpec((nt, hw), lambda h, i: (0, heads + h)),
                  pl.BlockSpec((nt, hw), lambda h, i: (0, 2 * heads + h)),
                  pl.BlockSpec(lam_p.shape, lambda h, i: (0, 0)),
                  pl.BlockSpec((1, hw), lambda h, i: (0, 0))],
        out_specs=pl.BlockSpec((tq, hw), lambda h, i: (i, h)),
        out_shape=jax.ShapeDtypeStruct((nt, d), BF16),
        compiler_params=_params(("arbitrary", "arbitrary")),
        name="diff_attn",
    )(qkv, qkv, qkv, lam_p, subln.reshape(1, hw))


def _diff_layer(t_arr, s_lat, mod, g, w_qkv, lam_p, subln, w_out, lambda_init, cos_t, sin_t):
    nt, d = t_arr.shape
    w_qkv = w_qkv.astype(BF16)
    tn = _pick(d, 512, LANES)
    nblk = d // tn
    ep = [(lambda j: j < 2 * nblk, functools.partial(_rope_epilogue, n_q=nblk, tn=tn)),
          (lambda j: j >= 2 * nblk, _cast_epilogue)]
    (qkv,) = _norm_mod_matmul(
        t_arr, nt, s_lat, g, mod, 0, d, w_qkv, [0], tn, 3 * nblk, ep,
        [cos_t, sin_t], [lambda tm: pl.BlockSpec((tm, LANES), lambda i, j: (i, 0))] * 2,
        [jax.ShapeDtypeStruct((nt, 3 * d), BF16)],
        [lambda tm: pl.BlockSpec((tm, tn), lambda i, j: (i, j))], "diff_qkv")
    o = _diff_attention(qkv, s_lat, lam_p, subln, lambda_init, d)
    return _matmul_residual(o, w_out, t_arr, nt, s_lat, mod, 2 * d, "diff_out")


def _gla_proj_epilogue(accs, extra, outs, j, rows, h, *, n_q, dk, with_rank):
    scale = jnp.where(j < n_q, dk ** -0.5, 1.0).astype(F32)
    outs[0][rows, :] = accs[0] * scale
    if with_rank:
        outs[1][rows, :] = _dot(h, extra[0][...])


def _log_sigmoid(z):
    return jnp.minimum(z, 0.0) - jnp.log(1.0 + jnp.exp(-jnp.abs(z)))


def _gla_time(c, rev):
    rowi = lax.broadcasted_iota(I32, (c, 1), 0)
    coli = lax.broadcasted_iota(I32, (1, c), 1)
    return rowi, coli, ((c - 1 - rowi) if rev else rowi), ((c - 1 - coli) if rev else coli)


def _gla_decay_kernel(z_ref, w2_ref, bias_ref, bf_ref, bb_ref, *, c):
    rows = z_ref.shape[0]
    grp = _pick(rows, MXU_DIM, c)
    z_hi, z_lo = _split2(z_ref[...])
    ri = lax.broadcasted_iota(I32, (grp, grp), 0)
    ci = lax.broadcasted_iota(I32, (grp, grp), 1)
    same_chunk = (ri // c) == (ci // c)
    for t, (o_ref, rev) in enumerate(((bf_ref, False), (bb_ref, True))):
        w_hi, w_lo = _split2(w2_ref[t])
        z = _dot(z_hi, w_hi) + _dot(z_lo, w_hi) + _dot(z_hi, w_lo) + bias_ref[t]
        g = _log_sigmoid(z) * (1.0 / GLA_TAU)
        tri = (same_chunk & ((ci >= ri) if rev else (ci <= ri))).astype(BF16)
        for m in range(rows // grp):
            g1, g2, g3 = _split3(g[m * grp:(m + 1) * grp])
            o_ref[m * grp:(m + 1) * grp, :] = _dot(tri, g1) + _dot(tri, g2) + _dot(tri, g3)


def _gla_block_first(b, rev):
    c = b.shape[0]
    sub = min(GLA_FACTOR_ROWS, c)
    parts = []
    for m in range(c // sub):
        r = m * sub + (sub - 1 if rev else 0)
        parts.append(jnp.broadcast_to(b[r:r + 1, :], (sub, b.shape[1])))
    return jnp.concatenate(parts, axis=0)


def _gla_scores_factored(q, k, b, b_first, rev):
    c = q.shape[0]
    sub = min(GLA_FACTOR_ROWS, c)
    _, _, tau, tau_col = _gla_time(c, rev)
    qt = q * jnp.exp(b - b_first)
    q_parts, k_parts = [], []
    for blk in range(c // sub):
        rb = (c - 1 - sub * blk) if rev else sub * blk
        b_at = b[rb:rb + 1, :]
        in_blk = (tau // sub) == blk
        upto = tau < sub * (blk + 1)
        q_parts.append(jnp.where(in_blk, qt, 0.0).astype(BF16))
        k_parts.append(jnp.where(upto, k * jnp.exp(jnp.where(upto, b_at - b, 0.0)), 0.0).astype(BF16))
    a = _dot_nt(jnp.concatenate(q_parts, axis=1), jnp.concatenate(k_parts, axis=1))
    return jnp.where(tau_col <= tau, a, 0.0)


def _gla_scores_exact(q, k, b, kpad_ref, bpad_ref, rev):
    c = q.shape[0]
    sub = GLA_SUB
    rowi, coli, tau, _ = _gla_time(c, rev)
    kpad_ref[sub:sub + c, :] = k
    bpad_ref[sub:sub + c, :] = b
    a_mat = jnp.zeros((c, c), F32)
    for dlt in range(sub):
        off = sub + dlt if rev else sub - dlt
        k_sh = kpad_ref[off:off + c, :]
        b_sh = bpad_ref[off:off + c, :]
        valid = (tau % sub) >= dlt
        e = jnp.exp(jnp.where(valid, b - b_sh, 0.0))
        dsum = jnp.sum(q * k_sh * e, axis=1, keepdims=True)
        partner = (rowi + dlt) if rev else (rowi - dlt)
        a_mat = a_mat + jnp.where((coli == partner) & valid, dsum, 0.0)
    q_parts, k_parts = [], []
    for blk in range(1, c // sub):
        ref_row = (c - sub * blk) if rev else sub * blk - 1
        b_at = b[ref_row:ref_row + 1, :]
        in_blk = (tau // sub) == blk
        earlier = tau < sub * blk
        qt = jnp.where(in_blk, q * jnp.exp(jnp.where(in_blk, b - b_at, 0.0)), 0.0)
        kt = jnp.where(earlier, k * jnp.exp(jnp.where(earlier, b_at - b, 0.0)), 0.0)
        q_parts.append(qt.astype(BF16))
        k_parts.append(kt.astype(BF16))
    return a_mat + _dot_nt(jnp.concatenate(q_parts, axis=1), jnp.concatenate(k_parts, axis=1))


def _gla_chunk_output(q, k, v, b, a_mat, s_ref, rev):
    c = q.shape[0]
    rowi = lax.broadcasted_iota(I32, (c, 1), 0)
    end_row = 0 if rev else c - 1
    b_end = b[end_row:end_row + 1, :]
    vb = v.astype(BF16)
    s_old = s_ref[...]
    o = _dot(a_mat.astype(BF16), vb) + _dot((q * jnp.exp(b)).astype(BF16), s_old.astype(BF16))
    upd = _dot_tn((k * jnp.exp(b_end - b)).astype(BF16), vb)
    d1, d2, d3 = _split3(jnp.where(rowi == end_row, b, 0.0))
    ones = jnp.ones((c, LANES), BF16)
    decay_col = _dot_tn(d1, ones) + _dot_tn(d2, ones) + _dot_tn(d3, ones)
    s_ref[...] = jnp.exp(decay_col[:, 0:1]) * s_old + upd
    return o


def _gla_kernel(qf_ref, kf_ref, vf_ref, bf_ref, qb_ref, kb_ref, vb_ref, bb_ref,
                of_ref, ob_ref, s_ref, b1_ref, pad_ref, *, heads):
    @pl.when(pl.program_id(0) == 0)
    def _():
        s_ref[...] = jnp.zeros_like(s_ref)
        pad_ref[...] = jnp.zeros_like(pad_ref)

    dk = qf_ref.shape[1] // heads
    dv = vf_ref.shape[1] // heads
    dirs = ((qf_ref, kf_ref, vf_ref, bf_ref, of_ref, False), (qb_ref, kb_ref, vb_ref, bb_ref, ob_ref, True))
    excess = jnp.zeros((1, 1), F32)
    for t, (_, _, _, b_ref, _, rev) in enumerate(dirs):
        b = b_ref[...]
        b_first = _gla_block_first(b, rev)
        b1_ref[t] = b_first
        excess = jnp.maximum(excess, jnp.max(jnp.max(b_first - b, axis=1, keepdims=True), axis=0, keepdims=True))
    mild = jnp.max(excess) <= GLA_FACTOR_BOUND

    def run(exact):
        for t, (q_ref, k_ref, v_ref, b_ref, o_ref, rev) in enumerate(dirs):
            for h in range(heads):
                ks = slice(h * dk, (h + 1) * dk)
                vs = slice(h * dv, (h + 1) * dv)
                q, k, v, b = q_ref[:, ks], k_ref[:, ks], v_ref[:, vs], b_ref[:, ks]
                if exact:
                    a_mat = _gla_scores_exact(q, k, b, pad_ref.at[t, h, 0], pad_ref.at[t, h, 1], rev)
                else:
                    a_mat = _gla_scores_factored(q, k, b, b1_ref[t, :, ks], rev)
                o_ref[:, vs] = _gla_chunk_output(q, k, v, b, a_mat, s_ref.at[t, h], rev)

    pl.when(mild)(lambda: run(False))
    pl.when(jnp.logical_not(mild))(lambda: run(True))


def _gla_post_kernel(of_ref, ob_ref, g_ref, on_ref, a_ref, *, dv):
    o = of_ref[...] + ob_ref[...]
    gate = _silu(g_ref[...])
    for h in range(o.shape[1] // dv):
        oh = o[:, h * dv:(h + 1) * dv]
        ms = jnp.mean(oh * oh, axis=1, keepdims=True)
        y = oh * lax.rsqrt(ms + RMS_EPS) * on_ref[...] * gate[:, h * dv:(h + 1) * dv]
        a_ref[:, h * dv:(h + 1) * dv] = y.astype(BF16)


def _gla_layer(t_arr, s_lat, mod, g, w_in, gate_w1, gate_w2, gate_b, onorm, w_out):
    nt, d = t_arr.shape
    w_in = w_in.astype(BF16)
    cx = nt - s_lat
    heads = GLA_HEADS
    dk = d // (2 * heads)
    dv = d // heads
    rank = gate_w1.shape[2]
    n_proj = w_in.shape[1]
    tn = _pick(d, 512, LANES)
    nq_tiles = (heads * dk) // tn
    w1cat = jnp.concatenate([gate_w1[0], gate_w1[1]], axis=1)
    w1pad = jnp.pad(w1cat, ((0, 0), (0, LANES - 2 * rank))).astype(BF16)
    proj_ep = functools.partial(_gla_proj_epilogue, n_q=nq_tiles, dk=dk)
    proj, z1 = _norm_mod_matmul(
        t_arr, nt, s_lat, g, mod, 0, d, w_in, [0], tn, n_proj // tn,
        [(lambda j: j == 0, functools.partial(proj_ep, with_rank=True)),
         (lambda j: j > 0, functools.partial(proj_ep, with_rank=False))],
        [w1pad], [lambda tm: pl.BlockSpec((d, LANES), lambda i, j: (0, 0))],
        [jax.ShapeDtypeStruct((nt, n_proj), F32), jax.ShapeDtypeStruct((nt, LANES), F32)],
        [lambda tm: pl.BlockSpec((tm, tn), lambda i, j: (i, j)),
         lambda tm: pl.BlockSpec((tm, LANES), lambda i, j: (i, 0))], "gla_proj")
    w2pad = jnp.zeros((2, LANES, heads * dk), F32)
    w2pad = w2pad.at[0, 0:rank].set(gate_w2[0]).at[1, rank:2 * rank].set(gate_w2[1])
    bias = gate_b.reshape(2, 1, heads * dk)

    c = GLA_CHUNK
    n_lat, n_ctx = s_lat // c, cx // c
    nch = n_lat + n_ctx

    def fwd(s):
        return jnp.where(s < n_ctx, n_lat + s, s - n_ctx)

    def bwd(s):
        return nch - 1 - s

    tz = _pick(nt, 512, c)
    b_f, b_b = pl.pallas_call(
        functools.partial(_gla_decay_kernel, c=c),
        grid=(nt // tz,),
        in_specs=[pl.BlockSpec((tz, LANES), lambda i: (i, 0)),
                  pl.BlockSpec(w2pad.shape, lambda i: (0, 0, 0)),
                  pl.BlockSpec(bias.shape, lambda i: (0, 0, 0))],
        out_specs=[pl.BlockSpec((tz, heads * dk), lambda i: (i, 0))] * 2,
        out_shape=[jax.ShapeDtypeStruct((nt, heads * dk), F32)] * 2,
        compiler_params=_params(("arbitrary",)),
        name="gla_decay",
    )(z1, w2pad, bias)

    def specs(row_of):
        return [pl.BlockSpec((c, heads * dk), lambda s: (row_of(s), 0)),
                pl.BlockSpec((c, heads * dk), lambda s: (row_of(s), 1)),
                pl.BlockSpec((c, heads * dv), lambda s: (row_of(s), (2 * heads * dk) // (heads * dv))),
                pl.BlockSpec((c, heads * dk), lambda s: (row_of(s), 0))]

    pad_rows = c + 2 * GLA_SUB
    of, ob = pl.pallas_call(
        functools.partial(_gla_kernel, heads=heads),
        grid=(nch,),
        in_specs=specs(fwd) + specs(bwd),
        out_specs=[pl.BlockSpec((c, heads * dv), lambda s: (fwd(s), 0)),
                   pl.BlockSpec((c, heads * dv), lambda s: (bwd(s), 0))],
        out_shape=[jax.ShapeDtypeStruct((nt, d), F32)] * 2,
        scratch_shapes=[pltpu.VMEM((2, heads, dk, dv), F32),
                        pltpu.VMEM((2, c, heads * dk), F32),
                        pltpu.VMEM((2, heads, 2, pad_rows, dk), F32)],
        compiler_params=_params(("arbitrary",)),
        name="gla_scan",
    )(proj, proj, proj, b_f, proj, proj, proj, b_b)

    tm = _pick(nt, 512, ROW_CHUNK)
    gblk = (2 * heads * dk + heads * dv) // d
    a = pl.pallas_call(
        functools.partial(_gla_post_kernel, dv=dv),
        grid=(nt // tm,),
        in_specs=[pl.BlockSpec((tm, d), lambda i: (i, 0)),
                  pl.BlockSpec((tm, d), lambda i: (i, 0)),
                  pl.BlockSpec((tm, d), lambda i: (i, gblk)),
                  pl.BlockSpec((1, dv), lambda i: (0, 0))],
        out_specs=pl.BlockSpec((tm, d), lambda i: (i, 0)),
        out_shape=jax.ShapeDtypeStruct((nt, d), BF16),
        compiler_params=_params(("arbitrary",)),
        name="gla_post",
    )(of, ob, proj, onorm.reshape(1, dv))
    return _matmul_residual(a, w_out, t_arr, nt, s_lat, mod, 2 * d, "gla_out")


def _router_kernel(x_ref, g_ref, mod_ref, rw_ref, h_ref, aff_ref, *, s_lat, tm, n_exp, shift_col, scale_col):
    i = pl.program_id(0)
    gain, shift = _mod_vectors(g_ref[...], mod_ref[...], shift_col, scale_col)
    rw_hi, rw_lo = _split2(rw_ref[...])
    rw2 = (rw_hi.astype(F32) + pltpu.roll(rw_lo.astype(F32), n_exp, axis=1)).astype(BF16)

    def slab(r0):
        h = _norm_mod(x_ref[pl.ds(r0, ROW_CHUNK), :], gain, shift, i * tm + r0 >= s_lat)
        h_ref[pl.ds(r0, ROW_CHUNK), :] = h
        h_hi, h_lo = _split2(h)
        p = _dot(h_hi, rw2)
        p2 = _dot(h_lo, rw2)
        logits = p + pltpu.roll(p, LANES - n_exp, axis=1) + p2
        lt = jnp.transpose(logits)[0:n_exp, :]
        m = jnp.max(lt, axis=0, keepdims=True)
        e = jnp.exp(lt - m)
        aff_ref[r0 // ROW_CHUNK] = e / jnp.sum(e, axis=0, keepdims=True)

    _for_rows(tm, ROW_CHUNK, slab)


def _router(t_arr, n_rows, s_lat, g, mod, router_w):
    nt, d = t_arr.shape
    n_exp = router_w.shape[1]
    tm = _pick(n_rows, 1024, ROW_CHUNK)
    rw_pad = jnp.pad(router_w, ((0, 0), (0, LANES - n_exp)))
    hmod, aff = pl.pallas_call(
        functools.partial(_router_kernel, s_lat=s_lat, tm=tm, n_exp=n_exp,
                          shift_col=3 * d, scale_col=4 * d),
        grid=(n_rows // tm,),
        in_specs=[pl.BlockSpec((tm, d), lambda i: (i, 0)),
                  pl.BlockSpec((1, d), lambda i: (0, 0)),
                  pl.BlockSpec(mod.shape, lambda i: (0, 0)),
                  pl.BlockSpec((d, LANES), lambda i: (0, 0))],
        out_specs=[pl.BlockSpec((tm, d), lambda i: (i, 0)),
                   pl.BlockSpec((tm // ROW_CHUNK, n_exp, LANES), lambda i: (i, 0, 0))],
        out_shape=[jax.ShapeDtypeStruct((n_rows, d), F32),
                   jax.ShapeDtypeStruct((n_rows // ROW_CHUNK, n_exp, LANES), F32)],
        compiler_params=_params(("arbitrary",)),
        name="moe_router",
    )(t_arr, g.reshape(1, d), mod, rw_pad)
    return hmod, aff


def _select_kernel(aff_ref, idx_ref, gate_ref, *, cap):
    n_exp, nb, _ = aff_ref.shape
    aff = aff_ref[...]
    prefix = jnp.zeros((n_exp, 1, 1), I32)
    for bit in range(30, -1, -1):
        cand = prefix | (1 << bit)
        ge = aff >= lax.bitcast_convert_type(cand, F32)
        cnt = jnp.sum(jnp.sum(ge.astype(F32), axis=2, keepdims=True), axis=1, keepdims=True)
        prefix = jnp.where(cnt >= cap, cand, prefix)
    kth = lax.bitcast_convert_type(prefix, F32)
    gt = (aff > kth).astype(F32)
    eq = (aff == kth).astype(F32)
    n_gt = jnp.sum(jnp.sum(gt, axis=2, keepdims=True), axis=1, keepdims=True)
    need = cap - n_gt

    li = lax.broadcasted_iota(I32, (LANES, LANES), 0)
    lj = lax.broadcasted_iota(I32, (LANES, LANES), 1)
    upper = (li <= lj).astype(BF16)
    bi = lax.broadcasted_iota(I32, (nb, nb), 0)
    bj = lax.broadcasted_iota(I32, (nb, nb), 1)
    lower_strict = (bj < bi).astype(BF16)
    upper_nb = (bi <= bj).astype(BF16)
    ones_rows = jnp.ones((BF16_ROWS, LANES), BF16)
    pcol = lax.broadcasted_iota(I32, (cap, 1), 0).astype(F32)
    brow = lax.broadcasted_iota(I32, (1, nb), 1).astype(F32)
    lane_row = lax.broadcasted_iota(I32, (1, LANES), 1).astype(F32)

    for e in range(n_exp):
        eq_e = eq[e].astype(BF16)
        rank = _dot(eq_e, upper) + jnp.sum(_dot(lower_strict, eq_e), axis=1, keepdims=True)
        mask = jnp.maximum(gt[e], eq[e] * (rank <= need[e]).astype(F32))
        m_bf = mask.astype(BF16)
        lcs = _dot(m_bf, upper)
        tot = _dot_nt(ones_rows, m_bf)
        cb_row = _dot(tot.astype(BF16), upper_nb)[0:1, :]
        le = cb_row <= pcol
        blk = jnp.sum(le.astype(F32), axis=1, keepdims=True)
        lt = pcol - jnp.max(jnp.where(le, cb_row, 0.0), axis=1, keepdims=True)
        onehot = (brow == blk).astype(BF16)
        rowcs = _dot(onehot, lcs.astype(BF16))
        j = jnp.sum((rowcs <= lt).astype(F32), axis=1, keepdims=True)
        idx_ref[e] = (blk * LANES + j).astype(I32)
        a1, a2, a3 = _split3(aff[e])
        arow = _dot(onehot, a1) + _dot(onehot, a2) + _dot(onehot, a3)
        gate_ref[e] = jnp.sum(jnp.where(lane_row == j, arow, 0.0), axis=1, keepdims=True)


def _select(aff, cap):
    n_exp = aff.shape[0]
    return pl.pallas_call(
        functools.partial(_select_kernel, cap=cap),
        grid=(1,),
        in_specs=[pl.BlockSpec(aff.shape, lambda i: (0, 0, 0))],
        out_specs=[pl.BlockSpec((n_exp, cap, 1), lambda i: (0, 0, 0)),
                   pl.BlockSpec((n_exp, cap, 1), lambda i: (0, 0, 0))],
        out_shape=[jax.ShapeDtypeStruct((n_exp, cap, 1), I32),
                   jax.ShapeDtypeStruct((n_exp, cap, 1), F32)],
        compiler_params=_params(("arbitrary",)),
        name="moe_select",
    )(aff)


def _row_copy(hbm, buf, rows_ref, p, buf_tile, buf_sub, gather, sem):
    tiles_ref, subs_ref = rows_ref
    src = hbm.at[tiles_ref[0, 0, p], pl.ds(subs_ref[0, 0, p], 1), :]
    dst = buf.at[buf_tile, pl.ds(buf_sub, 1), :]
    if not gather:
        src, dst = dst, src
    return pltpu.make_async_copy(src, dst, sem)


def _start_row_copies(hbm, buf, rows_ref, sem, n_rows, gather):
    def body(p, carry):
        tile = lax.shift_right_logical(p, SUBLANES.bit_length() - 1)
        _row_copy(hbm, buf, rows_ref, p, tile, p & (SUBLANES - 1), gather, sem).start()
        return carry

    lax.fori_loop(0, n_rows, body, 0, unroll=8)


def _wait_row_copies(buf, sem):
    pltpu.make_async_copy(buf, buf, sem).wait()


def _ffn_kernel(tp_ref, sp_ref, tc_ref, sc_ref, tn_ref, sn_ref, gate_ref, h_hbm, x_hbm, wg_ref, wu_ref,
                wd_ref, mod_ref, o_hbm, xg_ref, xb_ref, y_ref, acc_ref, wgb_ref, wub_ref, wdb_ref, sem,
                *, k_lat, kt, nf, mc, n_exp, s_tiles, g_tiles, x_tiles):
    del x_hbm
    e = pl.program_id(0)
    f = pl.program_id(1)
    x_sem, acc_sem, out_sem = sem.at[0], sem.at[1], sem.at[2]
    rows_prev, rows_cur, rows_next = (tp_ref, sp_ref), (tc_ref, sc_ref), (tn_ref, sn_ref)
    d = xb_ref.shape[1]
    mc8 = mc // SUBLANES

    def tiles_of(r0):
        return pl.ds(pl.multiple_of(r0 // SUBLANES, mc8), mc8)

    @pl.when(f == 0)
    def _():
        @pl.when(e == 0)
        def _():
            _start_row_copies(h_hbm, xg_ref, rows_cur, x_sem, kt, True)
            _start_row_copies(o_hbm, acc_ref, rows_cur, acc_sem, kt, True)
            _wait_row_copies(acc_ref, acc_sem)

        _wait_row_copies(xg_ref, x_sem)

        def cast(r0):
            xb_ref[pl.ds(r0, mc), :] = xg_ref[tiles_of(r0)].reshape(mc, d).astype(BF16)

        _for_rows(kt, mc, cast)

    chunks = kt // mc
    streams = (
        (x_tiles, h_hbm, xg_ref, rows_next, True, x_sem),
        (s_tiles, o_hbm, acc_ref, rows_prev, False, out_sem),
        (g_tiles, o_hbm, acc_ref, rows_cur, True, acc_sem),
    )
    for fv in range(nf):
        jobs = []
        for stream in streams:
            tiles = stream[0]
            if fv in tiles:
                per_tile = kt // len(tiles)
                head = max(h for h in range(0, CAST_COPIES + 1, SUBLANES)
                           if (per_tile - h) % (chunks * SUBLANES) == 0)
                jobs.append((tiles.index(fv) * per_tile, head, (per_tile - head) // chunks, stream[1:]))

        @pl.when(f == fv)
        def _(fv=fv, jobs=jobs):
            if fv == g_tiles[0]:
                _wait_row_copies(acc_ref, out_sem)

            wgb_ref[...] = wg_ref[0, 0].astype(BF16)
            wub_ref[...] = wu_ref[0, 0].astype(BF16)
            wdb_ref[...] = wd_ref[0, 0].astype(BF16)
            for first, head, per, (hbm, buf, rows_ref, gather, sem_) in jobs:
                for k in range(head):
                    p = first + k
                    _row_copy(hbm, buf, rows_ref, p, p // SUBLANES, p % SUBLANES, gather, sem_).start()

            def ffn(r0):
                xb = xb_ref[pl.ds(r0, mc), :]
                a = _dot(xb, wgb_ref[...])
                u = _dot(xb, wub_ref[...])
                for first, head, per, (hbm, buf, rows_ref, gather, sem_) in jobs:
                    base = first + head + (r0 // mc) * per
                    base_tile = base // SUBLANES
                    for k in range(per):
                        _row_copy(hbm, buf, rows_ref, base + k, base_tile + k // SUBLANES, k % SUBLANES,
                                  gather, sem_).start()
                part = _dot((_silu(a) * u).astype(BF16), wdb_ref[...])
                if fv == 0:
                    y_ref[pl.ds(r0, mc), :] = part
                else:
                    y_ref[pl.ds(r0, mc), :] += part

            _for_rows(kt, mc, ffn)

    @pl.when(f == nf - 1)
    def _():
        _wait_row_copies(acc_ref, acc_sem)
        mod = mod_ref[...]

        def rmw(r0):
            rows = r0 + lax.broadcasted_iota(I32, (mc, 1), 0)
            g2 = jnp.where(rows >= k_lat, mod[1:2, :], mod[0:1, :])
            upd = g2 * (y_ref[pl.ds(r0, mc), :] * gate_ref[0, pl.ds(r0, mc), :])
            acc_ref[tiles_of(r0)] = acc_ref[tiles_of(r0)] + upd.reshape(mc8, SUBLANES, d)

        _for_rows(kt, mc, rmw)

        @pl.when(e == n_exp - 1)
        def _():
            _start_row_copies(o_hbm, acc_ref, rows_cur, out_sem, kt, False)
            _wait_row_copies(acc_ref, out_sem)
            _wait_row_copies(xg_ref, x_sem)


def _moe_ffn(t_arr, hmod, idx, gate, mod, w_gate, w_up, w_down, layer, k_lat):
    nt, d = t_arr.shape
    _, n_exp, _, ff = w_gate.shape
    kt = idx.shape[1]
    tf = _pick(ff, MXU_DIM, LANES)
    nf = ff // tf
    mc = _pick(kt, 528, BF16_ROWS)
    assert nf >= 3 and nt % SUBLANES == 0 and kt % SUBLANES == 0
    n_s = max(1, nf // 3)
    g0 = min(n_s + 1, nf - 2)
    s_tiles = tuple(range(n_s))
    g_tiles = tuple(range(g0, g0 + max(1, min(n_s, nf - 1 - g0))))
    x_tiles = tuple(t for t in range(nf) if t not in s_tiles + g_tiles)
    chunks = kt // mc
    assert all(kt % (len(t) * chunks) == 0 for t in (s_tiles, g_tiles, x_tiles))
    tile3 = lax.shift_right_logical(idx, SUBLANES.bit_length() - 1).reshape(n_exp, 1, kt)
    sub3 = (idx & (SUBLANES - 1)).reshape(n_exp, 1, kt)
    smem = functools.partial(pl.BlockSpec, (1, 1, kt), memory_space=pltpu.SMEM)
    prev_e = lambda e, f: (jnp.maximum(e - 1, 0), 0, 0)
    cur_e = lambda e, f: (e, 0, 0)
    next_e = lambda e, f: (jnp.minimum(e + 1, n_exp - 1), 0, 0)
    out = pl.pallas_call(
        functools.partial(_ffn_kernel, k_lat=k_lat, kt=kt, nf=nf, mc=mc, n_exp=n_exp,
                          s_tiles=s_tiles, g_tiles=g_tiles, x_tiles=x_tiles),
        grid=(n_exp, nf),
        in_specs=[smem(prev_e), smem(prev_e), smem(cur_e), smem(cur_e), smem(next_e), smem(next_e),
                  pl.BlockSpec((1, kt, 1), lambda e, f: (e, 0, 0)),
                  pl.BlockSpec(memory_space=pl.ANY),
                  pl.BlockSpec(memory_space=pl.ANY),
                  pl.BlockSpec((1, 1, d, tf), lambda e, f: (layer, e, 0, f)),
                  pl.BlockSpec((1, 1, d, tf), lambda e, f: (layer, e, 0, f)),
                  pl.BlockSpec((1, 1, tf, d), lambda e, f: (layer, e, f, 0)),
                  pl.BlockSpec((SUBLANES, d), lambda e, f: (0, 5))],
        out_specs=pl.BlockSpec(memory_space=pl.ANY),
        out_shape=jax.ShapeDtypeStruct((nt // SUBLANES, SUBLANES, d), F32),
        input_output_aliases={8: 0},
        scratch_shapes=[pltpu.VMEM((kt // SUBLANES, SUBLANES, d), F32), pltpu.VMEM((kt, d), BF16),
                        pltpu.VMEM((kt, d), F32), pltpu.VMEM((kt // SUBLANES, SUBLANES, d), F32),
                        pltpu.VMEM((d, tf), BF16), pltpu.VMEM((d, tf), BF16), pltpu.VMEM((tf, d), BF16),
                        pltpu.SemaphoreType.DMA((3,))],
        compiler_params=_params(("arbitrary", "arbitrary")),
        name="moe_ffn",
    )(tile3, sub3, tile3, sub3, tile3, sub3, gate, hmod.reshape(-1, SUBLANES, d),
      t_arr.reshape(nt // SUBLANES, SUBLANES, d), w_gate, w_up, w_down, mod)
    return out.reshape(nt, d)


def _pad_blocks(aff):
    nb = aff.shape[1]
    pad = (-nb) % BF16_ROWS
    if pad:
        aff = jnp.concatenate([aff, jnp.full((aff.shape[0], pad, LANES), -1.0, F32)], axis=1)
    return aff


def _moe_layer(t_arr, n_rows, s_lat, mod, g, router_w, w_gate, w_up, w_down, layer):
    n_exp = router_w.shape[1]
    hmod, aff = _router(t_arr, n_rows, s_lat, g, mod, router_w)
    aff = jnp.transpose(aff, (1, 0, 2))
    nb_lat = s_lat // LANES
    cap_lat = max(1, (CAPACITY_FACTOR * s_lat) // n_exp)
    idx, gate = _select(_pad_blocks(aff[:, :nb_lat]), cap_lat)
    idx = idx.reshape(n_exp, cap_lat)
    if n_rows > s_lat:
        cx = n_rows - s_lat
        cap_ctx = max(1, (CAPACITY_FACTOR * cx) // n_exp)
        idx_c, gate_c = _select(_pad_blocks(aff[:, nb_lat:]), cap_ctx)
        idx = jnp.concatenate([idx, idx_c.reshape(n_exp, cap_ctx) + s_lat], axis=1)
        gate = jnp.concatenate([gate, gate_c], axis=1)
    return _moe_ffn(t_arr, hmod, idx, gate, mod, w_gate, w_up, w_down, layer, cap_lat)


def _final_kernel(x_ref, g_ref, o_ref):
    x = x_ref[...]
    ms = jnp.mean(x * x, axis=-1, keepdims=True)
    o_ref[...] = x * lax.rsqrt(ms + RMS_EPS) * g_ref[...]


def _final_norm(t_arr, s_lat, g):
    d = t_arr.shape[1]
    tm = _pick(s_lat, 512, ROW_CHUNK)
    return pl.pallas_call(
        _final_kernel,
        grid=(s_lat // tm,),
        in_specs=[pl.BlockSpec((tm, d), lambda i: (i, 0)), pl.BlockSpec((1, d), lambda i: (0, 0))],
        out_specs=pl.BlockSpec((tm, d), lambda i: (i, 0)),
        out_shape=jax.ShapeDtypeStruct((s_lat, d), F32),
        compiler_params=_params(("arbitrary",)),
        name="final_norm",
    )(t_arr, g.reshape(1, d))


def kernel(x, c, ctx, c_ctx, mod_w, mod_b, norm_mix, norm_ffn, conv_w_in, conv_w_dw, conv_w_out,
           diff_w_qkv, diff_lambda, diff_subln, diff_w_out, gla_w_in, gla_gate_w1, gla_gate_w2,
           gla_gate_b, gla_onorm, gla_w_out, router_w, exp_w_gate, exp_w_up, exp_w_down, final_norm):
    batch, s_lat, d = x.shape
    cx = ctx.shape[1]
    depth = mod_w.shape[0]
    assert batch == 1 and s_lat % GRID_W == 0 and s_lat % ROW_CHUNK == 0
    nt = s_lat + cx
    t_arr = jnp.concatenate([x[0], ctx[0]], axis=0)
    cvec8 = jnp.concatenate([c, c_ctx[None, :], jnp.zeros((SUBLANES - 2, d), F32)], axis=0)
    mods = _modulation(cvec8, mod_w, mod_b)
    cos_t, sin_t = _rope_tables(nt, s_lat)

    for i in range(depth):
        kind, j = i % N_MIXERS, i // N_MIXERS
        ctx_next = i < depth - 1
        ctx_read = ctx_next or kind != 0
        n_rows = nt if ctx_read else s_lat
        if t_arr.shape[0] != n_rows:
            t_arr = t_arr[:n_rows]
        mod = mods[i]
        if kind == 0:
            t_arr = _short_conv_layer(t_arr, n_rows, s_lat, mod, norm_mix[i],
                                      conv_w_in[j], conv_w_dw[j], conv_w_out[j])
        elif kind == 1:
            lambda_init = DIFF_LAMBDA_A - DIFF_LAMBDA_B * math.exp(-DIFF_LAMBDA_C * i)
            t_arr = _diff_layer(t_arr, s_lat, mod, norm_mix[i], diff_w_qkv[j], diff_lambda[j],
                                diff_subln[j], diff_w_out[j], lambda_init, cos_t, sin_t)
        else:
            t_arr = _gla_layer(t_arr, s_lat, mod, norm_mix[i], gla_w_in[j], gla_gate_w1[j],
                               gla_gate_w2[j], gla_gate_b[j], gla_onorm[j], gla_w_out[j])
        n_moe = nt if ctx_next else s_lat
        if t_arr.shape[0] != n_moe:
            t_arr = t_arr[:n_moe]
        t_arr = _moe_layer(t_arr, n_moe, s_lat, mod, norm_ffn[i], router_w[i],
                           exp_w_gate, exp_w_up, exp_w_down, i)
    return _final_norm(t_arr, s_lat, final_norm)[None]
```

```python
import functools
import math

import jax
import jax.numpy as jnp
from jax import lax
from jax.experimental import pallas as pl
from jax.experimental.pallas import tpu as pltpu

F32 = jnp.float32
BF16 = jnp.bfloat16
I32 = jnp.int32

GRID_W = 64
N_MIXERS = 3
RMS_EPS = 1e-6
DIFF_HEAD_DIM = 128
DIFF_SUBLN_EPS = 1e-5
DIFF_LAMBDA_A = 0.8
DIFF_LAMBDA_B = 0.6
DIFF_LAMBDA_C = 0.3
ROPE_THETA = 10000.0
GLA_HEADS = 4
GLA_TAU = 16.0
GLA_CHUNK = 128
GLA_SUB = 16
GLA_FACTOR_ROWS = 128
GLA_FACTOR_BOUND = 60.0
CAPACITY_FACTOR = 2

LANES = 128
SUBLANES = 8
BF16_ROWS = 16
MXU_DIM = 256
VMEM_LIMIT_BYTES = 56 * 1024 * 1024
ROW_CHUNK = 128
MATMUL_ROWS = 768
ATTN_QUERIES = 256
ATTN_KEYS = 512
CAST_COPIES = 48
DMA_QUEUES = 2


def _params(sem, vmem=VMEM_LIMIT_BYTES):
    return pltpu.CompilerParams(dimension_semantics=sem, vmem_limit_bytes=vmem)


def _pick(n, cap, mult):
    best = None
    for d in range(mult, min(n, cap) + 1, mult):
        if n % d == 0:
            best = d
    assert best is not None, (n, cap, mult)
    return best


def _dot(a, b):
    return jnp.dot(a, b, preferred_element_type=F32)


def _dot_nt(a, b):
    return lax.dot_general(a, b, (((1,), (1,)), ((), ())), preferred_element_type=F32)


def _dot_tn(a, b):
    return lax.dot_general(a, b, (((0,), (0,)), ((), ())), preferred_element_type=F32)


def _split2(x):
    hi = x.astype(BF16)
    lo = (x - hi.astype(F32)).astype(BF16)
    return hi, lo


def _split3(x):
    a = x.astype(BF16)
    r = x - a.astype(F32)
    b = r.astype(BF16)
    c = (r - b.astype(F32)).astype(BF16)
    return a, b, c


def _sigmoid(x):
    return 1.0 / (1.0 + jnp.exp(-x))


def _silu(x):
    return x * _sigmoid(x)


def _for_rows(n_rows, chunk, fn):
    def body(r, carry):
        fn(pl.multiple_of(r * chunk, chunk))
        return carry

    lax.fori_loop(0, n_rows // chunk, body, 0)


def _mod_vectors(g, mod, shift_col, scale_col):
    d = g.shape[-1]
    return g * (1.0 + mod[0:2, scale_col:scale_col + d]), mod[0:2, shift_col:shift_col + d]


def _norm_mod(x, gain, shift, is_ctx):
    ms = jnp.mean(x * x, axis=-1, keepdims=True)
    gain = jnp.where(is_ctx, gain[1:2], gain[0:1])
    shift = jnp.where(is_ctx, shift[1:2], shift[0:1])
    return x * lax.rsqrt(ms + RMS_EPS) * gain + shift


def _mod_kernel(c_ref, w_ref, b_ref, o_ref):
    @pl.when(pl.program_id(1) == 0)
    def _():
        o_ref[0] = jnp.broadcast_to(b_ref[0], o_ref.shape[1:])

    s_hi, s_lo = _split2(_silu(c_ref[...]))
    w_hi, w_lo = _split2(w_ref[0])
    o_ref[0] += _dot(s_hi, w_hi) + _dot(s_lo, w_hi) + _dot(s_hi, w_lo)


def _modulation(cvec8, mod_w, mod_b):
    depth, d, n6 = mod_w.shape
    tk = _pick(d, 256, LANES)
    return pl.pallas_call(
        _mod_kernel,
        grid=(depth, d // tk),
        in_specs=[
            pl.BlockSpec((SUBLANES, tk), lambda l, k: (0, k)),
            pl.BlockSpec((1, tk, n6), lambda l, k: (l, k, 0)),
            pl.BlockSpec((1, 1, n6), lambda l, k: (l, 0, 0)),
        ],
        out_specs=pl.BlockSpec((1, SUBLANES, n6), lambda l, k: (l, 0, 0)),
        out_shape=jax.ShapeDtypeStruct((depth, SUBLANES, n6), F32),
        compiler_params=_params(("arbitrary", "arbitrary")),
        name="modulation",
    )(cvec8, mod_w, mod_b.reshape(depth, 1, n6))


def _rope_kernel(inv_ref, sgn_ref, cos_ref, sin_ref, *, s_lat, tm):
    i = pl.program_id(0)
    t = i * tm + lax.broadcasted_iota(I32, (tm, LANES), 0)
    lane = lax.broadcasted_iota(I32, (tm, LANES), 1)
    row = t // GRID_W
    col = t % GRID_W
    pos = jnp.where(lane < DIFF_HEAD_DIM // 2, row, col)
    pos = jnp.where(t < s_lat, pos, 0)
    ang = pos.astype(F32) * inv_ref[...]
    cos_ref[...] = jnp.cos(ang)
    sin_ref[...] = jnp.sin(ang) * sgn_ref[...]


def _rope_tables(nt, s_lat):
    quarter = DIFF_HEAD_DIM // 4
    inv = ROPE_THETA ** (-jnp.arange(quarter, dtype=F32) / quarter)
    inv128 = jnp.tile(inv, 4).reshape(1, LANES)
    sgn = jnp.tile(jnp.concatenate([-jnp.ones((quarter,), F32), jnp.ones((quarter,), F32)]), 2)
    tm = _pick(nt, 1024, LANES)
    return pl.pallas_call(
        functools.partial(_rope_kernel, s_lat=s_lat, tm=tm),
        grid=(nt // tm,),
        in_specs=[pl.BlockSpec((1, LANES), lambda i: (0, 0)),
                  pl.BlockSpec((1, LANES), lambda i: (0, 0))],
        out_specs=[pl.BlockSpec((tm, LANES), lambda i: (i, 0)),
                   pl.BlockSpec((tm, LANES), lambda i: (i, 0))],
        out_shape=[jax.ShapeDtypeStruct((nt, LANES), F32)] * 2,
        compiler_params=_params(("arbitrary",)),
        name="rope_tables",
    )(inv128, sgn.reshape(1, LANES))


def _nmm_kernel(*refs, n_w, n_extra, n_out, s_lat, tm, mc, shift_col, scale_col, epilogue):
    x_ref, g_ref, mod_ref = refs[:3]
    w_refs = refs[3:3 + n_w]
    extra = refs[3 + n_w:3 + n_w + n_extra]
    outs = refs[3 + n_w + n_extra:3 + n_w + n_extra + n_out]
    h_ref = refs[-1]
    i = pl.program_id(0)
    j = pl.program_id(1)

    @pl.when(j == 0)
    def _():
        gain, shift = _mod_vectors(g_ref[...], mod_ref[...], shift_col, scale_col)

        def slab(r0):
            h = _norm_mod(x_ref[pl.ds(r0, ROW_CHUNK), :], gain, shift, i * tm + r0 >= s_lat)
            h_ref[pl.ds(r0, ROW_CHUNK), :] = h.astype(BF16)

        _for_rows(tm, ROW_CHUNK, slab)

    for applies, ep in epilogue:
        @pl.when(applies(j))
        def _(ep=ep):
            for r in range(tm // mc):
                rows = pl.ds(r * mc, mc)
                h = h_ref[rows, :]
                ep([_dot(h, w[...]) for w in w_refs], extra, outs, j, rows, h)


def _always(j):
    return j >= 0


def _norm_mod_matmul(t_arr, n_rows, s_lat, g, mod, shift_col, scale_col, w, w_col_blocks, tn,
                     n_tiles, epilogue, extra, extra_specs, out_shapes, out_specs, name):
    d = t_arr.shape[1]
    assert w.dtype == BF16
    tm = _pick(n_rows, 1408, ROW_CHUNK)
    mc = _pick(tm, MATMUL_ROWS, BF16_ROWS)
    w_specs = [pl.BlockSpec((d, tn), functools.partial(lambda i, j, o: (0, o + j), o=o))
               for o in w_col_blocks]
    kern = functools.partial(
        _nmm_kernel, n_w=len(w_col_blocks), n_extra=len(extra), n_out=len(out_shapes),
        s_lat=s_lat, tm=tm, mc=mc, shift_col=shift_col, scale_col=scale_col, epilogue=epilogue)
    return pl.pallas_call(
        kern,
        grid=(n_rows // tm, n_tiles),
        in_specs=[pl.BlockSpec((tm, d), lambda i, j: (i, 0)),
                  pl.BlockSpec((1, d), lambda i, j: (0, 0)),
                  pl.BlockSpec(mod.shape, lambda i, j: (0, 0))]
                 + w_specs + [s(tm) for s in extra_specs],
        out_specs=[s(tm) for s in out_specs],
        out_shape=out_shapes,
        scratch_shapes=[pltpu.VMEM((tm, d), BF16)],
        compiler_params=_params(("arbitrary", "arbitrary")),
        name=name,
    )(t_arr, g.reshape(1, d), mod, *([w] * len(w_col_blocks)), *extra)


def _gated_residual_rows(a_ref, w_ref, x_ref, mod_ref, o_ref, i, s_lat, tm, mc):
    mod = mod_ref[...]
    w = w_ref[...].astype(BF16)
    for r in range(tm // mc):
        rows = pl.ds(r * mc, mc)
        acc = _dot(a_ref[rows, :], w)
        tok = i * tm + r * mc + lax.broadcasted_iota(I32, (mc, 1), 0)
        gate = jnp.where(tok >= s_lat, mod[1:2, :], mod[0:1, :])
        o_ref[rows, :] = x_ref[rows, :] + gate * acc


def _mmres_kernel(a_ref, w_ref, x_ref, mod_ref, o_ref, *, s_lat, tm, mc):
    _gated_residual_rows(a_ref, w_ref, x_ref, mod_ref, o_ref, pl.program_id(0), s_lat, tm, mc)


def _matmul_residual(a, w, t_arr, n_rows, s_lat, mod, gate_col, name):
    k = a.shape[1]
    d = t_arr.shape[1]
    tm = _pick(n_rows, 1408, ROW_CHUNK)
    mc = _pick(tm, MATMUL_ROWS, BF16_ROWS)
    tn = _pick(d, 512, LANES)
    gblk = gate_col // tn
    return pl.pallas_call(
        functools.partial(_mmres_kernel, s_lat=s_lat, tm=tm, mc=mc),
        grid=(n_rows // tm, d // tn),
        in_specs=[pl.BlockSpec((tm, k), lambda i, j: (i, 0)),
                  pl.BlockSpec((k, tn), lambda i, j: (0, j)),
                  pl.BlockSpec((tm, tn), lambda i, j: (i, j)),
                  pl.BlockSpec((SUBLANES, tn), lambda i, j: (0, gblk + j))],
        out_specs=pl.BlockSpec((tm, tn), lambda i, j: (i, j)),
        out_shape=jax.ShapeDtypeStruct(t_arr.shape, F32),
        input_output_aliases={2: 0},
        compiler_params=_params(("arbitrary", "arbitrary")),
        name=name,
    )(a, w, t_arr, mod)


def _conv_in_epilogue(accs, extra, outs, j, rows, h):
    b, c, u = accs
    outs[0][rows, :] = b.astype(BF16)
    outs[1][rows, :] = c * u


def _conv_out_kernel(b_ref, v_ref, vp_ref, vn_ref, dw_ref, w_ref, x_ref, mod_ref, o_ref, a_ref, buf_ref,
                     *, s_lat, n_rows, tm, mc):
    i = pl.program_id(0)
    j = pl.program_id(1)

    @pl.when(j == 0)
    def _():
        buf_ref[0:SUBLANES, :] = vp_ref[...]
        buf_ref[SUBLANES + tm:2 * SUBLANES + tm, :] = vn_ref[...]

        def copy(r0):
            buf_ref[pl.ds(SUBLANES + r0, ROW_CHUNK), :] = v_ref[pl.ds(r0, ROW_CHUNK), :]

        _for_rows(tm, ROW_CHUNK, copy)
        dw = dw_ref[...]

        def slab(r0):
            win = buf_ref[pl.ds(r0, ROW_CHUNK + 2 * SUBLANES), :]
            prev = win[SUBLANES - 1:SUBLANES - 1 + ROW_CHUNK]
            cur = win[SUBLANES:SUBLANES + ROW_CHUNK]
            nxt = win[SUBLANES + 1:SUBLANES + 1 + ROW_CHUNK]
            rows = i * tm + r0 + lax.broadcasted_iota(I32, (ROW_CHUNK, 1), 0)
            has_prev = (rows != 0) & (rows != s_lat)
            has_next = (rows != s_lat - 1) & (rows != n_rows - 1)
            conv = (jnp.where(has_prev, prev, 0.0) * dw[0:1] + cur * dw[1:2]
                    + jnp.where(has_next, nxt, 0.0) * dw[2:3])
            a = b_ref[pl.ds(r0, ROW_CHUNK), :].astype(F32) * conv
            a_ref[pl.ds(r0, ROW_CHUNK), :] = a.astype(BF16)

        _for_rows(tm, ROW_CHUNK, slab)

    _gated_residual_rows(a_ref, w_ref, x_ref, mod_ref, o_ref, i, s_lat, tm, mc)


def _short_conv_layer(t_arr, n_rows, s_lat, mod, g, w_in, w_dw, w_out):
    nt, d = t_arr.shape
    w_in = w_in.astype(BF16)
    tn = _pick(d, 512, LANES)
    nblk = d // tn
    b, v = _norm_mod_matmul(
        t_arr, n_rows, s_lat, g, mod, 0, d, w_in, [0, nblk, 2 * nblk], tn, nblk,
        [(_always, _conv_in_epilogue)], [], [],
        [jax.ShapeDtypeStruct((n_rows, d), BF16), jax.ShapeDtypeStruct((n_rows, d), F32)],
        [lambda tm: pl.BlockSpec((tm, tn), lambda i, j: (i, j))] * 2, "conv_in")
    tm = _pick(n_rows, 1024, ROW_CHUNK)
    mc = _pick(tm, MATMUL_ROWS, BF16_ROWS)
    hb = tm // SUBLANES
    last = n_rows // SUBLANES - 1
    gblk = (2 * d) // tn
    return pl.pallas_call(
        functools.partial(_conv_out_kernel, s_lat=s_lat, n_rows=n_rows, tm=tm, mc=mc),
        grid=(n_rows // tm, nblk),
        in_specs=[pl.BlockSpec((tm, d), lambda i, j: (i, 0)),
                  pl.BlockSpec((tm, d), lambda i, j: (i, 0)),
                  pl.BlockSpec((SUBLANES, d), lambda i, j: (jnp.maximum(i * hb - 1, 0), 0)),
                  pl.BlockSpec((SUBLANES, d), lambda i, j: (jnp.minimum((i + 1) * hb, last), 0)),
                  pl.BlockSpec(w_dw.shape, lambda i, j: (0, 0)),
                  pl.BlockSpec((d, tn), lambda i, j: (0, j)),
                  pl.BlockSpec((tm, tn), lambda i, j: (i, j)),
                  pl.BlockSpec((SUBLANES, tn), lambda i, j: (0, gblk + j))],
        out_specs=pl.BlockSpec((tm, tn), lambda i, j: (i, j)),
        out_shape=jax.ShapeDtypeStruct((nt, d), F32),
        input_output_aliases={6: 0},
        scratch_shapes=[pltpu.VMEM((tm, d), BF16),
                        pltpu.VMEM((tm + 2 * SUBLANES, d), F32)],
        compiler_params=_params(("arbitrary", "arbitrary")),
        name="conv_out",
    )(b, v, v, v, w_dw, w_out, t_arr, mod)


def _rope_rot(a):
    half = DIFF_HEAD_DIM // 4
    lane = lax.broadcasted_iota(I32, a.shape, 1)
    up = pltpu.roll(a, LANES - half, axis=1)
    dn = pltpu.roll(a, half, axis=1)
    return jnp.where(lane % (2 * half) < half, up, dn)


def _rope_epilogue(accs, extra, outs, j, rows, h, *, n_q, tn):
    acc = accs[0]
    cos_ref, sin_ref = extra
    scale = jnp.where(j < n_q, DIFF_HEAD_DIM ** -0.5 * math.log2(math.e), 1.0).astype(F32)
    cos = cos_ref[rows, :] * scale
    sin = sin_ref[rows, :] * scale
    for gidx in range(tn // LANES):
        a = acc[:, gidx * LANES:(gidx + 1) * LANES]
        outs[0][rows, gidx * LANES:(gidx + 1) * LANES] = (a * cos + _rope_rot(a) * sin).astype(BF16)


def _cast_epilogue(accs, extra, outs, j, rows, h):
    outs[0][rows, :] = accs[0].astype(outs[0].dtype)


def _attn_body(q_ref, k_ref, v_ref, lam_ref, sub_ref, o_ref, k_lo, k_hi, lambda_init):
    hd = DIFF_HEAD_DIM
    tq = q_ref.shape[0]
    chunks = [slice(lo, min(lo + ATTN_KEYS, k_hi)) for lo in range(k_lo, k_hi, ATTN_KEYS)]
    m = jnp.full((2 * tq, 1), -jnp.inf, F32)
    l = jnp.zeros((2 * tq, 1), F32)
    acc = jnp.zeros((2 * tq, 2 * hd), F32)
    for sl in chunks:
        s = jnp.concatenate([_dot_nt(q_ref[:, t * hd:(t + 1) * hd], k_ref[sl, t * hd:(t + 1) * hd])
                             for t in range(2)], axis=0)
        m_new = jnp.maximum(m, jnp.max(s, axis=1, keepdims=True))
        alpha = jnp.exp2(m - m_new)
        p = jnp.exp2(s - m_new)
        l = alpha * l + jnp.sum(p, axis=1, keepdims=True)
        acc = alpha * acc + _dot(p.astype(BF16), v_ref[sl, :])
        m = m_new
    l0, l1, acc0, acc1 = l[:tq], l[tq:], acc[:tq], acc[tq:]
    lp = lam_ref[...]
    lam = (jnp.exp(jnp.sum(lp[0:1] * lp[1:2], axis=1, keepdims=True))
           - jnp.exp(jnp.sum(lp[2:3] * lp[3:4], axis=1, keepdims=True)) + lambda_init)
    o = acc0 * (1.0 / l0) - acc1 * (lam / l1)
    ms = jnp.mean(o * o, axis=1, keepdims=True)
    o = o * lax.rsqrt(ms + DIFF_SUBLN_EPS) * sub_ref[...] * (1.0 - lambda_init)
    o_ref[...] = o.astype(BF16)


def _attn_kernel(q_ref, k_ref, v_ref, lam_ref, sub_ref, o_ref, *, s_lat, lambda_init):
    tq = q_ref.shape[0]
    nt = k_ref.shape[0]
    n_lat_blocks = s_lat // tq
    i = pl.program_id(1)

    @pl.when(i < n_lat_blocks)
    def _():
        _attn_body(q_ref, k_ref, v_ref, lam_ref, sub_ref, o_ref, 0, nt, lambda_init)

    @pl.when(i >= n_lat_blocks)
    def _():
        _attn_body(q_ref, k_ref, v_ref, lam_ref, sub_ref, o_ref, s_lat, nt, lambda_init)


def _diff_attention(qkv, s_lat, lam_p, subln, lambda_init, d):
    nt = qkv.shape[0]
    hw = 2 * DIFF_HEAD_DIM
    heads = d // hw
    tq = ATTN_QUERIES
    assert s_lat % tq == 0 and (nt - s_lat) % tq == 0
    return pl.pallas_call(
        functools.partial(_attn_kernel, s_lat=s_lat, lambda_init=lambda_init),
        grid=(heads, nt // tq),
        in_specs=[pl.BlockSpec((tq, hw), lambda h, i: (i, h)),
                  pl.BlockSpec((nt, hw), lambda h, i: (0, heads + h)),
                  pl.BlockSpec((nt, hw), lambda h, i: (0, 2 * heads + h)),
                  pl.BlockSpec(lam_p.shape, lambda h, i: (0, 0)),
                  pl.BlockSpec((1, hw), lambda h, i: (0, 0))],
        out_specs=pl.BlockSpec((tq, hw), lambda h, i: (i, h)),
        out_shape=jax.ShapeDtypeStruct((nt, d), BF16),
        compiler_params=_params(("arbitrary", "arbitrary")),
        name="diff_attn",
    )(qkv, qkv, qkv, lam_p, subln.reshape(1, hw))


def _diff_layer(t_arr, s_lat, mod, g, w_qkv, lam_p, subln, w_out, lambda_init, cos_t, sin_t):
    nt, d = t_arr.shape
    w_qkv = w_qkv.astype(BF16)
    tn = _pick(d, 512, LANES)
    nblk = d // tn
    ep = [(lambda j: j < 2 * nblk, functools.partial(_rope_epilogue, n_q=nblk, tn=tn)),
          (lambda j: j >= 2 * nblk, _cast_epilogue)]
    (qkv,) = _norm_mod_matmul(
        t_arr, nt, s_lat, g, mod, 0, d, w_qkv, [0], tn, 3 * nblk, ep,
        [cos_t, sin_t], [lambda tm: pl.BlockSpec((tm, LANES), lambda i, j: (i, 0))] * 2,
        [jax.ShapeDtypeStruct((nt, 3 * d), BF16)],
        [lambda tm: pl.BlockSpec((tm, tn), lambda i, j: (i, j))], "diff_qkv")
    o = _diff_attention(qkv, s_lat, lam_p, subln, lambda_init, d)
    return _matmul_residual(o, w_out, t_arr, nt, s_lat, mod, 2 * d, "diff_out")


def _gla_proj_epilogue(accs, extra, outs, j, rows, h, *, n_q, dk, with_rank):
    scale = jnp.where(j < n_q, dk ** -0.5, 1.0).astype(F32)
    outs[0][rows, :] = accs[0] * scale
    if with_rank:
        outs[1][rows, :] = _dot(h, extra[0][...])


def _log_sigmoid(z):
    return jnp.minimum(z, 0.0) - jnp.log(1.0 + jnp.exp(-jnp.abs(z)))


def _gla_time(c, rev):
    rowi = lax.broadcasted_iota(I32, (c, 1), 0)
    coli = lax.broadcasted_iota(I32, (1, c), 1)
    return rowi, coli, ((c - 1 - rowi) if rev else rowi), ((c - 1 - coli) if rev else coli)


def _gla_decay_kernel(z_ref, w2_ref, bias_ref, bf_ref, bb_ref, *, c):
    rows = z_ref.shape[0]
    grp = _pick(rows, MXU_DIM, c)
    z_hi, z_lo = _split2(z_ref[...])
    ri = lax.broadcasted_iota(I32, (grp, grp), 0)
    ci = lax.broadcasted_iota(I32, (grp, grp), 1)
    same_chunk = (ri // c) == (ci // c)
    for t, (o_ref, rev) in enumerate(((bf_ref, False), (bb_ref, True))):
        w_hi, w_lo = _split2(w2_ref[t])
        z = _dot(z_hi, w_hi) + _dot(z_lo, w_hi) + _dot(z_hi, w_lo) + bias_ref[t]
        g = _log_sigmoid(z) * (1.0 / GLA_TAU)
        tri = (same_chunk & ((ci >= ri) if rev else (ci <= ri))).astype(BF16)
        for m in range(rows // grp):
            g1, g2, g3 = _split3(g[m * grp:(m + 1) * grp])
            o_ref[m * grp:(m + 1) * grp, :] = _dot(tri, g1) + _dot(tri, g2) + _dot(tri, g3)


def _gla_block_first(b, rev):
    c = b.shape[0]
    sub = min(GLA_FACTOR_ROWS, c)
    parts = []
    for m in range(c // sub):
        r = m * sub + (sub - 1 if rev else 0)
        parts.append(jnp.broadcast_to(b[r:r + 1, :], (sub, b.shape[1])))
    return jnp.concatenate(parts, axis=0)


def _gla_scores_factored(q, k, b, b_first, rev):
    c = q.shape[0]
    sub = min(GLA_FACTOR_ROWS, c)
    _, _, tau, tau_col = _gla_time(c, rev)
    qt = q * jnp.exp(b - b_first)
    q_parts, k_parts = [], []
    for blk in range(c // sub):
        rb = (c - 1 - sub * blk) if rev else sub * blk
        b_at = b[rb:rb + 1, :]
        in_blk = (tau // sub) == blk
        upto = tau < sub * (blk + 1)
        q_parts.append(jnp.where(in_blk, qt, 0.0).astype(BF16))
        k_parts.append(jnp.where(upto, k * jnp.exp(jnp.where(upto, b_at - b, 0.0)), 0.0).astype(BF16))
    a = _dot_nt(jnp.concatenate(q_parts, axis=1), jnp.concatenate(k_parts, axis=1))
    return jnp.where(tau_col <= tau, a, 0.0)


def _gla_scores_exact(q, k, b, kpad_ref, bpad_ref, rev):
    c = q.shape[0]
    sub = GLA_SUB
    rowi, coli, tau, _ = _gla_time(c, rev)
    kpad_ref[sub:sub + c, :] = k
    bpad_ref[sub:sub + c, :] = b
    a_mat = jnp.zeros((c, c), F32)
    for dlt in range(sub):
        off = sub + dlt if rev else sub - dlt
        k_sh = kpad_ref[off:off + c, :]
        b_sh = bpad_ref[off:off + c, :]
        valid = (tau % sub) >= dlt
        e = jnp.exp(jnp.where(valid, b - b_sh, 0.0))
        dsum = jnp.sum(q * k_sh * e, axis=1, keepdims=True)
        partner = (rowi + dlt) if rev else (rowi - dlt)
        a_mat = a_mat + jnp.where((coli == partner) & valid, dsum, 0.0)
    q_parts, k_parts = [], []
    for blk in range(1, c // sub):
        ref_row = (c - sub * blk) if rev else sub * blk - 1
        b_at = b[ref_row:ref_row + 1, :]
        in_blk = (tau // sub) == blk
        earlier = tau < sub * blk
        qt = jnp.where(in_blk, q * jnp.exp(jnp.where(in_blk, b - b_at, 0.0)), 0.0)
        kt = jnp.where(earlier, k * jnp.exp(jnp.where(earlier, b_at - b, 0.0)), 0.0)
        q_parts.append(qt.astype(BF16))
        k_parts.append(kt.astype(BF16))
    return a_mat + _dot_nt(jnp.concatenate(q_parts, axis=1), jnp.concatenate(k_parts, axis=1))


def _gla_chunk_output(q, k, v, b, a_mat, s_ref, rev):
    c = q.shape[0]
    rowi = lax.broadcasted_iota(I32, (c, 1), 0)
    end_row = 0 if rev else c - 1
    b_end = b[end_row:end_row + 1, :]
    vb = v.astype(BF16)
    s_old = s_ref[...]
    o = _dot(a_mat.astype(BF16), vb) + _dot((q * jnp.exp(b)).astype(BF16), s_old.astype(BF16))
    upd = _dot_tn((k * jnp.exp(b_end - b)).astype(BF16), vb)
    d1, d2, d3 = _split3(jnp.where(rowi == end_row, b, 0.0))
    ones = jnp.ones((c, LANES), BF16)
    decay_col = _dot_tn(d1, ones) + _dot_tn(d2, ones) + _dot_tn(d3, ones)
    s_ref[...] = jnp.exp(decay_col[:, 0:1]) * s_old + upd
    return o


def _gla_kernel(qf_ref, kf_ref, vf_ref, bf_ref, qb_ref, kb_ref, vb_ref, bb_ref,
                of_ref, ob_ref, s_ref, b1_ref, pad_ref, *, heads):
    @pl.when(pl.program_id(0) == 0)
    def _():
        s_ref[...] = jnp.zeros_like(s_ref)
        pad_ref[...] = jnp.zeros_like(pad_ref)

    dk = qf_ref.shape[1] // heads
    dv = vf_ref.shape[1] // heads
    dirs = ((qf_ref, kf_ref, vf_ref, bf_ref, of_ref, False), (qb_ref, kb_ref, vb_ref, bb_ref, ob_ref, True))
    excess = jnp.zeros((1, 1), F32)
    for t, (_, _, _, b_ref, _, rev) in enumerate(dirs):
        b = b_ref[...]
        b_first = _gla_block_first(b, rev)
        b1_ref[t] = b_first
        excess = jnp.maximum(excess, jnp.max(jnp.max(b_first - b, axis=1, keepdims=True), axis=0, keepdims=True))
    mild = jnp.max(excess) <= GLA_FACTOR_BOUND

    def run(exact):
        for t, (q_ref, k_ref, v_ref, b_ref, o_ref, rev) in enumerate(dirs):
            for h in range(heads):
                ks = slice(h * dk, (h + 1) * dk)
                vs = slice(h * dv, (h + 1) * dv)
                q, k, v, b = q_ref[:, ks], k_ref[:, ks], v_ref[:, vs], b_ref[:, ks]
                if exact:
                    a_mat = _gla_scores_exact(q, k, b, pad_ref.at[t, h, 0], pad_ref.at[t, h, 1], rev)
                else:
                    a_mat = _gla_scores_factored(q, k, b, b1_ref[t, :, ks], rev)
                o_ref[:, vs] = _gla_chunk_output(q, k, v, b, a_mat, s_ref.at[t, h], rev)

    pl.when(mild)(lambda: run(False))
    pl.when(jnp.logical_not(mild))(lambda: run(True))


def _gla_post_kernel(of_ref, ob_ref, g_ref, on_ref, a_ref, *, dv):
    o = of_ref[...] + ob_ref[...]
    gate = _silu(g_ref[...])
    for h in range(o.shape[1] // dv):
        oh = o[:, h * dv:(h + 1) * dv]
        ms = jnp.mean(oh * oh, axis=1, keepdims=True)
        y = oh * lax.rsqrt(ms + RMS_EPS) * on_ref[...] * gate[:, h * dv:(h + 1) * dv]
        a_ref[:, h * dv:(h + 1) * dv] = y.astype(BF16)


def _gla_layer(t_arr, s_lat, mod, g, w_in, gate_w1, gate_w2, gate_b, onorm, w_out):
    nt, d = t_arr.shape
    w_in = w_in.astype(BF16)
    cx = nt - s_lat
    heads = GLA_HEADS
    dk = d // (2 * heads)
    dv = d // heads
    rank = gate_w1.shape[2]
    n_proj = w_in.shape[1]
    tn = _pick(d, 512, LANES)
    nq_tiles = (heads * dk) // tn
    w1cat = jnp.concatenate([gate_w1[0], gate_w1[1]], axis=1)
    w1pad = jnp.pad(w1cat, ((0, 0), (0, LANES - 2 * rank))).astype(BF16)
    proj_ep = functools.partial(_gla_proj_epilogue, n_q=nq_tiles, dk=dk)
    proj, z1 = _norm_mod_matmul(
        t_arr, nt, s_lat, g, mod, 0, d, w_in, [0], tn, n_proj // tn,
        [(lambda j: j == 0, functools.partial(proj_ep, with_rank=True)),
         (lambda j: j > 0, functools.partial(proj_ep, with_rank=False))],
        [w1pad], [lambda tm: pl.BlockSpec((d, LANES), lambda i, j: (0, 0))],
        [jax.ShapeDtypeStruct((nt, n_proj), F32), jax.ShapeDtypeStruct((nt, LANES), F32)],
        [lambda tm: pl.BlockSpec((tm, tn), lambda i, j: (i, j)),
         lambda tm: pl.BlockSpec((tm, LANES), lambda i, j: (i, 0))], "gla_proj")
    w2pad = jnp.zeros((2, LANES, heads * dk), F32)
    w2pad = w2pad.at[0, 0:rank].set(gate_w2[0]).at[1, rank:2 * rank].set(gate_w2[1])
    bias = gate_b.reshape(2, 1, heads * dk)

    c = GLA_CHUNK
    n_lat, n_ctx = s_lat // c, cx // c
    nch = n_lat + n_ctx

    def fwd(s):
        return jnp.where(s < n_ctx, n_lat + s, s - n_ctx)

    def bwd(s):
        return nch - 1 - s

    tz = _pick(nt, 512, c)
    b_f, b_b = pl.pallas_call(
        functools.partial(_gla_decay_kernel, c=c),
        grid=(nt // tz,),
        in_specs=[pl.BlockSpec((tz, LANES), lambda i: (i, 0)),
                  pl.BlockSpec(w2pad.shape, lambda i: (0, 0, 0)),
                  pl.BlockSpec(bias.shape, lambda i: (0, 0, 0))],
        out_specs=[pl.BlockSpec((tz, heads * dk), lambda i: (i, 0))] * 2,
        out_shape=[jax.ShapeDtypeStruct((nt, heads * dk), F32)] * 2,
        compiler_params=_params(("arbitrary",)),
        name="gla_decay",
    )(z1, w2pad, bias)

    def specs(row_of):
        return [pl.BlockSpec((c, heads * dk), lambda s: (row_of(s), 0)),
                pl.BlockSpec((c, heads * dk), lambda s: (row_of(s), 1)),
                pl.BlockSpec((c, heads * dv), lambda s: (row_of(s), (2 * heads * dk) // (heads * dv))),
                pl.BlockSpec((c, heads * dk), lambda s: (row_of(s), 0))]

    pad_rows = c + 2 * GLA_SUB
    of, ob = pl.pallas_call(
        functools.partial(_gla_kernel, heads=heads),
        grid=(nch,),
        in_specs=specs(fwd) + specs(bwd),
        out_specs=[pl.BlockSpec((c, heads * dv), lambda s: (fwd(s), 0)),
                   pl.BlockSpec((c, heads * dv), lambda s: (bwd(s), 0))],
        out_shape=[jax.ShapeDtypeStruct((nt, d), F32)] * 2,
        scratch_shapes=[pltpu.VMEM((2, heads, dk, dv), F32),
                        pltpu.VMEM((2, c, heads * dk), F32),
                        pltpu.VMEM((2, heads, 2, pad_rows, dk), F32)],
        compiler_params=_params(("arbitrary",)),
        name="gla_scan",
    )(proj, proj, proj, b_f, proj, proj, proj, b_b)

    tm = _pick(nt, 512, ROW_CHUNK)
    gblk = (2 * heads * dk + heads * dv) // d
    a = pl.pallas_call(
        functools.partial(_gla_post_kernel, dv=dv),
        grid=(nt // tm,),
        in_specs=[pl.BlockSpec((tm, d), lambda i: (i, 0)),
                  pl.BlockSpec((tm, d), lambda i: (i, 0)),
                  pl.BlockSpec((tm, d), lambda i: (i, gblk)),
                  pl.BlockSpec((1, dv), lambda i: (0, 0))],
        out_specs=pl.BlockSpec((tm, d), lambda i: (i, 0)),
        out_shape=jax.ShapeDtypeStruct((nt, d), BF16),
        compiler_params=_params(("arbitrary",)),
        name="gla_post",
    )(of, ob, proj, onorm.reshape(1, dv))
    return _matmul_residual(a, w_out, t_arr, nt, s_lat, mod, 2 * d, "gla_out")


def _router_kernel(x_ref, g_ref, mod_ref, rw_ref, h_ref, aff_ref, *, s_lat, tm, n_exp, shift_col, scale_col):
    i = pl.program_id(0)
    gain, shift = _mod_vectors(g_ref[...], mod_ref[...], shift_col, scale_col)
    rw_hi, rw_lo = _split2(rw_ref[...])
    rw2 = (rw_hi.astype(F32) + pltpu.roll(rw_lo.astype(F32), n_exp, axis=1)).astype(BF16)

    def slab(r0):
        h = _norm_mod(x_ref[pl.ds(r0, ROW_CHUNK), :], gain, shift, i * tm + r0 >= s_lat)
        h_ref[pl.ds(r0, ROW_CHUNK), :] = h
        h_hi, h_lo = _split2(h)
        p = _dot(h_hi, rw2)
        p2 = _dot(h_lo, rw2)
        logits = p + pltpu.roll(p, LANES - n_exp, axis=1) + p2
        lt = jnp.transpose(logits)[0:n_exp, :]
        m = jnp.max(lt, axis=0, keepdims=True)
        e = jnp.exp(lt - m)
        aff_ref[r0 // ROW_CHUNK] = e / jnp.sum(e, axis=0, keepdims=True)

    _for_rows(tm, ROW_CHUNK, slab)


def _router(t_arr, n_rows, s_lat, g, mod, router_w):
    nt, d = t_arr.shape
    n_exp = router_w.shape[1]
    tm = _pick(n_rows, 1024, ROW_CHUNK)
    rw_pad = jnp.pad(router_w, ((0, 0), (0, LANES - n_exp)))
    hmod, aff = pl.pallas_call(
        functools.partial(_router_kernel, s_lat=s_lat, tm=tm, n_exp=n_exp,
                          shift_col=3 * d, scale_col=4 * d),
        grid=(n_rows // tm,),
        in_specs=[pl.BlockSpec((tm, d), lambda i: (i, 0)),
                  pl.BlockSpec((1, d), lambda i: (0, 0)),
                  pl.BlockSpec(mod.shape, lambda i: (0, 0)),
                  pl.BlockSpec((d, LANES), lambda i: (0, 0))],
        out_specs=[pl.BlockSpec((tm, d), lambda i: (i, 0)),
                   pl.BlockSpec((tm // ROW_CHUNK, n_exp, LANES), lambda i: (i, 0, 0))],
        out_shape=[jax.ShapeDtypeStruct((n_rows, d), F32),
                   jax.ShapeDtypeStruct((n_rows // ROW_CHUNK, n_exp, LANES), F32)],
        compiler_params=_params(("arbitrary",)),
        name="moe_router",
    )(t_arr, g.reshape(1, d), mod, rw_pad)
    return hmod, aff


def _select_kernel(aff_ref, idx_ref, gate_ref, *, cap):
    n_exp, nb, _ = aff_ref.shape
    aff = aff_ref[...]
    prefix = jnp.zeros((n_exp, 1, 1), I32)
    for bit in range(30, -1, -1):
        cand = prefix | (1 << bit)
        ge = aff >= lax.bitcast_convert_type(cand, F32)
        cnt = jnp.sum(jnp.sum(ge.astype(F32), axis=2, keepdims=True), axis=1, keepdims=True)
        prefix = jnp.where(cnt >= cap, cand, prefix)
    kth = lax.bitcast_convert_type(prefix, F32)
    gt = (aff > kth).astype(F32)
    eq = (aff == kth).astype(F32)
    n_gt = jnp.sum(jnp.sum(gt, axis=2, keepdims=True), axis=1, keepdims=True)
    need = cap - n_gt

    li = lax.broadcasted_iota(I32, (LANES, LANES), 0)
    lj = lax.broadcasted_iota(I32, (LANES, LANES), 1)
    upper = (li <= lj).astype(BF16)
    bi = lax.broadcasted_iota(I32, (nb, nb), 0)
    bj = lax.broadcasted_iota(I32, (nb, nb), 1)
    lower_strict = (bj < bi).astype(BF16)
    upper_nb = (bi <= bj).astype(BF16)
    ones_rows = jnp.ones((BF16_ROWS, LANES), BF16)
    pcol = lax.broadcasted_iota(I32, (cap, 1), 0).astype(F32)
    brow = lax.broadcasted_iota(I32, (1, nb), 1).astype(F32)
    lane_row = lax.broadcasted_iota(I32, (1, LANES), 1).astype(F32)

    for e in range(n_exp):
        eq_e = eq[e].astype(BF16)
        rank = _dot(eq_e, upper) + jnp.sum(_dot(lower_strict, eq_e), axis=1, keepdims=True)
        mask = jnp.maximum(gt[e], eq[e] * (rank <= need[e]).astype(F32))
        m_bf = mask.astype(BF16)
        lcs = _dot(m_bf, upper)
        tot = _dot_nt(ones_rows, m_bf)
        cb_row = _dot(tot.astype(BF16), upper_nb)[0:1, :]
        le = cb_row <= pcol
        blk = jnp.sum(le.astype(F32), axis=1, keepdims=True)
        lt = pcol - jnp.max(jnp.where(le, cb_row, 0.0), axis=1, keepdims=True)
        onehot = (brow == blk).astype(BF16)
        rowcs = _dot(onehot, lcs.astype(BF16))
        j = jnp.sum((rowcs <= lt).astype(F32), axis=1, keepdims=True)
        idx_ref[e] = (blk * LANES + j).astype(I32)
        a1, a2, a3 = _split3(aff[e])
        arow = _dot(onehot, a1) + _dot(onehot, a2) + _dot(onehot, a3)
        gate_ref[e] = jnp.sum(jnp.where(lane_row == j, arow, 0.0), axis=1, keepdims=True)


def _select(aff, cap):
    n_exp = aff.shape[0]
    return pl.pallas_call(
        functools.partial(_select_kernel, cap=cap),
        grid=(1,),
        in_specs=[pl.BlockSpec(aff.shape, lambda i: (0, 0, 0))],
        out_specs=[pl.BlockSpec((n_exp, cap, 1), lambda i: (0, 0, 0)),
                   pl.BlockSpec((n_exp, cap, 1), lambda i: (0, 0, 0))],
        out_shape=[jax.ShapeDtypeStruct((n_exp, cap, 1), I32),
                   jax.ShapeDtypeStruct((n_exp, cap, 1), F32)],
        compiler_params=_params(("arbitrary",)),
        name="moe_select",
    )(aff)


def _row_copy(hbm, buf, rows_ref, p, buf_tile, buf_sub, gather, sem):
    tiles_ref, subs_ref = rows_ref
    src = hbm.at[tiles_ref[0, 0, p], pl.ds(subs_ref[0, 0, p], 1), :]
    dst = buf.at[buf_tile, pl.ds(buf_sub, 1), :]
    if not gather:
        src, dst = dst, src
    return pltpu.make_async_copy(src, dst, sem)


def _start_row_copies(hbm, buf, rows_ref, sem, n_rows, gather):
    def body(p, carry):
        tile = lax.shift_right_logical(p, SUBLANES.bit_length() - 1)
        _row_copy(hbm, buf, rows_ref, p, tile, p & (SUBLANES - 1), gather, sem).start()
        return carry

    lax.fori_loop(0, n_rows, body, 0, unroll=8)


def _wait_row_copies(buf, sem):
    pltpu.make_async_copy(buf, buf, sem).wait()


def _ffn_kernel(tp_ref, sp_ref, tc_ref, sc_ref, tn_ref, sn_ref, gate_ref, h_hbm, x_hbm, wg_ref, wu_ref,
                wd_ref, mod_ref, o_hbm, xg_ref, xb_ref, y_ref, acc_ref, wgb_ref, wub_ref, wdb_ref, sem,
                *, k_lat, kt, nf, mc, n_exp, s_tiles, g_tiles, x_tiles):
    del x_hbm
    e = pl.program_id(0)
    f = pl.program_id(1)
    x_sem, acc_sem, out_sem = sem.at[0], sem.at[1], sem.at[2]
    rows_prev, rows_cur, rows_next = (tp_ref, sp_ref), (tc_ref, sc_ref), (tn_ref, sn_ref)
    d = xb_ref.shape[1]
    mc8 = mc // SUBLANES

    def tiles_of(r0):
        return pl.ds(pl.multiple_of(r0 // SUBLANES, mc8), mc8)

    @pl.when(f == 0)
    def _():
        @pl.when(e == 0)
        def _():
            _start_row_copies(h_hbm, xg_ref, rows_cur, x_sem, kt, True)
            _start_row_copies(o_hbm, acc_ref, rows_cur, acc_sem, kt, True)
            _wait_row_copies(acc_ref, acc_sem)

        _wait_row_copies(xg_ref, x_sem)

        def cast(r0):
            xb_ref[pl.ds(r0, mc), :] = xg_ref[tiles_of(r0)].reshape(mc, d).astype(BF16)

        _for_rows(kt, mc, cast)

    chunks = kt // mc
    streams = (
        (x_tiles, h_hbm, xg_ref, rows_next, True, x_sem),
        (s_tiles, o_hbm, acc_ref, rows_prev, False, out_sem),
        (g_tiles, o_hbm, acc_ref, rows_cur, True, acc_sem),
    )
    for fv in range(nf):
        jobs = []
        for stream in streams:
            tiles = stream[0]
            if fv in tiles:
                per_tile = kt // len(tiles)
                head = max(h for h in range(0, CAST_COPIES + 1, SUBLANES)
                           if (per_tile - h) % (chunks * SUBLANES) == 0)
                jobs.append((tiles.index(fv) * per_tile, head, (per_tile - head) // chunks, stream[1:]))

        @pl.when(f == fv)
        def _(fv=fv, jobs=jobs):
            if fv == g_tiles[0]:
                _wait_row_copies(acc_ref, out_sem)

            wgb_ref[...] = wg_ref[0, 0].astype(BF16)
            wub_ref[...] = wu_ref[0, 0].astype(BF16)
            wdb_ref[...] = wd_ref[0, 0].astype(BF16)
            for first, head, per, (hbm, buf, rows_ref, gather, sem_) in jobs:
                for k in range(head):
                    p = first + k
                    _row_copy(hbm, buf, rows_ref, p, p // SUBLANES, p % SUBLANES, gather, sem_).start(
                        priority=k % DMA_QUEUES)

            def ffn(r0):
                xb = xb_ref[pl.ds(r0, mc), :]
                a = _dot(xb, wgb_ref[...])
                u = _dot(xb, wub_ref[...])
                for first, head, per, (hbm, buf, rows_ref, gather, sem_) in jobs:
                    base = first + head + (r0 // mc) * per
                    base_tile = base // SUBLANES
                    for k in range(per):
                        _row_copy(hbm, buf, rows_ref, base + k, base_tile + k // SUBLANES, k % SUBLANES,
                                  gather, sem_).start(priority=k % DMA_QUEUES)
                part = _dot((_silu(a) * u).astype(BF16), wdb_ref[...])
                if fv == 0:
                    y_ref[pl.ds(r0, mc), :] = part
                else:
                    y_ref[pl.ds(r0, mc), :] += part

            _for_rows(kt, mc, ffn)

    @pl.when(f == nf - 1)
    def _():
        _wait_row_copies(acc_ref, acc_sem)
        mod = mod_ref[...]

        def rmw(r0):
            rows = r0 + lax.broadcasted_iota(I32, (mc, 1), 0)
            g2 = jnp.where(rows >= k_lat, mod[1:2, :], mod[0:1, :])
            upd = g2 * (y_ref[pl.ds(r0, mc), :] * gate_ref[0, pl.ds(r0, mc), :])
            acc_ref[tiles_of(r0)] = acc_ref[tiles_of(r0)] + upd.reshape(mc8, SUBLANES, d)

        _for_rows(kt, mc, rmw)

        @pl.when(e == n_exp - 1)
        def _():
            _start_row_copies(o_hbm, acc_ref, rows_cur, out_sem, kt, False)
            _wait_row_copies(acc_ref, out_sem)
            _wait_row_copies(xg_ref, x_sem)


def _moe_ffn(t_arr, hmod, idx, gate, mod, w_gate, w_up, w_down, layer, k_lat):
    nt, d = t_arr.shape
    _, n_exp, _, ff = w_gate.shape
    kt = idx.shape[1]
    tf = _pick(ff, MXU_DIM, LANES)
    nf = ff // tf
    mc = _pick(kt, 528, BF16_ROWS)
    assert nf >= 3 and nt % SUBLANES == 0 and kt % SUBLANES == 0
    n_s = max(1, nf // 3)
    g0 = min(n_s + 1, nf - 2)
    s_tiles = tuple(range(n_s))
    g_tiles = tuple(range(g0, g0 + max(1, min(n_s, nf - 1 - g0))))
    x_tiles = tuple(t for t in range(nf) if t not in s_tiles + g_tiles)
    chunks = kt // mc
    assert all(kt % (len(t) * chunks) == 0 for t in (s_tiles, g_tiles, x_tiles))
    tile3 = lax.shift_right_logical(idx, SUBLANES.bit_length() - 1).reshape(n_exp, 1, kt)
    sub3 = (idx & (SUBLANES - 1)).reshape(n_exp, 1, kt)
    smem = functools.partial(pl.BlockSpec, (1, 1, kt), memory_space=pltpu.SMEM)
    prev_e = lambda e, f: (jnp.maximum(e - 1, 0), 0, 0)
    cur_e = lambda e, f: (e, 0, 0)
    next_e = lambda e, f: (jnp.minimum(e + 1, n_exp - 1), 0, 0)
    out = pl.pallas_call(
        functools.partial(_ffn_kernel, k_lat=k_lat, kt=kt, nf=nf, mc=mc, n_exp=n_exp,
                          s_tiles=s_tiles, g_tiles=g_tiles, x_tiles=x_tiles),
        grid=(n_exp, nf),
        in_specs=[smem(prev_e), smem(prev_e), smem(cur_e), smem(cur_e), smem(next_e), smem(next_e),
                  pl.BlockSpec((1, kt, 1), lambda e, f: (e, 0, 0)),
                  pl.BlockSpec(memory_space=pl.ANY),
                  pl.BlockSpec(memory_space=pl.ANY),
                  pl.BlockSpec((1, 1, d, tf), lambda e, f: (layer, e, 0, f)),
                  pl.BlockSpec((1, 1, d, tf), lambda e, f: (layer, e, 0, f)),
                  pl.BlockSpec((1, 1, tf, d), lambda e, f: (layer, e, f, 0)),
                  pl.BlockSpec((SUBLANES, d), lambda e, f: (0, 5))],
        out_specs=pl.BlockSpec(memory_space=pl.ANY),
        out_shape=jax.ShapeDtypeStruct((nt // SUBLANES, SUBLANES, d), F32),
        input_output_aliases={8: 0},
        scratch_shapes=[pltpu.VMEM((kt // SUBLANES, SUBLANES, d), F32), pltpu.VMEM((kt, d), BF16),
                        pltpu.VMEM((kt, d), F32), pltpu.VMEM((kt // SUBLANES, SUBLANES, d), F32),
                        pltpu.VMEM((d, tf), BF16), pltpu.VMEM((d, tf), BF16), pltpu.VMEM((tf, d), BF16),
                        pltpu.SemaphoreType.DMA((3,))],
        compiler_params=_params(("arbitrary", "arbitrary")),
        name="moe_ffn",
    )(tile3, sub3, tile3, sub3, tile3, sub3, gate, hmod.reshape(-1, SUBLANES, d),
      t_arr.reshape(nt // SUBLANES, SUBLANES, d), w_gate, w_up, w_down, mod)
    return out.reshape(nt, d)


def _pad_blocks(aff):
    nb = aff.shape[1]
    pad = (-nb) % BF16_ROWS
    if pad:
        aff = jnp.concatenate([aff, jnp.full((aff.shape[0], pad, LANES), -1.0, F32)], axis=1)
    return aff


def _moe_layer(t_arr, n_rows, s_lat, mod, g, router_w, w_gate, w_up, w_down, layer):
    n_exp = router_w.shape[1]
    hmod, aff = _router(t_arr, n_rows, s_lat, g, mod, router_w)
    aff = jnp.transpose(aff, (1, 0, 2))
    nb_lat = s_lat // LANES
    cap_lat = max(1, (CAPACITY_FACTOR * s_lat) // n_exp)
    idx, gate = _select(_pad_blocks(aff[:, :nb_lat]), cap_lat)
    idx = idx.reshape(n_exp, cap_lat)
    if n_rows > s_lat:
        cx = n_rows - s_lat
        cap_ctx = max(1, (CAPACITY_FACTOR * cx) // n_exp)
        idx_c, gate_c = _select(_pad_blocks(aff[:, nb_lat:]), cap_ctx)
        idx = jnp.concatenate([idx, idx_c.reshape(n_exp, cap_ctx) + s_lat], axis=1)
        gate = jnp.concatenate([gate, gate_c], axis=1)
    return _moe_ffn(t_arr, hmod, idx, gate, mod, w_gate, w_up, w_down, layer, cap_lat)


def _final_kernel(x_ref, g_ref, o_ref):
    x = x_ref[...]
    ms = jnp.mean(x * x, axis=-1, keepdims=True)
    o_ref[...] = x * lax.rsqrt(ms + RMS_EPS) * g_ref[...]


def _final_norm(t_arr, s_lat, g):
    d = t_arr.shape[1]
    tm = _pick(s_lat, 512, ROW_CHUNK)
    return pl.pallas_call(
        _final_kernel,
        grid=(s_lat // tm,),
        in_specs=[pl.BlockSpec((tm, d), lambda i: (i, 0)), pl.BlockSpec((1, d), lambda i: (0, 0))],
        out_specs=pl.BlockSpec((tm, d), lambda i: (i, 0)),
        out_shape=jax.ShapeDtypeStruct((s_lat, d), F32),
        compiler_params=_params(("arbitrary",)),
        name="final_norm",
    )(t_arr, g.reshape(1, d))


def kernel(x, c, ctx, c_ctx, mod_w, mod_b, norm_mix, norm_ffn, conv_w_in, conv_w_dw, conv_w_out,
           diff_w_qkv, diff_lambda, diff_subln, diff_w_out, gla_w_in, gla_gate_w1, gla_gate_w2,
           gla_gate_b, gla_onorm, gla_w_out, router_w, exp_w_gate, exp_w_up, exp_w_down, final_norm):
    batch, s_lat, d = x.shape
    cx = ctx.shape[1]
    depth = mod_w.shape[0]
    assert batch == 1 and s_lat % GRID_W == 0 and s_lat % ROW_CHUNK == 0
    nt = s_lat + cx
    t_arr = jnp.concatenate([x[0], ctx[0]], axis=0)
    cvec8 = jnp.concatenate([c, c_ctx[None, :], jnp.zeros((SUBLANES - 2, d), F32)], axis=0)
    mods = _modulation(cvec8, mod_w, mod_b)
    cos_t, sin_t = _rope_tables(nt, s_lat)

    for i in range(depth):
        kind, j = i % N_MIXERS, i // N_MIXERS
        ctx_next = i < depth - 1
        ctx_read = ctx_next or kind != 0
        n_rows = nt if ctx_read else s_lat
        if t_arr.shape[0] != n_rows:
            t_arr = t_arr[:n_rows]
        mod = mods[i]
        if kind == 0:
            t_arr = _short_conv_layer(t_arr, n_rows, s_lat, mod, norm_mix[i],
                                      conv_w_in[j], conv_w_dw[j], conv_w_out[j])
        elif kind == 1:
            lambda_init = DIFF_LAMBDA_A - DIFF_LAMBDA_B * math.exp(-DIFF_LAMBDA_C * i)
            t_arr = _diff_layer(t_arr, s_lat, mod, norm_mix[i], diff_w_qkv[j], diff_lambda[j],
                                diff_subln[j], diff_w_out[j], lambda_init, cos_t, sin_t)
        else:
            t_arr = _gla_layer(t_arr, s_lat, mod, norm_mix[i], gla_w_in[j], gla_gate_w1[j],
                               gla_gate_w2[j], gla_gate_b[j], gla_onorm[j], gla_w_out[j])
        n_moe = nt if ctx_next else s_lat
        if t_arr.shape[0] != n_moe:
            t_arr = t_arr[:n_moe]
        t_arr = _moe_layer(t_arr, n_moe, s_lat, mod, norm_ffn[i], router_w[i],
                           exp_w_gate, exp_w_up, exp_w_down, i)
    return _final_norm(t_arr, s_lat, final_norm)[None]
```
